```python
import math
import jax, jax.numpy as jnp
from jax import lax
import numpy as np

D_MODEL = 4096
BATCH = 8
SEQ = 2048
DEPTH = 2

MIX_WIDTH = D_MODEL
S5_WIDTH = MIX_WIDTH // 2
S5_GROUP = 16
S5_GROUPS = S5_WIDTH // S5_GROUP
S5_STATE = 64
S5_DT_MIN = 1e-3
S5_DT_MAX = 1e-1
GLA_WIDTH = MIX_WIDTH - S5_WIDTH
GLA_HEADS = 8
GLA_DV = GLA_WIDTH // GLA_HEADS
GLA_DK = GLA_DV // 2
GLA_KEY_WIDTH = GLA_HEADS * GLA_DK
GLA_GATE_RANK = 16
GLA_TAU = 16.0
HG_EXPAND = 128
HG_HEADS = MIX_WIDTH // HG_EXPAND
HG_DK = HG_EXPAND
HG_DV = MIX_WIDTH // HG_HEADS
HG_KEY_WIDTH = HG_HEADS * HG_DK
HG_VAL_WIDTH = HG_HEADS * HG_DV
CHUNK = 64
FFN_HIDDEN = -(-8 * D_MODEL // (3 * 256)) * 256
N_EVEN = (DEPTH + 1) // 2
N_ODD = DEPTH // 2
DEEPNORM_ALPHA = (2.0 * DEPTH) ** 0.25
DEEPNORM_BETA = (8.0 * DEPTH) ** -0.25
NORM_EPS = 1e-5
EVEN_SPLITS = (S5_WIDTH,
               S5_WIDTH + GLA_KEY_WIDTH,
               S5_WIDTH + 2 * GLA_KEY_WIDTH,
               S5_WIDTH + 2 * GLA_KEY_WIDTH + GLA_WIDTH,
               S5_WIDTH + 2 * GLA_KEY_WIDTH + 2 * GLA_WIDTH)
EVEN_IN = S5_WIDTH + 2 * GLA_KEY_WIDTH + 2 * GLA_WIDTH + GLA_GATE_RANK
ODD_SPLITS = (HG_KEY_WIDTH, 2 * HG_KEY_WIDTH, 2 * HG_KEY_WIDTH + HG_VAL_WIDTH)
ODD_IN = 2 * HG_KEY_WIDTH + HG_VAL_WIDTH + MIX_WIDTH

kernel_name = "hybrid_s5_gla_hgrn2_deepnorm"


def layer_norm(x, g, b):
    x32 = x.astype(jnp.float32)
    mu = jnp.mean(x32, axis=-1, keepdims=True)
    var = jnp.mean(jnp.square(x32 - mu), axis=-1, keepdims=True)
    return ((x32 - mu) * lax.rsqrt(var + NORM_EPS) * g + b).astype(x.dtype)


def gated_head_rmsnorm(o, gain, gate):
    o32 = o.astype(jnp.float32)
    o32 = o32 * lax.rsqrt(jnp.mean(jnp.square(o32), axis=-1, keepdims=True) + NORM_EPS) * gain
    return o32.reshape(gate.shape) * jax.nn.silu(gate.astype(jnp.float32))


def chunked_gated_linear_recurrence(q, k, v, log_a):
    bsz, t, h, dk = q.shape
    dv = v.shape[-1]
    n = t // CHUNK
    to_chunks = lambda a: a.astype(jnp.float32).reshape(bsz, n, CHUNK, h, a.shape[-1])
    q, k, v, log_a = to_chunks(q), to_chunks(k), to_chunks(v), to_chunks(log_a)
    b = jnp.cumsum(log_a, axis=2)
    b_end = b[:, :, -1:]
    q_dec = q * jnp.exp(b)
    k_inv = k * jnp.exp(-b)
    causal = jnp.tril(jnp.ones((CHUNK, CHUNK), dtype=bool))
    scores = jnp.einsum('bnlhd,bnmhd->bnhlm', q_dec, k_inv)
    scores = jnp.where(causal, scores, 0.0)
    o_intra = jnp.einsum('bnhlm,bnmhv->bnlhv', scores, v)
    k_end = k * jnp.exp(b_end - b)
    chunk_kv = jnp.einsum('bnlhd,bnlhv->bnhdv', k_end, v)
    chunk_decay = jnp.exp(b_end[:, :, 0])

    def step(state, inp):
        dec, kv = inp
        return dec[..., None] * state + kv, state

    s0 = jnp.zeros((bsz, h, dk, dv), jnp.float32)
    _, s_prev = lax.scan(step, s0, (jnp.moveaxis(chunk_decay, 1, 0), jnp.moveaxis(chunk_kv, 1, 0)))
    s_prev = jnp.moveaxis(s_prev, 0, 1)
    o_inter = jnp.einsum('bnlhd,bnhdv->bnlhv', q_dec, s_prev)
    return (o_intra + o_inter).reshape(bsz, t, h, dv)


def _complex_affine_combine(e1, e2):
    a1r, a1i, b1r, b1i = e1
    a2r, a2i, b2r, b2i = e2
    return (a2r * a1r - a2i * a1i,
            a2r * a1i + a2i * a1r,
            a2r * b1r - a2i * b1i + b2r,
            a2r * b1i + a2i * b1r + b2i)


def s5_mixer(u, lam_re, lam_im, log_dt, b_re, b_im, c_re, c_im, d_skip, w_glu):
    bsz, t, _ = u.shape
    uf = u.astype(jnp.float32)
    ug = uf.reshape(bsz, t, S5_GROUPS, S5_GROUP)
    dt = jnp.exp(log_dt.astype(jnp.float32))[:, None]
    lr, li = lam_re.astype(jnp.float32), lam_im.astype(jnp.float32)
    mag = jnp.exp(lr * dt)
    abar_re, abar_im = mag * jnp.cos(li * dt), mag * jnp.sin(li * dt)
    nr, ni = abar_re - 1.0, abar_im
    den = lr * lr + li * li
    fr, fi = (nr * lr + ni * li) / den, (ni * lr - nr * li) / den
    br, bi = b_re.astype(jnp.float32), b_im.astype(jnp.float32)
    bb_re = fr[..., None] * br - fi[..., None] * bi
    bb_im = fr[..., None] * bi + fi[..., None] * br
    bu_re = jnp.einsum('btgi,gpi->btgp', ug, bb_re)
    bu_im = jnp.einsum('btgi,gpi->btgp', ug, bb_im)
    a_re = jnp.broadcast_to(abar_re, (1, t, S5_GROUPS, S5_STATE))
    a_im = jnp.broadcast_to(abar_im, (1, t, S5_GROUPS, S5_STATE))
    _, _, s_re, s_im = lax.associative_scan(_complex_affine_combine, (a_re, a_im, bu_re, bu_im), axis=1)
    y = (jnp.einsum('btgp,gip->btgi', s_re, c_re.astype(jnp.float32))
         - jnp.einsum('btgp,gip->btgi', s_im, c_im.astype(jnp.float32)))
    y = y.reshape(bsz, t, S5_WIDTH) + d_skip.astype(jnp.float32) * uf
    y = jax.nn.gelu(y)
    y = y * jax.nn.sigmoid(y @ w_glu.astype(jnp.float32))
    return y.astype(u.dtype)


def even_mixer(x, w_in, w_out, lam_re, lam_im, log_dt, b_re, b_im, c_re, c_im, d_skip, w_glu,
               w_alpha, b_alpha, norm_g):
    bsz, t, _ = x.shape
    h = x @ w_in
    u, q, k, v, g, a_lr = jnp.split(h, EVEN_SPLITS, axis=-1)
    y_s5 = s5_mixer(u, lam_re, lam_im, log_dt, b_re, b_im, c_re, c_im, d_skip, w_glu)
    q = q.reshape(bsz, t, GLA_HEADS, GLA_DK) * (GLA_DK ** -0.5)
    k = k.reshape(bsz, t, GLA_HEADS, GLA_DK)
    v = v.reshape(bsz, t, GLA_HEADS, GLA_DV)
    log_a = jax.nn.log_sigmoid((a_lr @ w_alpha + b_alpha).astype(jnp.float32)) / GLA_TAU
    o = chunked_gated_linear_recurrence(q, k, v, log_a.reshape(bsz, t, GLA_HEADS, GLA_DK))
    y_gla = gated_head_rmsnorm(o, norm_g, g).astype(x.dtype)
    y = jnp.concatenate([y_s5, y_gla], axis=-1)
    return y @ w_out


def odd_mixer(x, w_in, w_out, lower_bound, norm_g):
    bsz, t, _ = x.shape
    h = x @ w_in
    q, f_logit, i, g = jnp.split(h, ODD_SPLITS, axis=-1)
    lb = lower_bound.astype(jnp.float32)
    f = lb + (1.0 - lb) * jax.nn.sigmoid(f_logit.astype(jnp.float32))
    k = 1.0 - f
    i = jax.nn.silu(i.astype(jnp.float32))
    o = chunked_gated_linear_recurrence(
        q.reshape(bsz, t, HG_HEADS, HG_DK),
        k.reshape(bsz, t, HG_HEADS, HG_DK),
        i.reshape(bsz, t, HG_HEADS, HG_DV),
        jnp.log(f).reshape(bsz, t, HG_HEADS, HG_DK))
    y = gated_head_rmsnorm(o, norm_g, g).astype(x.dtype)
    return y @ w_out


def swiglu_ffn(x, w_gate, w_up, w_down):
    return (jax.nn.silu(x @ w_gate) * (x @ w_up)) @ w_down


def setup_inputs(seed: int = 0) -> dict:
    key = jax.random.key(seed)
    ks = jax.random.split(key, 26)
    f32 = jnp.float32
    nrm = lambda k, shape, std: std * jax.random.normal(k, shape, f32)
    n_idx = jnp.arange(S5_STATE, dtype=f32)
    return {
        "x": nrm(ks[0], (BATCH, SEQ, D_MODEL), 1.0),
        "ev_w_in": nrm(ks[1], (N_EVEN, D_MODEL, EVEN_IN), D_MODEL ** -0.5),
        "ev_w_out": nrm(ks[2], (N_EVEN, MIX_WIDTH, D_MODEL), DEEPNORM_BETA * MIX_WIDTH ** -0.5),
        "s5_lam_re": -0.5 + nrm(ks[3], (N_EVEN, S5_GROUPS, S5_STATE), 0.01),
        "s5_lam_im": math.pi * n_idx + nrm(ks[4], (N_EVEN, S5_GROUPS, S5_STATE), 0.01),
        "s5_log_dt": jax.random.uniform(ks[5], (N_EVEN, S5_GROUPS), f32,
                                        math.log(S5_DT_MIN), math.log(S5_DT_MAX)),
        "s5_b_re": nrm(ks[6], (N_EVEN, S5_GROUPS, S5_STATE, S5_GROUP), (2 * S5_GROUP) ** -0.5),
        "s5_b_im": nrm(ks[7], (N_EVEN, S5_GROUPS, S5_STATE, S5_GROUP), (2 * S5_GROUP) ** -0.5),
        "s5_c_re": nrm(ks[8], (N_EVEN, S5_GROUPS, S5_GROUP, S5_STATE), S5_STATE ** -0.5),
        "s5_c_im": nrm(ks[9], (N_EVEN, S5_GROUPS, S5_GROUP, S5_STATE), S5_STATE ** -0.5),
        "s5_d": nrm(ks[10], (N_EVEN, S5_WIDTH), 1.0),
        "s5_w_glu": nrm(ks[11], (N_EVEN, S5_WIDTH, S5_WIDTH), S5_WIDTH ** -0.5),
        "gla_w_alpha": nrm(ks[12], (N_EVEN, GLA_GATE_RANK, GLA_KEY_WIDTH), GLA_GATE_RANK ** -0.5),
        "gla_b_alpha": nrm(ks[13], (N_EVEN, GLA_KEY_WIDTH), 0.1),
        "gla_norm_g": 1.0 + nrm(ks[14], (N_EVEN, GLA_DV), 0.01),
        "od_w_in": nrm(ks[15], (N_ODD, D_MODEL, ODD_IN), D_MODEL ** -0.5),
        "od_w_out": nrm(ks[16], (N_ODD, MIX_WIDTH, D_MODEL), DEEPNORM_BETA * MIX_WIDTH ** -0.5),
        "hg_lb_table": nrm(ks[17], (DEPTH, HG_KEY_WIDTH), 0.1),
        "hg_norm_g": 1.0 + nrm(ks[18], (N_ODD, HG_DV), 0.01),
        "ln_mix_g": 1.0 + nrm(ks[19], (DEPTH, D_MODEL), 0.01),
        "ln_mix_b": nrm(ks[20], (DEPTH, D_MODEL), 0.01),
        "ln_ffn_g": 1.0 + nrm(ks[21], (DEPTH, D_MODEL), 0.01),
        "ln_ffn_b": nrm(ks[22], (DEPTH, D_MODEL), 0.01),
        "ffn_w_gate": nrm(ks[23], (DEPTH, D_MODEL, FFN_HIDDEN), D_MODEL ** -0.5),
        "ffn_w_up": nrm(ks[24], (DEPTH, D_MODEL, FFN_HIDDEN), D_MODEL ** -0.5),
        "ffn_w_down": nrm(ks[25], (DEPTH, FFN_HIDDEN, D_MODEL), DEEPNORM_BETA * FFN_HIDDEN ** -0.5),
    }


def reference(x, ev_w_in, ev_w_out, s5_lam_re, s5_lam_im, s5_log_dt, s5_b_re, s5_b_im,
              s5_c_re, s5_c_im, s5_d, s5_w_glu, gla_w_alpha, gla_b_alpha, gla_norm_g,
              od_w_in, od_w_out, hg_lb_table, hg_norm_g, ln_mix_g, ln_mix_b, ln_ffn_g, ln_ffn_b,
              ffn_w_gate, ffn_w_up, ffn_w_down):
    lb_soft = jax.nn.softmax(hg_lb_table.astype(jnp.float32), axis=0)
    lower_bounds = jnp.cumsum(lb_soft, axis=0) - lb_soft[0]
    for layer in range(DEPTH):
        if layer % 2 == 0:
            e = layer // 2
            sub = even_mixer(x, ev_w_in[e], ev_w_out[e], s5_lam_re[e], s5_lam_im[e], s5_log_dt[e],
                             s5_b_re[e], s5_b_im[e], s5_c_re[e], s5_c_im[e], s5_d[e], s5_w_glu[e],
                             gla_w_alpha[e], gla_b_alpha[e], gla_norm_g[e])
        else:
            o = layer // 2
            sub = odd_mixer(x, od_w_in[o], od_w_out[o], lower_bounds[layer], hg_norm_g[o])
        x = layer_norm(DEEPNORM_ALPHA * x + sub, ln_mix_g[layer], ln_mix_b[layer])
        x = layer_norm(DEEPNORM_ALPHA * x + swiglu_ffn(x, ffn_w_gate[layer], ffn_w_up[layer], ffn_w_down[layer]),
                       ln_ffn_g[layer], ln_ffn_b[layer])
    return x
```

```python
import functools
import math

import jax
import jax.numpy as jnp
from jax import lax
from jax.experimental import pallas as pl
from jax.experimental.pallas import tpu as pltpu

F32 = jnp.float32
BF16 = jnp.bfloat16

DEPTH = 2
S5_GROUP = 16
S5_STATE = 64
GLA_HEADS = 8
GLA_DK = 128
GLA_DV = 256
GLA_GATE_RANK = 16
GLA_TAU = 16.0
HG_DK = 128
HG_DV = 128
CHUNK = 64
DEEPNORM_ALPHA = (2.0 * DEPTH) ** 0.25
NORM_EPS = 1e-5

V7X_LANES = 128
V7X_SUBLANES = 8
V7X_VMEM_LIMIT_BYTES = 56 * 1024 * 1024

LN_COL_BLOCK = 1024
LN_ROW_BLOCK = 64

S5_CHUNK = 16
S5_ROW = S5_CHUNK * S5_GROUP


def _params(*sem):
    return pltpu.CompilerParams(dimension_semantics=sem, vmem_limit_bytes=V7X_VMEM_LIMIT_BYTES)


def _mm_kernel(a_ref, w_ref, o_ref):
    o_ref[...] = jnp.dot(a_ref[...], w_ref[...], preferred_element_type=F32).astype(o_ref.dtype)


def _matmul(a, w, *, tm, tn, out_dtype, name):
    m, k = a.shape
    n = w.shape[1]
    return pl.pallas_call(
        _mm_kernel,
        grid=(m // tm, n // tn),
        in_specs=[pl.BlockSpec((tm, k), lambda i, j: (i, 0)),
                  pl.BlockSpec((k, tn), lambda i, j: (0, j))],
        out_specs=pl.BlockSpec((tm, tn), lambda i, j: (i, j)),
        out_shape=jax.ShapeDtypeStruct((m, n), out_dtype),
        compiler_params=_params("parallel", "arbitrary"),
        name=name,
    )(a, w)


def _glu_kernel(a_ref, w_ref, y_ref, o_ref):
    z = jnp.dot(a_ref[...], w_ref[...], preferred_element_type=F32)
    o_ref[...] = (y_ref[...] * jax.nn.sigmoid(z)).astype(o_ref.dtype)


def _glu(y_bf, y_f32, w, *, tm, tn):
    m, k = y_bf.shape
    n = w.shape[1]
    return pl.pallas_call(
        _glu_kernel,
        grid=(m // tm, n // tn),
        in_specs=[pl.BlockSpec((tm, k), lambda i, j: (i, 0)),
                  pl.BlockSpec((k, tn), lambda i, j: (0, j)),
                  pl.BlockSpec((tm, tn), lambda i, j: (i, j))],
        out_specs=pl.BlockSpec((tm, tn), lambda i, j: (i, j)),
        out_shape=jax.ShapeDtypeStruct((m, n), BF16),
        compiler_params=_params("parallel", "arbitrary"),
        name="s5_glu",
    )(y_bf, w, y_f32)


def _ffn_up_kernel(x_ref, wg_ref, wu_ref, o_ref):
    x = x_ref[...]
    g = jnp.dot(x, wg_ref[...], preferred_element_type=F32)
    u = jnp.dot(x, wu_ref[...], preferred_element_type=F32)
    o_ref[...] = (g * jax.nn.sigmoid(g) * u).astype(o_ref.dtype)


def _ffn_up(x_bf, wg, wu, *, tm, tn):
    m, k = x_bf.shape
    n = wg.shape[1]
    return pl.pallas_call(
        _ffn_up_kernel,
        grid=(m // tm, n // tn),
        in_specs=[pl.BlockSpec((tm, k), lambda i, j: (i, 0)),
                  pl.BlockSpec((k, tn), lambda i, j: (0, j)),
                  pl.BlockSpec((k, tn), lambda i, j: (0, j))],
        out_specs=pl.BlockSpec((tm, tn), lambda i, j: (i, j)),
        out_shape=jax.ShapeDtypeStruct((m, n), BF16),
        compiler_params=_params("parallel", "arbitrary"),
        name="ffn_up",
    )(x_bf, wg, wu)


def _mm_ln_kernel(a_ref, w_ref, r_ref, g_ref, b_ref, of_ref, *maybe_ob_ref, nk):
    k = pl.program_id(1)

    @pl.when(k == 0)
    def _():
        of_ref[...] = DEEPNORM_ALPHA * r_ref[...]

    a = a_ref[...]
    n = of_ref.shape[1]
    for c0 in range(0, n, LN_COL_BLOCK):
        cols = slice(c0, c0 + LN_COL_BLOCK)
        of_ref[:, cols] += jnp.dot(a, w_ref[:, cols], preferred_element_type=F32)

    @pl.when(k == nk - 1)
    def _():
        gain, bias = g_ref[...], b_ref[...]

        def norm_rows(r, carry):
            rows = pl.ds(pl.multiple_of(r * LN_ROW_BLOCK, LN_ROW_BLOCK), LN_ROW_BLOCK)
            z = of_ref[rows, :]
            mu = jnp.mean(z, axis=-1, keepdims=True)
            zc = z - mu
            var = jnp.mean(zc * zc, axis=-1, keepdims=True)
            y = zc * lax.rsqrt(var + NORM_EPS) * gain + bias
            of_ref[rows, :] = y
            for ob_ref in maybe_ob_ref:
                ob_ref[rows, :] = y.astype(ob_ref.dtype)
            return carry

        lax.fori_loop(0, of_ref.shape[0] // LN_ROW_BLOCK, norm_rows, 0)


def _matmul_residual_ln(a, w, resid, gain, bias, *, tm, tk, emit_bf16, name):
    m, kdim = a.shape
    n = w.shape[1]
    nk = kdim // tk
    row_spec = pl.BlockSpec((tm, n), lambda i, k: (i, 0))
    vec_spec = pl.BlockSpec((1, n), lambda i, k: (0, 0))
    out_shape = [jax.ShapeDtypeStruct((m, n), F32)]
    out_specs = [row_spec]
    if emit_bf16:
        out_shape.append(jax.ShapeDtypeStruct((m, n), BF16))
        out_specs.append(row_spec)
    return pl.pallas_call(
        functools.partial(_mm_ln_kernel, nk=nk),
        grid=(m // tm, nk),
        in_specs=[pl.BlockSpec((tm, tk), lambda i, k: (i, k)),
                  pl.BlockSpec((tk, n), lambda i, k: (k, 0)),
                  row_spec, vec_spec, vec_spec],
        out_specs=out_specs,
        out_shape=out_shape,
        compiler_params=_params("parallel", "arbitrary"),
        name=name,
    )(a, w, resid, gain.reshape(1, n), bias.reshape(1, n))


def _chunk_step(q, k, v, la, st_ref, hd, tril_bf, causal):
    c = q.shape[0]
    la_hi = la.astype(BF16)
    la_lo = (la - la_hi.astype(F32)).astype(BF16)
    b = (jnp.dot(tril_bf, la_hi, preferred_element_type=F32)
         + jnp.dot(tril_bf, la_lo, preferred_element_type=F32))
    b_end = b[c - 1:c, :]
    q_dec = (q * jnp.exp(b)).astype(BF16)
    k_inv = (k * jnp.exp(-b)).astype(BF16)
    k_end = (k * jnp.exp(b_end - b)).astype(BF16)
    v_bf = v.astype(BF16)
    nt = (((1,), (1,)), ((), ()))
    tn = (((0,), (0,)), ((), ()))
    scores = lax.dot_general(q_dec, k_inv, nt, preferred_element_type=F32)
    scores = jnp.where(causal, scores, 0.0).astype(BF16)
    st = st_ref[hd]
    o = (jnp.dot(scores, v_bf, preferred_element_type=F32)
         + lax.dot_general(q_dec, st.astype(BF16), nt, preferred_element_type=F32))
    st_ref[hd] = jnp.exp(b_end) * st + lax.dot_general(v_bf, k_end, tn, preferred_element_type=F32)
    return o


def _gated_rmsnorm(o, gain, gate):
    ms = jnp.mean(o * o, axis=-1, keepdims=True)
    return o * lax.rsqrt(ms + NORM_EPS) * gain * (gate * jax.nn.sigmoid(gate))


def _chunk_masks():
    row = lax.broadcasted_iota(jnp.int32, (CHUNK, CHUNK), 0)
    col = lax.broadcasted_iota(jnp.int32, (CHUNK, CHUNK), 1)
    causal = row >= col
    return jnp.where(causal, 1.0, 0.0).astype(BF16), causal


def _log_sigmoid(z):
    return jnp.minimum(z, 0.0) - jnp.log1p(jnp.exp(-jnp.abs(z)))


def _gla_kernel(q_ref, k_ref, v_ref, g_ref, alr_ref, wa_ref, ba_ref, gain_ref, o_ref, st_ref,
                *, hb, nchunks):
    @pl.when(pl.program_id(2) == 0)
    def _():
        st_ref[...] = jnp.zeros_like(st_ref)

    tril_bf, causal = _chunk_masks()
    gain = gain_ref[...]
    ba = ba_ref[...]
    wa = wa_ref[...]

    def body(c, carry):
        rows = pl.ds(pl.multiple_of(c * CHUNK, CHUNK), CHUNK)
        z = jnp.dot(alr_ref[rows, :].astype(BF16), wa, preferred_element_type=F32) + ba
        la_all = _log_sigmoid(z) / GLA_TAU
        for hd in range(hb):
            kc = slice(hd * GLA_DK, (hd + 1) * GLA_DK)
            vc = slice(hd * GLA_DV, (hd + 1) * GLA_DV)
            q = q_ref[rows, kc] * (GLA_DK ** -0.5)
            o = _chunk_step(q, k_ref[rows, kc], v_ref[rows, vc], la_all[:, kc], st_ref, hd,
                            tril_bf, causal)
            o_ref[rows, vc] = _gated_rmsnorm(o, gain, g_ref[rows, vc]).astype(o_ref.dtype)
        return carry

    lax.fori_loop(0, nchunks, body, 0)


def _gla(h, alr, w_alpha, b_alpha, norm_g, *, bsz, seq, col_q, col_k, col_v, col_g, tt, hb):
    n = bsz * seq
    nt = seq // tt
    kw, vw = hb * GLA_DK, hb * GLA_DV
    hg = GLA_HEADS // hb

    def rows(b, g, t):
        return b * nt + t

    return pl.pallas_call(
        functools.partial(_gla_kernel, hb=hb, nchunks=tt // CHUNK),
        grid=(bsz, hg, nt),
        in_specs=[pl.BlockSpec((tt, kw), lambda b, g, t: (rows(b, g, t), col_q // kw + g)),
                  pl.BlockSpec((tt, kw), lambda b, g, t: (rows(b, g, t), col_k // kw + g)),
                  pl.BlockSpec((tt, vw), lambda b, g, t: (rows(b, g, t), col_v // vw + g)),
                  pl.BlockSpec((tt, vw), lambda b, g, t: (rows(b, g, t), col_g // vw + g)),
                  pl.BlockSpec((tt, V7X_LANES), lambda b, g, t: (rows(b, g, t), 0)),
                  pl.BlockSpec((V7X_LANES, kw), lambda b, g, t: (0, g)),
                  pl.BlockSpec((1, kw), lambda b, g, t: (0, g)),
                  pl.BlockSpec((1, GLA_DV), lambda b, g, t: (0, 0))],
        out_specs=pl.BlockSpec((tt, vw), lambda b, g, t: (rows(b, g, t), g)),
        out_shape=jax.ShapeDtypeStruct((n, GLA_HEADS * GLA_DV), BF16),
        scratch_shapes=[pltpu.VMEM((hb, GLA_DV, GLA_DK), F32)],
        compiler_params=_params("parallel", "parallel", "arbitrary"),
        name="gla",
    )(h, h, h, h, alr, w_alpha, b_alpha, norm_g)


def _hgrn_kernel(q_ref, f_ref, i_ref, g_ref, tab_ref, gain_ref, o_ref, st_ref, *, layer, hb, nchunks):
    @pl.when(pl.program_id(2) == 0)
    def _():
        st_ref[...] = jnp.zeros_like(st_ref)

    tab = tab_ref[...]
    e = jnp.exp(tab - jnp.max(tab, axis=0, keepdims=True))
    soft = e / jnp.sum(e, axis=0, keepdims=True)
    lb_all = jnp.sum(soft[0:layer + 1], axis=0, keepdims=True) - soft[0:1]
    tril_bf, causal = _chunk_masks()
    gain = gain_ref[...]

    def body(c, carry):
        rows = pl.ds(pl.multiple_of(c * CHUNK, CHUNK), CHUNK)
        for hd in range(hb):
            kc = slice(hd * HG_DK, (hd + 1) * HG_DK)
            vc = slice(hd * HG_DV, (hd + 1) * HG_DV)
            lb = lb_all[:, kc]
            f = lb + (1.0 - lb) * jax.nn.sigmoid(f_ref[rows, kc])
            iv = i_ref[rows, vc]
            o = _chunk_step(q_ref[rows, kc], 1.0 - f, iv * jax.nn.sigmoid(iv), jnp.log(f),
                            st_ref, hd, tril_bf, causal)
            o_ref[rows, vc] = _gated_rmsnorm(o, gain, g_ref[rows, vc]).astype(o_ref.dtype)
        return carry

    lax.fori_loop(0, nchunks, body, 0)


def _hgrn(h, lb_table, norm_g, *, layer, bsz, seq, heads, tt, hb):
    n = bsz * seq
    nt = seq // tt
    w = hb * HG_DK
    hg = heads // hb

    def spec(seg):
        return pl.BlockSpec((tt, w), lambda b, g, t: (b * nt + t, seg * hg + g))

    return pl.pallas_call(
        functools.partial(_hgrn_kernel, layer=layer, hb=hb, nchunks=tt // CHUNK),
        grid=(bsz, hg, nt),
        in_specs=[spec(0), spec(1), spec(2), spec(3),
                  pl.BlockSpec((DEPTH, w), lambda b, g, t: (0, g)),
                  pl.BlockSpec((1, HG_DV), lambda b, g, t: (0, 0))],
        out_specs=pl.BlockSpec((tt, w), lambda b, g, t: (b * nt + t, g)),
        out_shape=jax.ShapeDtypeStruct((n, heads * HG_DV), BF16),
        scratch_shapes=[pltpu.VMEM((hb, HG_DV, HG_DK), F32)],
        compiler_params=_params("parallel", "parallel", "arbitrary"),
        name="hgrn2",
    )(h, h, h, h, lb_table, norm_g)


def _s5_prep_kernel(lr_row_ref, li_row_ref, lr_col_ref, li_col_ref, ldt_ref,
                    btr_ref, bti_ref, ctr_ref, cti_ref, w1_ref, f_ref, coef_ref):
    L, J, P = S5_CHUNK, S5_GROUP, S5_STATE
    hi = lax.Precision.HIGHEST
    dt = jnp.exp(ldt_ref[0])
    lr, li = lr_row_ref[0], li_row_ref[0]
    npow = ((L - 1) - lax.broadcasted_iota(jnp.int32, (L, 2 * P), 0)).astype(F32)
    mag = jnp.exp(npow * (lr * dt))
    th = npow * (li * dt)
    ar, ai = mag * jnp.cos(th), mag * jnp.sin(th)
    half = lax.broadcasted_iota(jnp.int32, (L, 2 * P), 1) < P
    a1r, a1i = ar[L - 2:L - 1], ai[L - 2:L - 1]
    nr, ni = a1r - 1.0, a1i
    den = lr * lr + li * li
    fr, fi = (nr * lr + ni * li) / den, (ni * lr - nr * li) / den
    btr, bti = btr_ref[0], bti_ref[0]
    bbr, bbi = fr * btr - fi * bti, fr * bti + fi * btr
    pa, pb = jnp.where(half, ar, ai), jnp.where(half, -ai, ar)
    pasw, pbsw = jnp.where(half, ai, ar), jnp.where(half, ar, -ai)
    for s in range(L):
        rows = slice(s * J, (s + 1) * J)
        w1_ref[0, rows, L * J:L * J + 2 * P] = (bbr * pa[s:s + 1] + bbi * pb[s:s + 1]).astype(BF16)
        w1_ref[0, rows, L * J + 2 * P:] = (bbr * pasw[s:s + 1] + bbi * pbsw[s:s + 1]).astype(BF16)
    alr_, ali_ = ar[0:1] * a1r - ai[0:1] * a1i, ar[0:1] * a1i + ai[0:1] * a1r
    half1 = half[0:1]
    coef_ref[0, 0:1, :] = alr_
    coef_ref[0, 1:2, :] = jnp.where(half1, -ali_, ali_)
    coef_ref[0, 2:3, :] = jnp.where(half1, ali_, -ali_)
    coef_ref[0, 3:, :] = jnp.zeros((V7X_SUBLANES - 3, 2 * P), F32)

    lrc, lic = lr_col_ref[0], li_col_ref[0]
    lane = lax.broadcasted_iota(jnp.int32, (P, L * J), 1)
    lag = jnp.right_shift(lane, int(math.log2(J))).astype(F32)
    magq = jnp.exp(lag * (lrc * dt))
    thq = lag * (lic * dt)
    aqr, aqi = magq * jnp.cos(thq), magq * jnp.sin(thq)
    cr, ci = ctr_ref[0], cti_ref[0]
    qr, qi = cr * aqr - ci * aqi, cr * aqi + ci * aqr
    r = (jnp.dot(bbr[:, :P], qr, precision=hi, preferred_element_type=F32)
         - jnp.dot(bbi[:, :P], qi, precision=hi, preferred_element_type=F32))
    lane_r = lax.broadcasted_iota(jnp.int32, (J, L * J), 1)
    for s in range(L):
        blk = r if s == 0 else jnp.where(lane_r >= s * J, pltpu.roll(r, s * J, 1), 0.0)
        w1_ref[0, s * J:(s + 1) * J, 0:L * J] = blk.astype(BF16)
    m1 = jnp.exp(lrc * dt)
    c1r, c1i = m1 * jnp.cos(lic * dt), m1 * jnp.sin(lic * dt)
    f_ref[0, 0:P, :] = (qr * c1r - qi * c1i).astype(BF16)
    f_ref[0, P:, :] = (-(qr * c1i + qi * c1r)).astype(BF16)


def _s5_prep(lam_re, lam_im, log_dt, b_re, b_im, c_re, c_im):
    g = lam_re.shape[0]
    P, J, L = S5_STATE, S5_GROUP, S5_CHUNK
    dbl = lambda a: jnp.concatenate([a, a], axis=-1)
    lr_row = dbl(lam_re).reshape(g, 1, 2 * P)
    li_row = dbl(lam_im).reshape(g, 1, 2 * P)
    lr_col = lam_re.reshape(g, P, 1)
    li_col = lam_im.reshape(g, P, 1)
    ldt = log_dt.reshape(g, 1, 1)
    btr = dbl(jnp.swapaxes(b_re, 1, 2))
    bti = dbl(jnp.swapaxes(b_im, 1, 2))
    ctr = jnp.tile(jnp.swapaxes(c_re, 1, 2), (1, 1, L))
    cti = jnp.tile(jnp.swapaxes(c_im, 1, 2), (1, 1, L))
    blk = lambda *s: pl.BlockSpec((1,) + s, lambda i: (i, 0, 0))
    return pl.pallas_call(
        _s5_prep_kernel,
        grid=(g,),
        in_specs=[blk(1, 2 * P), blk(1, 2 * P), blk(P, 1), blk(P, 1), blk(1, 1),
                  blk(J, 2 * P), blk(J, 2 * P), blk(P, L * J), blk(P, L * J)],
        out_specs=[blk(L * J, L * J + 4 * P), blk(2 * P, L * J), blk(V7X_SUBLANES, 2 * P)],
        out_shape=[jax.ShapeDtypeStruct((g, L * J, L * J + 4 * P), BF16),
                   jax.ShapeDtypeStruct((g, 2 * P, L * J), BF16),
                   jax.ShapeDtypeStruct((g, V7X_SUBLANES, 2 * P), F32)],
        compiler_params=_params("parallel"),
        name="s5_prep",
    )(lr_row, li_row, lr_col, li_col, ldt, btr, bti, ctr, cti)


def _s5_main_kernel(u_ref, w1_ref, f_ref, coef_ref, d_ref, o_ref, ys_ref, sp_ref, *, nchunk, nb):
    P2 = 2 * S5_STATE
    u = u_ref[0]
    ys_ref[...] = jnp.dot(u.astype(BF16), w1_ref[0], preferred_element_type=F32)
    a1, a2, a2sw = coef_ref[0, 0:1, :], coef_ref[0, 1:2, :], coef_ref[0, 2:3, :]

    def body(c, carry):
        s, ssw = carry
        rows = pl.ds(pl.multiple_of(c * nb, nb), nb)
        sp_ref[rows, :] = s
        loc = ys_ref[rows, S5_ROW:S5_ROW + P2]
        locsw = ys_ref[rows, S5_ROW + P2:]
        return a1 * s + a2 * ssw + loc, a1 * ssw + a2sw * s + locsw

    zero = jnp.zeros((nb, P2), F32)
    lax.fori_loop(0, nchunk, body, (zero, zero), unroll=8)
    y = (ys_ref[:, 0:S5_ROW]
         + jnp.dot(sp_ref[...].astype(BF16), f_ref[0], preferred_element_type=F32)
         + d_ref[0] * u)
    o_ref[0] = jax.nn.gelu(y, approximate=True)


def _s5_main(u_g, w1, fmat, coef, d_t, *, nchunk, nb):
    g, rows, width = u_g.shape
    P2 = 2 * S5_STATE
    blk = lambda *s: pl.BlockSpec((1,) + s, lambda i: (i, 0, 0))
    return pl.pallas_call(
        functools.partial(_s5_main_kernel, nchunk=nchunk, nb=nb),
        grid=(g,),
        in_specs=[blk(rows, width), blk(width, width + 2 * P2), blk(P2, width),
                  blk(V7X_SUBLANES, P2), blk(1, width)],
        out_specs=blk(rows, width),
        out_shape=jax.ShapeDtypeStruct((g, rows, width), F32),
        scratch_shapes=[pltpu.VMEM((rows, width + 2 * P2), F32), pltpu.VMEM((rows, P2), F32)],
        compiler_params=_params("parallel"),
        name="s5_main",
    )(u_g, w1, fmat, coef, d_t)


def _pad_cols(w, n):
    return jnp.pad(w, ((0, 0), (0, n - w.shape[1])))


def _ffn_block(x_f32, x_bf, w_gate, w_up, w_down, ln_g, ln_b, *, emit_bf16):
    hidden = w_gate.shape[1]
    hpad = -(-hidden // 1024) * 1024
    wg = _pad_cols(w_gate, hpad).astype(BF16)
    wu = _pad_cols(w_up, hpad).astype(BF16)
    wd = jnp.pad(w_down, ((0, hpad - hidden), (0, 0))).astype(BF16)
    a = _ffn_up(x_bf, wg, wu, tm=1024, tn=512)
    return _matmul_residual_ln(a, wd, x_f32, ln_g, ln_b, tm=512, tk=512, emit_bf16=emit_bf16,
                               name="ffn_down_ln")


def _even_layer(x_f32, x_bf, bsz, seq, w_in, w_out, lam_re, lam_im, log_dt, b_re, b_im, c_re, c_im,
                d_skip, w_glu, w_alpha, b_alpha, norm_g, ln_g, ln_b):
    n = bsz * seq
    s5w = d_skip.shape[0]
    groups = s5w // S5_GROUP
    kw = GLA_HEADS * GLA_DK
    vw = GLA_HEADS * GLA_DV
    main = s5w + 2 * kw + 2 * vw
    h = _matmul(x_bf, w_in[:, :main].astype(BF16), tm=1024, tn=1024, out_dtype=F32, name="even_in")
    alr = _matmul(x_bf, _pad_cols(w_in[:, main:], V7X_LANES).astype(BF16), tm=1024, tn=V7X_LANES,
                  out_dtype=F32, name="even_in_gate")

    w1, fmat, coef = _s5_prep(lam_re, lam_im, log_dt, b_re, b_im, c_re, c_im)
    nchunk = seq // S5_CHUNK
    u_g = (h[:, :s5w].reshape(bsz, nchunk, S5_CHUNK, groups, S5_GROUP)
           .transpose(3, 1, 0, 2, 4).reshape(groups, nchunk * bsz, S5_ROW))
    d_t = jnp.tile(d_skip.reshape(groups, 1, S5_GROUP), (1, 1, S5_CHUNK))
    y_g = _s5_main(u_g, w1, fmat, coef, d_t, nchunk=nchunk, nb=bsz)
    y5 = (y_g.reshape(groups, nchunk, bsz, S5_CHUNK, S5_GROUP)
          .transpose(2, 1, 3, 0, 4).reshape(n, s5w))
    y_s5 = _glu(y5.astype(BF16), y5, w_glu.astype(BF16), tm=1024, tn=512)

    wa = jnp.pad(w_alpha, ((0, V7X_LANES - w_alpha.shape[0]), (0, 0))).astype(BF16)
    y_gla = _gla(h, alr, wa, b_alpha.reshape(1, kw), norm_g.reshape(1, GLA_DV), bsz=bsz, seq=seq,
                 col_q=s5w, col_k=s5w + kw, col_v=s5w + 2 * kw, col_g=s5w + 2 * kw + vw,
                 tt=256, hb=4)

    y = jnp.concatenate([y_s5, y_gla], axis=-1)
    return _matmul_residual_ln(y, w_out.astype(BF16), x_f32, ln_g, ln_b, tm=512, tk=512,
                               emit_bf16=True, name="even_out_ln")


def _odd_layer(x_f32, x_bf, bsz, seq, layer, w_in, w_out, lb_table, norm_g, ln_g, ln_b):
    heads = w_out.shape[0] // HG_DV
    h = _matmul(x_bf, w_in.astype(BF16), tm=1024, tn=1024, out_dtype=F32, name="odd_in")
    y = _hgrn(h, lb_table, norm_g.reshape(1, HG_DV), layer=layer, bsz=bsz, seq=seq, heads=heads,
              tt=256, hb=4)
    return _matmul_residual_ln(y, w_out.astype(BF16), x_f32, ln_g, ln_b, tm=512, tk=512,
                               emit_bf16=True, name="odd_out_ln")


def kernel(x, ev_w_in, ev_w_out, s5_lam_re, s5_lam_im, s5_log_dt, s5_b_re, s5_b_im, s5_c_re, s5_c_im,
           s5_d, s5_w_glu, gla_w_alpha, gla_b_alpha, gla_norm_g, od_w_in, od_w_out, hg_lb_table,
           hg_norm_g, ln_mix_g, ln_mix_b, ln_ffn_g, ln_ffn_b, ffn_w_gate, ffn_w_up, ffn_w_down):
    bsz, seq, d = x.shape
    depth = ln_mix_g.shape[0]
    assert depth == DEPTH
    xf = x.reshape(bsz * seq, d)
    xb = xf.astype(BF16)
    for layer in range(depth):
        if layer % 2 == 0:
            e = layer // 2
            xf, xb = _even_layer(xf, xb, bsz, seq, ev_w_in[e], ev_w_out[e], s5_lam_re[e], s5_lam_im[e],
                                 s5_log_dt[e], s5_b_re[e], s5_b_im[e], s5_c_re[e], s5_c_im[e], s5_d[e],
                                 s5_w_glu[e], gla_w_alpha[e], gla_b_alpha[e], gla_norm_g[e],
                                 ln_mix_g[layer], ln_mix_b[layer])
        else:
            o = layer // 2
            xf, xb = _odd_layer(xf, xb, bsz, seq, layer, od_w_in[o], od_w_out[o], hg_lb_table,
                                hg_norm_g[o], ln_mix_g[layer], ln_mix_b[layer])
        last = layer == depth - 1
        res = _ffn_block(xf, xb, ffn_w_gate[layer], ffn_w_up[layer], ffn_w_down[layer],
                         ln_ffn_g[layer], ln_ffn_b[layer], emit_bf16=not last)
        xf, xb = (res[0], None) if last else res
    return xf.reshape(bsz, seq, d).astype(x.dtype)
```

```python
import functools
import math

import jax
import jax.numpy as jnp
from jax import lax
from jax.experimental import pallas as pl
from jax.experimental.pallas import tpu as pltpu

F32 = jnp.float32
BF16 = jnp.bfloat16

DEPTH = 2
S5_GROUP = 16
S5_STATE = 64
GLA_HEADS = 8
GLA_DK = 128
GLA_DV = 256
GLA_GATE_RANK = 16
GLA_TAU = 16.0
HG_DK = 128
HG_DV = 128
CHUNK = 64
DEEPNORM_ALPHA = (2.0 * DEPTH) ** 0.25
NORM_EPS = 1e-5

V7X_LANES = 128
V7X_SUBLANES = 8
V7X_VMEM_LIMIT_BYTES = 56 * 1024 * 1024

FFN_DOWN_TK = 1024
LN_COL_BLOCK = 1024
LN_ROW_BLOCK = 16

S5_CHUNK = 16
S5_ROW = S5_CHUNK * S5_GROUP


def _params(*sem):
    return pltpu.CompilerParams(dimension_semantics=sem, vmem_limit_bytes=V7X_VMEM_LIMIT_BYTES)


def _mm_kernel(a_ref, w_ref, o_ref, wb_ref):
    @pl.when(pl.program_id(1) == 0)
    def _():
        wb_ref[...] = w_ref[...].astype(BF16)

    o_ref[...] = jnp.dot(a_ref[...], wb_ref[...], preferred_element_type=F32).astype(o_ref.dtype)


def _matmul(a, w, *, tm, tn, out_dtype, name):
    m, k = a.shape
    n = w.shape[1]
    return pl.pallas_call(
        _mm_kernel,
        grid=(n // tn, m // tm),
        in_specs=[pl.BlockSpec((tm, k), lambda j, i: (i, 0)),
                  pl.BlockSpec((k, tn), lambda j, i: (0, j))],
        out_specs=pl.BlockSpec((tm, tn), lambda j, i: (i, j)),
        out_shape=jax.ShapeDtypeStruct((m, n), out_dtype),
        scratch_shapes=[pltpu.VMEM((k, tn), BF16)],
        compiler_params=_params("arbitrary", "arbitrary"),
        name=name,
    )(a, w)


def _glu_kernel(a_ref, w_ref, y_ref, o_ref):
    z = jnp.dot(a_ref[...], w_ref[...], preferred_element_type=F32)
    o_ref[...] = (y_ref[...] * jax.nn.sigmoid(z)).astype(o_ref.dtype)


def _glu(y_bf, y_f32, w, *, tm, tn):
    m, k = y_bf.shape
    n = w.shape[1]
    return pl.pallas_call(
        _glu_kernel,
        grid=(m // tm, n // tn),
        in_specs=[pl.BlockSpec((tm, k), lambda i, j: (i, 0)),
                  pl.BlockSpec((k, tn), lambda i, j: (0, j)),
                  pl.BlockSpec((tm, tn), lambda i, j: (i, j))],
        out_specs=pl.BlockSpec((tm, tn), lambda i, j: (i, j)),
        out_shape=jax.ShapeDtypeStruct((m, n), BF16),
        compiler_params=_params("parallel", "arbitrary"),
        name="s5_glu",
    )(y_bf, w, y_f32)


def _ffn_up_kernel(x_ref, wg_ref, wu_ref, o_ref, wgb_ref, wub_ref, *, n_real):
    j = pl.program_id(0)

    @pl.when(j < n_real)
    def _():
        @pl.when(pl.program_id(1) == 0)
        def _():
            wgb_ref[...] = wg_ref[...].astype(BF16)
            wub_ref[...] = wu_ref[...].astype(BF16)

        x = x_ref[...]
        g = jnp.dot(x, wgb_ref[...], preferred_element_type=F32)
        u = jnp.dot(x, wub_ref[...], preferred_element_type=F32)
        o_ref[...] = (g * jax.nn.sigmoid(g) * u).astype(o_ref.dtype)

    @pl.when(j >= n_real)
    def _():
        o_ref[...] = jnp.zeros_like(o_ref)


def _ffn_up(x_bf, wg, wu, *, tm, tn, n_out):
    m, k = x_bf.shape
    n_real = wg.shape[1] // tn
    w_spec = pl.BlockSpec((k, tn), lambda j, i: (0, jnp.minimum(j, n_real - 1)))
    return pl.pallas_call(
        functools.partial(_ffn_up_kernel, n_real=n_real),
        grid=(n_out // tn, m // tm),
        in_specs=[pl.BlockSpec((tm, k), lambda j, i: (i, 0)), w_spec, w_spec],
        out_specs=pl.BlockSpec((tm, tn), lambda j, i: (i, j)),
        out_shape=jax.ShapeDtypeStruct((m, n_out), BF16),
        scratch_shapes=[pltpu.VMEM((k, tn), BF16), pltpu.VMEM((k, tn), BF16)],
        compiler_params=_params("arbitrary", "arbitrary"),
        name="ffn_up",
    )(x_bf, wg, wu)


def _mm_ln_kernel(*refs, nk_parts):
    n_a = len(nk_parts)
    a_refs = refs[:n_a]
    w_ref, r_ref, g_ref, b_ref, of_ref = refs[n_a:n_a + 5]
    n_out = len(refs) - n_a - 6
    maybe_ob_ref = refs[n_a + 5:n_a + 5 + n_out - 1]
    mu_ref, rs_ref = refs[-2:]
    k = pl.program_id(1)

    @pl.when(k == 0)
    def _():
        of_ref[...] = DEEPNORM_ALPHA * r_ref[...]

    def accumulate(a_ref):
        a = a_ref[...]
        for c0 in range(0, of_ref.shape[1], LN_COL_BLOCK):
            cols = slice(c0, c0 + LN_COL_BLOCK)
            of_ref[:, cols] += jnp.dot(a, w_ref[:, cols], preferred_element_type=F32)

    if n_a == 1:
        accumulate(a_refs[0])
    else:
        k0 = 0
        for a_ref, nkp in zip(a_refs, nk_parts):
            pl.when(jnp.logical_and(k >= k0, k < k0 + nkp))(functools.partial(accumulate, a_ref))
            k0 += nkp

    @pl.when(k == sum(nk_parts) - 1)
    def _():
        inv_n = 1.0 / of_ref.shape[1]
        nblk = of_ref.shape[0] // LN_ROW_BLOCK
        lanes = mu_ref.shape[1]
        nrep = of_ref.shape[1] // lanes

        def block(r):
            return pl.ds(pl.multiple_of(r * LN_ROW_BLOCK, LN_ROW_BLOCK), LN_ROW_BLOCK)

        def mean_rows(r, carry):
            rows = block(r)
            mu = jnp.sum(of_ref[rows, :], axis=-1, keepdims=True) * inv_n
            mu_ref[rows, :] = jnp.broadcast_to(mu, (LN_ROW_BLOCK, lanes))
            return carry

        def rstd_rows(r, carry):
            rows = block(r)
            zc = of_ref[rows, :] - pltpu.repeat(mu_ref[rows, :], nrep, axis=1)
            var = jnp.sum(zc * zc, axis=-1, keepdims=True) * inv_n
            rs_ref[rows, :] = jnp.broadcast_to(lax.rsqrt(var + NORM_EPS), (LN_ROW_BLOCK, lanes))
            return carry

        def norm_rows(r, carry):
            rows = block(r)
            mu = pltpu.repeat(mu_ref[rows, :], LN_COL_BLOCK // lanes, axis=1)
            rs = pltpu.repeat(rs_ref[rows, :], LN_COL_BLOCK // lanes, axis=1)
            for c0 in range(0, of_ref.shape[1], LN_COL_BLOCK):
                cols = slice(c0, c0 + LN_COL_BLOCK)
                y = (of_ref[rows, cols] - mu) * rs * g_ref[:, cols] + b_ref[:, cols]
                of_ref[rows, cols] = y
                for ob_ref in maybe_ob_ref:
                    ob_ref[rows, cols] = y.astype(ob_ref.dtype)
            return carry

        lax.fori_loop(0, nblk, mean_rows, 0, unroll=4)
        lax.fori_loop(0, nblk, rstd_rows, 0, unroll=4)
        lax.fori_loop(0, nblk, norm_rows, 0, unroll=2)


def _matmul_residual_ln(a_parts, w, resid, gain, bias, *, tm, tk, emit_bf16, name):
    m = resid.shape[0]
    n = w.shape[1]
    nk_parts = tuple(a.shape[1] // tk for a in a_parts)
    row_spec = pl.BlockSpec((tm, n), lambda i, k: (i, 0))
    vec_spec = pl.BlockSpec((1, n), lambda i, k: (0, 0))
    a_specs, k0 = [], 0
    for nkp in nk_parts:
        a_specs.append(pl.BlockSpec((tm, tk), lambda i, k, k0=k0, nkp=nkp: (i, jnp.clip(k - k0, 0, nkp - 1))))
        k0 += nkp
    out_shape = [jax.ShapeDtypeStruct((m, n), F32)]
    out_specs = [row_spec]
    if emit_bf16:
        out_shape.append(jax.ShapeDtypeStruct((m, n), BF16))
        out_specs.append(row_spec)
    return pl.pallas_call(
        functools.partial(_mm_ln_kernel, nk_parts=nk_parts),
        grid=(m // tm, sum(nk_parts)),
        in_specs=a_specs + [pl.BlockSpec((tk, n), lambda i, k: (k, 0)),
                            pl.BlockSpec((tm, n), lambda i, k: (i, 0), pipeline_mode=pl.Buffered(1)),
                            vec_spec, vec_spec],
        out_specs=out_specs,
        out_shape=out_shape,
        scratch_shapes=[pltpu.VMEM((tm, V7X_LANES), F32), pltpu.VMEM((tm, V7X_LANES), F32)],
        compiler_params=_params("parallel", "arbitrary"),
        name=name,
    )(*a_parts, w, resid, gain.reshape(1, n), bias.reshape(1, n))


def _chunk_steps(qs, ks, vs, las, st_ref, tril_bf, causal):
    c = qs[0].shape[0]
    nt = (((1,), (1,)), ((), ()))
    tn = (((0,), (0,)), ((), ()))
    heads = range(len(qs))
    bs = []
    for la in las:
        la_hi = la.astype(BF16)
        la_lo = (la - la_hi.astype(F32)).astype(BF16)
        bs.append(jnp.dot(tril_bf, la_hi, preferred_element_type=F32)
                  + jnp.dot(tril_bf, la_lo, preferred_element_type=F32))
    q_decs, scores, kvs, inters, decays = [], [], [], [], []
    for hd in heads:
        b = bs[hd]
        b_end = b[c - 1:c, :]
        q_dec = (qs[hd] * jnp.exp(b)).astype(BF16)
        k_inv = (ks[hd] * jnp.exp(-b)).astype(BF16)
        k_end = (ks[hd] * jnp.exp(b_end - b)).astype(BF16)
        v_bf = vs[hd].astype(BF16)
        vs[hd] = v_bf
        scores.append(lax.dot_general(q_dec, k_inv, nt, preferred_element_type=F32))
        kvs.append(lax.dot_general(v_bf, k_end, tn, preferred_element_type=F32))
        inters.append(lax.dot_general(q_dec, st_ref[hd].astype(BF16), nt, preferred_element_type=F32))
        decays.append(jnp.exp(b_end))
    outs = []
    for hd in heads:
        sc = jnp.where(causal, scores[hd], 0.0).astype(BF16)
        outs.append(jnp.dot(sc, vs[hd], preferred_element_type=F32) + inters[hd])
        st_ref[hd] = decays[hd] * st_ref[hd] + kvs[hd]
    return outs


def _gated_rmsnorm(o, gain, gate):
    ms = jnp.mean(o * o, axis=-1, keepdims=True)
    return o * lax.rsqrt(ms + NORM_EPS) * gain * (gate * jax.nn.sigmoid(gate))


def _chunk_masks():
    row = lax.broadcasted_iota(jnp.int32, (CHUNK, CHUNK), 0)
    col = lax.broadcasted_iota(jnp.int32, (CHUNK, CHUNK), 1)
    causal = row >= col
    return jnp.where(causal, 1.0, 0.0).astype(BF16), causal


def _log_sigmoid(z):
    return jnp.minimum(z, 0.0) - jnp.log1p(jnp.exp(-jnp.abs(z)))


def _gla_kernel(q_ref, k_ref, v_ref, g_ref, alr_ref, wa_ref, ba_ref, gain_ref, o_ref, st_ref,
                *, hb, nchunks, unroll):
    @pl.when(pl.program_id(2) == 0)
    def _():
        st_ref[...] = jnp.zeros_like(st_ref)

    tril_bf, causal = _chunk_masks()
    gain = gain_ref[...]
    ba = ba_ref[...]
    wa = wa_ref[...]

    def body(c, carry):
        rows = pl.ds(pl.multiple_of(c * CHUNK, CHUNK), CHUNK)
        z = jnp.dot(alr_ref[rows, :].astype(BF16), wa, preferred_element_type=F32) + ba
        la_all = _log_sigmoid(z) / GLA_TAU
        kcs = [slice(hd * GLA_DK, (hd + 1) * GLA_DK) for hd in range(hb)]
        vcs = [slice(hd * GLA_DV, (hd + 1) * GLA_DV) for hd in range(hb)]
        outs = _chunk_steps([q_ref[rows, kc] * (GLA_DK ** -0.5) for kc in kcs],
                            [k_ref[rows, kc] for kc in kcs], [v_ref[rows, vc] for vc in vcs],
                            [la_all[:, kc] for kc in kcs], st_ref, tril_bf, causal)
        for o, vc in zip(outs, vcs):
            o_ref[rows, vc] = _gated_rmsnorm(o, gain, g_ref[rows, vc]).astype(o_ref.dtype)
        return carry

    lax.fori_loop(0, nchunks, body, 0, unroll=unroll)


def _gla(h, alr, w_alpha, b_alpha, norm_g, *, bsz, seq, col_q, col_k, col_v, col_g, tt, hb, unroll):
    n = bsz * seq
    nt = seq // tt
    kw, vw = hb * GLA_DK, hb * GLA_DV
    hg = GLA_HEADS // hb

    def rows(b, g, t):
        return b * nt + t

    return pl.pallas_call(
        functools.partial(_gla_kernel, hb=hb, nchunks=tt // CHUNK, unroll=unroll),
        grid=(bsz, hg, nt),
        in_specs=[pl.BlockSpec((tt, kw), lambda b, g, t: (rows(b, g, t), col_q // kw + g)),
                  pl.BlockSpec((tt, kw), lambda b, g, t: (rows(b, g, t), col_k // kw + g)),
                  pl.BlockSpec((tt, vw), lambda b, g, t: (rows(b, g, t), col_v // vw + g)),
                  pl.BlockSpec((tt, vw), lambda b, g, t: (rows(b, g, t), col_g // vw + g)),
                  pl.BlockSpec((tt, V7X_LANES), lambda b, g, t: (rows(b, g, t), 0)),
                  pl.BlockSpec((V7X_LANES, kw), lambda b, g, t: (0, g)),
                  pl.BlockSpec((1, kw), lambda b, g, t: (0, g)),
                  pl.BlockSpec((1, GLA_DV), lambda b, g, t: (0, 0))],
        out_specs=pl.BlockSpec((tt, vw), lambda b, g, t: (rows(b, g, t), g)),
        out_shape=jax.ShapeDtypeStruct((n, GLA_HEADS * GLA_DV), BF16),
        scratch_shapes=[pltpu.VMEM((hb, GLA_DV, GLA_DK), F32)],
        compiler_params=_params("parallel", "parallel", "arbitrary"),
        name="gla",
    )(h, h, h, h, alr, w_alpha, b_alpha, norm_g)


def _hgrn_kernel(q_ref, f_ref, i_ref, g_ref, tab_ref, gain_ref, o_ref, st_ref, *, layer, hb, nchunks,
                 unroll):
    @pl.when(pl.program_id(2) == 0)
    def _():
        st_ref[...] = jnp.zeros_like(st_ref)

    tab = tab_ref[...]
    e = jnp.exp(tab - jnp.max(tab, axis=0, keepdims=True))
    soft = e / jnp.sum(e, axis=0, keepdims=True)
    lb_all = jnp.sum(soft[0:layer + 1], axis=0, keepdims=True) - soft[0:1]
    tril_bf, causal = _chunk_masks()
    gain = gain_ref[...]

    def body(c, carry):
        rows = pl.ds(pl.multiple_of(c * CHUNK, CHUNK), CHUNK)
        kcs = [slice(hd * HG_DK, (hd + 1) * HG_DK) for hd in range(hb)]
        vcs = [slice(hd * HG_DV, (hd + 1) * HG_DV) for hd in range(hb)]
        fs = [lb_all[:, kc] + (1.0 - lb_all[:, kc]) * jax.nn.sigmoid(f_ref[rows, kc]) for kc in kcs]
        ivs = [i_ref[rows, vc] for vc in vcs]
        outs = _chunk_steps([q_ref[rows, kc] for kc in kcs], [1.0 - f for f in fs],
                            [iv * jax.nn.sigmoid(iv) for iv in ivs], [jnp.log(f) for f in fs],
                            st_ref, tril_bf, causal)
        for o, vc in zip(outs, vcs):
            o_ref[rows, vc] = _gated_rmsnorm(o, gain, g_ref[rows, vc]).astype(o_ref.dtype)
        return carry

    lax.fori_loop(0, nchunks, body, 0, unroll=unroll)


def _hgrn(h, lb_table, norm_g, *, layer, bsz, seq, heads, tt, hb, unroll):
    n = bsz * seq
    nt = seq // tt
    w = hb * HG_DK
    hg = heads // hb

    def spec(seg):
        return pl.BlockSpec((tt, w), lambda b, g, t: (b * nt + t, seg * hg + g))

    return pl.pallas_call(
        functools.partial(_hgrn_kernel, layer=layer, hb=hb, nchunks=tt // CHUNK, unroll=unroll),
        grid=(bsz, hg, nt),
        in_specs=[spec(0), spec(1), spec(2), spec(3),
                  pl.BlockSpec((DEPTH, w), lambda b, g, t: (0, g)),
                  pl.BlockSpec((1, HG_DV), lambda b, g, t: (0, 0))],
        out_specs=pl.BlockSpec((tt, w), lambda b, g, t: (b * nt + t, g)),
        out_shape=jax.ShapeDtypeStruct((n, heads * HG_DV), BF16),
        scratch_shapes=[pltpu.VMEM((hb, HG_DV, HG_DK), F32)],
        compiler_params=_params("parallel", "parallel", "arbitrary"),
        name="hgrn2",
    )(h, h, h, h, lb_table, norm_g)


def _s5_prep_kernel(lr_row_ref, li_row_ref, lr_col_ref, li_col_ref, ldt_ref,
                    btr_ref, bti_ref, ctr_ref, cti_ref, w1_ref, f_ref, coef_ref):
    L, J, P = S5_CHUNK, S5_GROUP, S5_STATE
    hi = lax.Precision.HIGHEST
    dt = jnp.exp(ldt_ref[0])
    lr, li = lr_row_ref[0], li_row_ref[0]
    npow = ((L - 1) - lax.broadcasted_iota(jnp.int32, (L, 2 * P), 0)).astype(F32)
    mag = jnp.exp(npow * (lr * dt))
    th = npow * (li * dt)
    ar, ai = mag * jnp.cos(th), mag * jnp.sin(th)
    half = lax.broadcasted_iota(jnp.int32, (L, 2 * P), 1) < P
    a1r, a1i = ar[L - 2:L - 1], ai[L - 2:L - 1]
    nr, ni = a1r - 1.0, a1i
    den = lr * lr + li * li
    fr, fi = (nr * lr + ni * li) / den, (ni * lr - nr * li) / den
    btr, bti = btr_ref[0], bti_ref[0]
    bbr, bbi = fr * btr - fi * bti, fr * bti + fi * btr
    pa, pb = jnp.where(half, ar, ai), jnp.where(half, -ai, ar)
    pasw, pbsw = jnp.where(half, ai, ar), jnp.where(half, ar, -ai)
    for s in range(L):
        rows = slice(s * J, (s + 1) * J)
        w1_ref[0, rows, L * J:L * J + 2 * P] = (bbr * pa[s:s + 1] + bbi * pb[s:s + 1]).astype(BF16)
        w1_ref[0, rows, L * J + 2 * P:] = (bbr * pasw[s:s + 1] + bbi * pbsw[s:s + 1]).astype(BF16)
    alr_, ali_ = ar[0:1] * a1r - ai[0:1] * a1i, ar[0:1] * a1i + ai[0:1] * a1r
    half1 = half[0:1]
    coef_ref[0, 0:1, :] = alr_
    coef_ref[0, 1:2, :] = jnp.where(half1, -ali_, ali_)
    coef_ref[0, 2:3, :] = jnp.where(half1, ali_, -ali_)
    coef_ref[0, 3:, :] = jnp.zeros((V7X_SUBLANES - 3, 2 * P), F32)

    lrc, lic = lr_col_ref[0], li_col_ref[0]
    lane = lax.broadcasted_iota(jnp.int32, (P, L * J), 1)
    lag = jnp.right_shift(lane, int(math.log2(J))).astype(F32)
    magq = jnp.exp(lag * (lrc * dt))
    thq = lag * (lic * dt)
    aqr, aqi = magq * jnp.cos(thq), magq * jnp.sin(thq)
    cr, ci = ctr_ref[0], cti_ref[0]
    qr, qi = cr * aqr - ci * aqi, cr * aqi + ci * aqr
    r = (jnp.dot(bbr[:, :P], qr, precision=hi, preferred_element_type=F32)
         - jnp.dot(bbi[:, :P], qi, precision=hi, preferred_element_type=F32))
    lane_r = lax.broadcasted_iota(jnp.int32, (J, L * J), 1)
    for s in range(L):
        blk = r if s == 0 else jnp.where(lane_r >= s * J, pltpu.roll(r, s * J, 1), 0.0)
        w1_ref[0, s * J:(s + 1) * J, 0:L * J] = blk.astype(BF16)
    m1 = jnp.exp(lrc * dt)
    c1r, c1i = m1 * jnp.cos(lic * dt), m1 * jnp.sin(lic * dt)
    f_ref[0, 0:P, :] = (qr * c1r - qi * c1i).astype(BF16)
    f_ref[0, P:, :] = (-(qr * c1i + qi * c1r)).astype(BF16)


def _s5_prep(lam_re, lam_im, log_dt, b_re, b_im, c_re, c_im):
    g = lam_re.shape[0]
    P, J, L = S5_STATE, S5_GROUP, S5_CHUNK
    dbl = lambda a: jnp.concatenate([a, a], axis=-1)
    lr_row = dbl(lam_re).reshape(g, 1, 2 * P)
    li_row = dbl(lam_im).reshape(g, 1, 2 * P)
    lr_col = lam_re.reshape(g, P, 1)
    li_col = lam_im.reshape(g, P, 1)
    ldt = log_dt.reshape(g, 1, 1)
    btr = dbl(jnp.swapaxes(b_re, 1, 2))
    bti = dbl(jnp.swapaxes(b_im, 1, 2))
    ctr = jnp.tile(jnp.swapaxes(c_re, 1, 2), (1, 1, L))
    cti = jnp.tile(jnp.swapaxes(c_im, 1, 2), (1, 1, L))
    blk = lambda *s: pl.BlockSpec((1,) + s, lambda i: (i, 0, 0))
    return pl.pallas_call(
        _s5_prep_kernel,
        grid=(g,),
        in_specs=[blk(1, 2 * P), blk(1, 2 * P), blk(P, 1), blk(P, 1), blk(1, 1),
                  blk(J, 2 * P), blk(J, 2 * P), blk(P, L * J), blk(P, L * J)],
        out_specs=[blk(L * J, L * J + 4 * P), blk(2 * P, L * J), blk(V7X_SUBLANES, 2 * P)],
        out_shape=[jax.ShapeDtypeStruct((g, L * J, L * J + 4 * P), BF16),
                   jax.ShapeDtypeStruct((g, 2 * P, L * J), BF16),
                   jax.ShapeDtypeStruct((g, V7X_SUBLANES, 2 * P), F32)],
        compiler_params=_params("parallel"),
        name="s5_prep",
    )(lr_row, li_row, lr_col, li_col, ldt, btr, bti, ctr, cti)


def _s5_main_kernel(u_ref, w1_ref, f_ref, coef_ref, d_ref, o_ref, ys_ref, sp_ref, *, nchunk, nb):
    P2 = 2 * S5_STATE
    u = u_ref[0]
    ys_ref[...] = jnp.dot(u.astype(BF16), w1_ref[0], preferred_element_type=F32)
    a1, a2, a2sw = coef_ref[0, 0:1, :], coef_ref[0, 1:2, :], coef_ref[0, 2:3, :]

    def body(c, carry):
        s, ssw = carry
        rows = pl.ds(pl.multiple_of(c * nb, nb), nb)
        sp_ref[rows, :] = s
        loc = ys_ref[rows, S5_ROW:S5_ROW + P2]
        locsw = ys_ref[rows, S5_ROW + P2:]
        return a1 * s + a2 * ssw + loc, a1 * ssw + a2sw * s + locsw

    zero = jnp.zeros((nb, P2), F32)
    lax.fori_loop(0, nchunk, body, (zero, zero), unroll=8)
    y = (ys_ref[:, 0:S5_ROW]
         + jnp.dot(sp_ref[...].astype(BF16), f_ref[0], preferred_element_type=F32)
         + d_ref[0] * u)
    o_ref[0] = jax.nn.gelu(y, approximate=True)


def _s5_main(u_g, w1, fmat, coef, d_t, *, nchunk, nb):
    g, rows, width = u_g.shape
    P2 = 2 * S5_STATE
    blk = lambda *s: pl.BlockSpec((1,) + s, lambda i: (i, 0, 0))
    return pl.pallas_call(
        functools.partial(_s5_main_kernel, nchunk=nchunk, nb=nb),
        grid=(g,),
        in_specs=[blk(rows, width), blk(width, width + 2 * P2), blk(P2, width),
                  blk(V7X_SUBLANES, P2), blk(1, width)],
        out_specs=blk(rows, width),
        out_shape=jax.ShapeDtypeStruct((g, rows, width), F32),
        scratch_shapes=[pltpu.VMEM((rows, width + 2 * P2), F32), pltpu.VMEM((rows, P2), F32)],
        compiler_params=_params("parallel"),
        name="s5_main",
    )(u_g, w1, fmat, coef, d_t)


def _pad_cols(w, n):
    return jnp.pad(w, ((0, 0), (0, n - w.shape[1])))


def _ffn_block(x_f32, x_bf, w_gate, w_up, w_down, ln_g, ln_b, *, emit_bf16):
    hidden = w_gate.shape[1]
    hpad = -(-hidden // FFN_DOWN_TK) * FFN_DOWN_TK
    wd = jnp.pad(w_down.astype(BF16), ((0, hpad - hidden), (0, 0)))
    a = _ffn_up(x_bf, w_gate, w_up, tm=1024, tn=256, n_out=hpad)
    return _matmul_residual_ln([a], wd, x_f32, ln_g, ln_b, tm=512, tk=FFN_DOWN_TK, emit_bf16=emit_bf16,
                               name="ffn_down_ln")


def _even_layer(x_f32, x_bf, bsz, seq, w_in, w_out, lam_re, lam_im, log_dt, b_re, b_im, c_re, c_im,
                d_skip, w_glu, w_alpha, b_alpha, norm_g, ln_g, ln_b):
    n = bsz * seq
    s5w = d_skip.shape[0]
    groups = s5w // S5_GROUP
    kw = GLA_HEADS * GLA_DK
    vw = GLA_HEADS * GLA_DV
    main = s5w + 2 * kw + 2 * vw
    h = _matmul(x_bf, w_in[:, :main], tm=1024, tn=512, out_dtype=F32, name="even_in")
    alr = _matmul(x_bf, _pad_cols(w_in[:, main:], V7X_LANES), tm=1024, tn=V7X_LANES,
                  out_dtype=F32, name="even_in_gate")

    w1, fmat, coef = _s5_prep(lam_re, lam_im, log_dt, b_re, b_im, c_re, c_im)
    nchunk = seq // S5_CHUNK
    u_g = (h[:, :s5w].reshape(bsz, nchunk, S5_CHUNK, groups, S5_GROUP)
           .transpose(3, 1, 0, 2, 4).reshape(groups, nchunk * bsz, S5_ROW))
    d_t = jnp.tile(d_skip.reshape(groups, 1, S5_GROUP), (1, 1, S5_CHUNK))
    y_g = _s5_main(u_g, w1, fmat, coef, d_t, nchunk=nchunk, nb=bsz)
    y5 = (y_g.reshape(groups, nchunk, bsz, S5_CHUNK, S5_GROUP)
          .transpose(2, 1, 3, 0, 4).reshape(n, s5w))
    y_s5 = _glu(y5.astype(BF16), y5, w_glu.astype(BF16), tm=1024, tn=512)

    wa = jnp.pad(w_alpha, ((0, V7X_LANES - w_alpha.shape[0]), (0, 0))).astype(BF16)
    y_gla = _gla(h, alr, wa, b_alpha.reshape(1, kw), norm_g.reshape(1, GLA_DV), bsz=bsz, seq=seq,
                 col_q=s5w, col_k=s5w + kw, col_v=s5w + 2 * kw, col_g=s5w + 2 * kw + vw,
                 tt=256, hb=4, unroll=2)

    return _matmul_residual_ln([y_s5, y_gla], w_out.astype(BF16), x_f32, ln_g, ln_b, tm=512, tk=1024,
                               emit_bf16=True, name="even_out_ln")


def _odd_layer(x_f32, x_bf, bsz, seq, layer, w_in, w_out, lb_table, norm_g, ln_g, ln_b):
    heads = w_out.shape[0] // HG_DV
    h = _matmul(x_bf, w_in, tm=1024, tn=512, out_dtype=F32, name="odd_in")
    y = _hgrn(h, lb_table, norm_g.reshape(1, HG_DV), layer=layer, bsz=bsz, seq=seq, heads=heads,
              tt=256, hb=8, unroll=2)
    return _matmul_residual_ln([y], w_out.astype(BF16), x_f32, ln_g, ln_b, tm=512, tk=1024,
                               emit_bf16=True, name="odd_out_ln")


def kernel(x, ev_w_in, ev_w_out, s5_lam_re, s5_lam_im, s5_log_dt, s5_b_re, s5_b_im, s5_c_re, s5_c_im,
           s5_d, s5_w_glu, gla_w_alpha, gla_b_alpha, gla_norm_g, od_w_in, od_w_out, hg_lb_table,
           hg_norm_g, ln_mix_g, ln_mix_b, ln_ffn_g, ln_ffn_b, ffn_w_gate, ffn_w_up, ffn_w_down):
    bsz, seq, d = x.shape
    depth = ln_mix_g.shape[0]
    assert depth == DEPTH
    xf = x.reshape(bsz * seq, d)
    xb = xf.astype(BF16)
    for layer in range(depth):
        if layer % 2 == 0:
            e = layer // 2
            xf, xb = _even_layer(xf, xb, bsz, seq, ev_w_in[e], ev_w_out[e], s5_lam_re[e], s5_lam_im[e],
                                 s5_log_dt[e], s5_b_re[e], s5_b_im[e], s5_c_re[e], s5_c_im[e], s5_d[e],
                                 s5_w_glu[e], gla_w_alpha[e], gla_b_alpha[e], gla_norm_g[e],
                                 ln_mix_g[layer], ln_mix_b[layer])
        else:
            o = layer // 2
            xf, xb = _odd_layer(xf, xb, bsz, seq, layer, od_w_in[o], od_w_out[o], hg_lb_table,
                                hg_norm_g[o], ln_mix_g[layer], ln_mix_b[layer])
        last = layer == depth - 1
        res = _ffn_block(xf, xb, ffn_w_gate[layer], ffn_w_up[layer], ffn_w_down[layer],
                         ln_ffn_g[layer], ln_ffn_b[layer], emit_bf16=not last)
        xf, xb = (res[0], None) if last else res
    return xf.reshape(bsz, seq, d).astype(x.dtype)
```

```python
import functools
import math

import jax
import jax.numpy as jnp
from jax import lax
from jax.experimental import pallas as pl
from jax.experimental.pallas import tpu as pltpu

F32 = jnp.float32
BF16 = jnp.bfloat16

DEPTH = 2
S5_GROUP = 16
S5_STATE = 64
GLA_HEADS = 8
GLA_DK = 128
GLA_DV = 256
GLA_GATE_RANK = 16
GLA_TAU = 16.0
HG_DK = 128
HG_DV = 128
CHUNK = 64
DEEPNORM_ALPHA = (2.0 * DEPTH) ** 0.25
NORM_EPS = 1e-5

V7X_LANES = 128
V7X_SUBLANES = 8
V7X_VMEM_LIMIT_BYTES = 56 * 1024 * 1024

FFN_DOWN_TK = 512
LN_COL_BLOCK = 1024
LN_ROW_BLOCK = 16

S5_CHUNK = 16
S5_ROW = S5_CHUNK * S5_GROUP
S5_LANE_GROUPS = V7X_LANES // S5_GROUP


def _params(*sem):
    return pltpu.CompilerParams(dimension_semantics=sem, vmem_limit_bytes=V7X_VMEM_LIMIT_BYTES)


def _mm_kernel(a_ref, w_ref, o_ref, wb_ref):
    @pl.when(pl.program_id(1) == 0)
    def _():
        wb_ref[...] = w_ref[...].astype(BF16)

    o_ref[...] = jnp.dot(a_ref[...], wb_ref[...], preferred_element_type=F32).astype(o_ref.dtype)


def _matmul(a, w, layer, *, n, tm, tn, out_dtype, name):
    m, k = a.shape
    return pl.pallas_call(
        _mm_kernel,
        grid=(n // tn, m // tm),
        in_specs=[pl.BlockSpec((tm, k), lambda j, i: (i, 0)),
                  pl.BlockSpec((None, k, tn), lambda j, i: (layer, 0, j))],
        out_specs=pl.BlockSpec((tm, tn), lambda j, i: (i, j)),
        out_shape=jax.ShapeDtypeStruct((m, n), out_dtype),
        scratch_shapes=[pltpu.VMEM((k, tn), BF16)],
        compiler_params=_params("arbitrary", "arbitrary"),
        name=name,
    )(a, w)


def _ffn_up_kernel(x_ref, wg_ref, wu_ref, o_ref, wgb_ref, wub_ref, *, n_real):
    j = pl.program_id(0)

    @pl.when(j < n_real)
    def _():
        @pl.when(pl.program_id(1) == 0)
        def _():
            wgb_ref[...] = wg_ref[...].astype(BF16)
            wub_ref[...] = wu_ref[...].astype(BF16)

        x = x_ref[...]
        g = jnp.dot(x, wgb_ref[...], preferred_element_type=F32)
        u = jnp.dot(x, wub_ref[...], preferred_element_type=F32)
        o_ref[...] = (g * jax.nn.sigmoid(g) * u).astype(o_ref.dtype)

    @pl.when(j >= n_real)
    def _():
        o_ref[...] = jnp.zeros_like(o_ref)


def _ffn_up(x_bf, wg, wu, layer, *, tm, tn, n_out):
    m, k = x_bf.shape
    n_real = wg.shape[2] // tn
    w_spec = pl.BlockSpec((None, k, tn), lambda j, i: (layer, 0, jnp.minimum(j, n_real - 1)))
    return pl.pallas_call(
        functools.partial(_ffn_up_kernel, n_real=n_real),
        grid=(n_out // tn, m // tm),
        in_specs=[pl.BlockSpec((tm, k), lambda j, i: (i, 0)), w_spec, w_spec],
        out_specs=pl.BlockSpec((tm, tn), lambda j, i: (i, j)),
        out_shape=jax.ShapeDtypeStruct((m, n_out), BF16),
        scratch_shapes=[pltpu.VMEM((k, tn), BF16), pltpu.VMEM((k, tn), BF16)],
        compiler_params=_params("arbitrary", "arbitrary"),
        name="ffn_up",
    )(x_bf, wg, wu)


def _mm_ln_kernel(*refs, nk_parts):
    n_a = len(nk_parts)
    a_refs = refs[:n_a]
    w_ref, r_ref, g_ref, b_ref, of_ref = refs[n_a:n_a + 5]
    n_out = len(refs) - n_a - 6
    maybe_ob_ref = refs[n_a + 5:n_a + 5 + n_out - 1]
    mu_ref, rs_ref = refs[-2:]
    k = pl.program_id(1)

    @pl.when(k == 0)
    def _():
        of_ref[...] = DEEPNORM_ALPHA * r_ref[...]

    def accumulate(a_ref):
        a = a_ref[...]
        for c0 in range(0, of_ref.shape[1], LN_COL_BLOCK):
            cols = slice(c0, c0 + LN_COL_BLOCK)
            of_ref[:, cols] += jnp.dot(a, w_ref[:, cols], preferred_element_type=F32)

    if n_a == 1:
        accumulate(a_refs[0])
    else:
        k0 = 0
        for a_ref, nkp in zip(a_refs, nk_parts):
            pl.when(jnp.logical_and(k >= k0, k < k0 + nkp))(functools.partial(accumulate, a_ref))
            k0 += nkp

    @pl.when(k == sum(nk_parts) - 1)
    def _():
        inv_n = 1.0 / of_ref.shape[1]
        nblk = of_ref.shape[0] // LN_ROW_BLOCK
        lanes = mu_ref.shape[1]
        nrep = of_ref.shape[1] // lanes

        def block(r):
            return pl.ds(pl.multiple_of(r * LN_ROW_BLOCK, LN_ROW_BLOCK), LN_ROW_BLOCK)

        def mean_rows(r, carry):
            rows = block(r)
            mu = jnp.sum(of_ref[rows, :], axis=-1, keepdims=True) * inv_n
            mu_ref[rows, :] = jnp.broadcast_to(mu, (LN_ROW_BLOCK, lanes))
            return carry

        def rstd_rows(r, carry):
            rows = block(r)
            zc = of_ref[rows, :] - jnp.tile(mu_ref[rows, :], (1, nrep))
            var = jnp.sum(zc * zc, axis=-1, keepdims=True) * inv_n
            rs_ref[rows, :] = jnp.broadcast_to(lax.rsqrt(var + NORM_EPS), (LN_ROW_BLOCK, lanes))
            return carry

        def norm_rows(r, carry):
            rows = block(r)
            mu = jnp.tile(mu_ref[rows, :], (1, LN_COL_BLOCK // lanes))
            rs = jnp.tile(rs_ref[rows, :], (1, LN_COL_BLOCK // lanes))
            for c0 in range(0, of_ref.shape[1], LN_COL_BLOCK):
                cols = slice(c0, c0 + LN_COL_BLOCK)
                y = (of_ref[rows, cols] - mu) * rs * g_ref[:, cols] + b_ref[:, cols]
                of_ref[rows, cols] = y
                for ob_ref in maybe_ob_ref:
                    ob_ref[rows, cols] = y.astype(ob_ref.dtype)
            return carry

        lax.fori_loop(0, nblk, mean_rows, 0, unroll=4)
        lax.fori_loop(0, nblk, rstd_rows, 0, unroll=4)
        lax.fori_loop(0, nblk, norm_rows, 0, unroll=2)


def _matmul_residual_ln(a_parts, w, resid, gain, bias, *, tm, tk, emit_bf16, name):
    m = resid.shape[0]
    n = w.shape[1]
    nk_parts = tuple(a.shape[1] // tk for a in a_parts)
    row_spec = pl.BlockSpec((tm, n), lambda i, k: (i, 0))
    vec_spec = pl.BlockSpec((1, n), lambda i, k: (0, 0))
    a_specs, k0 = [], 0
    for nkp in nk_parts:
        a_specs.append(pl.BlockSpec((tm, tk), lambda i, k, k0=k0, nkp=nkp: (i, jnp.clip(k - k0, 0, nkp - 1))))
        k0 += nkp
    out_shape = [jax.ShapeDtypeStruct((m, n), F32)]
    out_specs = [row_spec]
    if emit_bf16:
        out_shape.append(jax.ShapeDtypeStruct((m, n), BF16))
        out_specs.append(row_spec)
    return pl.pallas_call(
        functools.partial(_mm_ln_kernel, nk_parts=nk_parts),
        grid=(m // tm, sum(nk_parts)),
        in_specs=a_specs + [pl.BlockSpec((tk, n), lambda i, k: (k, 0)),
                            row_spec,
                            vec_spec, vec_spec],
        out_specs=out_specs,
        out_shape=out_shape,
        scratch_shapes=[pltpu.VMEM((tm, V7X_LANES), F32), pltpu.VMEM((tm, V7X_LANES), F32)],
        compiler_params=_params("parallel", "arbitrary"),
        name=name,
    )(*a_parts, w, resid, gain.reshape(1, n), bias.reshape(1, n))


def _chunk_steps(qs, ks, vs, las, st_ref, tril_bf, causal):
    c = qs[0].shape[0]
    nt = (((1,), (1,)), ((), ()))
    tn = (((0,), (0,)), ((), ()))
    heads = range(len(qs))
    bs = []
    for la in las:
        la_hi = la.astype(BF16)
        la_lo = (la - la_hi.astype(F32)).astype(BF16)
        bs.append(jnp.dot(tril_bf, la_hi, preferred_element_type=F32)
                  + jnp.dot(tril_bf, la_lo, preferred_element_type=F32))
    q_decs, scores, kvs, inters, decays = [], [], [], [], []
    for hd in heads:
        b = bs[hd]
        b_end = b[c - 1:c, :]
        q_dec = (qs[hd] * jnp.exp(b)).astype(BF16)
        k_inv = (ks[hd] * jnp.exp(-b)).astype(BF16)
        k_end = (ks[hd] * jnp.exp(b_end - b)).astype(BF16)
        v_bf = vs[hd].astype(BF16)
        vs[hd] = v_bf
        scores.append(lax.dot_general(q_dec, k_inv, nt, preferred_element_type=F32))
        kvs.append(lax.dot_general(v_bf, k_end, tn, preferred_element_type=F32))
        inters.append(lax.dot_general(q_dec, st_ref[hd].astype(BF16), nt, preferred_element_type=F32))
        decays.append(jnp.exp(b_end))
    outs = []
    for hd in heads:
        sc = jnp.where(causal, scores[hd], 0.0).astype(BF16)
        outs.append(jnp.dot(sc, vs[hd], preferred_element_type=F32) + inters[hd])
        st_ref[hd] = decays[hd] * st_ref[hd] + kvs[hd]
    return outs


def _gated_rmsnorm(o, gain, gate):
    ms = jnp.mean(o * o, axis=-1, keepdims=True)
    return o * lax.rsqrt(ms + NORM_EPS) * gain * (gate * jax.nn.sigmoid(gate))


def _chunk_masks():
    row = lax.broadcasted_iota(jnp.int32, (CHUNK, CHUNK), 0)
    col = lax.broadcasted_iota(jnp.int32, (CHUNK, CHUNK), 1)
    causal = row >= col
    return jnp.where(causal, 1.0, 0.0).astype(BF16), causal


def _log_sigmoid(z):
    return jnp.minimum(z, 0.0) - jnp.log1p(jnp.exp(-jnp.abs(z)))


def _gla_kernel(q_ref, k_ref, v_ref, g_ref, alr_ref, wa_ref, ba_ref, gain_ref, o_ref, st_ref,
                *, hb, nchunks, unroll):
    @pl.when(pl.program_id(2) == 0)
    def _():
        st_ref[...] = jnp.zeros_like(st_ref)

    tril_bf, causal = _chunk_masks()
    gain = gain_ref[...]
    ba = ba_ref[...]
    wa = wa_ref[...]

    def body(c, carry):
        rows = pl.ds(pl.multiple_of(c * CHUNK, CHUNK), CHUNK)
        z = jnp.dot(alr_ref[rows, :].astype(BF16), wa, preferred_element_type=F32) + ba
        la_all = _log_sigmoid(z) / GLA_TAU
        kcs = [slice(hd * GLA_DK, (hd + 1) * GLA_DK) for hd in range(hb)]
        vcs = [slice(hd * GLA_DV, (hd + 1) * GLA_DV) for hd in range(hb)]
        outs = _chunk_steps([q_ref[rows, kc] * (GLA_DK ** -0.5) for kc in kcs],
                            [k_ref[rows, kc] for kc in kcs], [v_ref[rows, vc] for vc in vcs],
                            [la_all[:, kc] for kc in kcs], st_ref, tril_bf, causal)
        for o, vc in zip(outs, vcs):
            o_ref[rows, vc] = _gated_rmsnorm(o, gain, g_ref[rows, vc]).astype(o_ref.dtype)
        return carry

    lax.fori_loop(0, nchunks, body, 0, unroll=unroll)


def _gla(h, alr, w_alpha, b_alpha, norm_g, *, bsz, seq, col_q, col_k, col_v, col_g, tt, hb, unroll):
    n = bsz * seq
    nt = seq // tt
    kw, vw = hb * GLA_DK, hb * GLA_DV
    hg = GLA_HEADS // hb

    def rows(b, g, t):
        return b * nt + t

    return pl.pallas_call(
        functools.partial(_gla_kernel, hb=hb, nchunks=tt // CHUNK, unroll=unroll),
        grid=(bsz, hg, nt),
        in_specs=[pl.BlockSpec((tt, kw), lambda b, g, t: (rows(b, g, t), col_q // kw + g)),
                  pl.BlockSpec((tt, kw), lambda b, g, t: (rows(b, g, t), col_k // kw + g)),
                  pl.BlockSpec((tt, vw), lambda b, g, t: (rows(b, g, t), col_v // vw + g)),
                  pl.BlockSpec((tt, vw), lambda b, g, t: (rows(b, g, t), col_g // vw + g)),
                  pl.BlockSpec((tt, V7X_LANES), lambda b, g, t: (rows(b, g, t), 0)),
                  pl.BlockSpec((V7X_LANES, kw), lambda b, g, t: (0, g)),
                  pl.BlockSpec((1, kw), lambda b, g, t: (0, g)),
                  pl.BlockSpec((1, GLA_DV), lambda b, g, t: (0, 0))],
        out_specs=pl.BlockSpec((tt, vw), lambda b, g, t: (rows(b, g, t), g)),
        out_shape=jax.ShapeDtypeStruct((n, GLA_HEADS * GLA_DV), BF16),
        scratch_shapes=[pltpu.VMEM((hb, GLA_DV, GLA_DK), F32)],
        compiler_params=_params("parallel", "parallel", "arbitrary"),
        name="gla",
    )(h, h, h, h, alr, w_alpha, b_alpha, norm_g)


def _hgrn_kernel(q_ref, f_ref, i_ref, g_ref, tab_ref, gain_ref, o_ref, st_ref, *, layer, hb, nchunks,
                 unroll):
    @pl.when(pl.program_id(2) == 0)
    def _():
        st_ref[...] = jnp.zeros_like(st_ref)

    tab = tab_ref[...]
    e = jnp.exp(tab - jnp.max(tab, axis=0, keepdims=True))
    soft = e / jnp.sum(e, axis=0, keepdims=True)
    lb_all = jnp.sum(soft[0:layer + 1], axis=0, keepdims=True) - soft[0:1]
    tril_bf, causal = _chunk_masks()
    gain = gain_ref[...]

    def body(c, carry):
        rows = pl.ds(pl.multiple_of(c * CHUNK, CHUNK), CHUNK)
        kcs = [slice(hd * HG_DK, (hd + 1) * HG_DK) for hd in range(hb)]
        vcs = [slice(hd * HG_DV, (hd + 1) * HG_DV) for hd in range(hb)]
        fs = [lb_all[:, kc] + (1.0 - lb_all[:, kc]) * jax.nn.sigmoid(f_ref[rows, kc]) for kc in kcs]
        ivs = [i_ref[rows, vc] for vc in vcs]
        outs = _chunk_steps([q_ref[rows, kc] for kc in kcs], [1.0 - f for f in fs],
                            [iv * jax.nn.sigmoid(iv) for iv in ivs], [jnp.log(f) for f in fs],
                            st_ref, tril_bf, causal)
        for o, vc in zip(outs, vcs):
            o_ref[rows, vc] = _gated_rmsnorm(o, gain, g_ref[rows, vc]).astype(o_ref.dtype)
        return carry

    lax.fori_loop(0, nchunks, body, 0, unroll=unroll)


def _hgrn(h, lb_table, norm_g, *, layer, bsz, seq, heads, tt, hb, unroll):
    n = bsz * seq
    nt = seq // tt
    w = hb * HG_DK
    hg = heads // hb

    def spec(seg):
        return pl.BlockSpec((tt, w), lambda b, g, t: (b * nt + t, seg * hg + g))

    return pl.pallas_call(
        functools.partial(_hgrn_kernel, layer=layer, hb=hb, nchunks=tt // CHUNK, unroll=unroll),
        grid=(bsz, hg, nt),
        in_specs=[spec(0), spec(1), spec(2), spec(3),
                  pl.BlockSpec((DEPTH, w), lambda b, g, t: (0, g)),
                  pl.BlockSpec((1, HG_DV), lambda b, g, t: (0, 0))],
        out_specs=pl.BlockSpec((tt, w), lambda b, g, t: (b * nt + t, g)),
        out_shape=jax.ShapeDtypeStruct((n, heads * HG_DV), BF16),
        scratch_shapes=[pltpu.VMEM((hb, HG_DV, HG_DK), F32)],
        compiler_params=_params("parallel", "parallel", "arbitrary"),
        name="hgrn2",
    )(h, h, h, h, lb_table, norm_g)


def _s5_prep_kernel(lr_row_ref, li_row_ref, lr_col_ref, li_col_ref, ldt_ref,
                    btr_ref, bti_ref, ctr_ref, cti_ref, w1_ref, f_ref, coef_ref):
    L, J, P = S5_CHUNK, S5_GROUP, S5_STATE
    hi = lax.Precision.HIGHEST
    dt = jnp.exp(ldt_ref[0])
    lr, li = lr_row_ref[0], li_row_ref[0]
    npow = ((L - 1) - lax.broadcasted_iota(jnp.int32, (L, 2 * P), 0)).astype(F32)
    mag = jnp.exp(npow * (lr * dt))
    th = npow * (li * dt)
    ar, ai = mag * jnp.cos(th), mag * jnp.sin(th)
    half = lax.broadcasted_iota(jnp.int32, (L, 2 * P), 1) < P
    a1r, a1i = ar[L - 2:L - 1], ai[L - 2:L - 1]
    nr, ni = a1r - 1.0, a1i
    den = lr * lr + li * li
    fr, fi = (nr * lr + ni * li) / den, (ni * lr - nr * li) / den
    btr, bti = btr_ref[0], bti_ref[0]
    bbr, bbi = fr * btr - fi * bti, fr * bti + fi * btr
    pa, pb = jnp.where(half, ar, ai), jnp.where(half, -ai, ar)
    for s in range(L):
        rows = slice(s * J, (s + 1) * J)
        w1_ref[0, rows, L * J:] = (bbr * pa[s:s + 1] + bbi * pb[s:s + 1]).astype(BF16)
    alr_, ali_ = ar[0:1] * a1r - ai[0:1] * a1i, ar[0:1] * a1i + ai[0:1] * a1r
    half1 = half[0:1]
    coef_ref[0, 0:1, :] = alr_
    coef_ref[0, 1:2, :] = jnp.where(half1, -ali_, ali_)
    coef_ref[0, 2:, :] = jnp.zeros((V7X_SUBLANES - 2, 2 * P), F32)

    lrc, lic = lr_col_ref[0], li_col_ref[0]
    lane = lax.broadcasted_iota(jnp.int32, (P, L * J), 1)
    lag = jnp.right_shift(lane, int(math.log2(J))).astype(F32)
    magq = jnp.exp(lag * (lrc * dt))
    thq = lag * (lic * dt)
    aqr, aqi = magq * jnp.cos(thq), magq * jnp.sin(thq)
    cr, ci = ctr_ref[0], cti_ref[0]
    qr, qi = cr * aqr - ci * aqi, cr * aqi + ci * aqr
    r = (jnp.dot(bbr[:, :P], qr, precision=hi, preferred_element_type=F32)
         - jnp.dot(bbi[:, :P], qi, precision=hi, preferred_element_type=F32))
    lane_r = lax.broadcasted_iota(jnp.int32, (J, L * J), 1)
    for s in range(L):
        blk = r if s == 0 else jnp.where(lane_r >= s * J, pltpu.roll(r, s * J, 1), 0.0)
        w1_ref[0, s * J:(s + 1) * J, 0:L * J] = blk.astype(BF16)
    m1 = jnp.exp(lrc * dt)
    c1r, c1i = m1 * jnp.cos(lic * dt), m1 * jnp.sin(lic * dt)
    f_ref[0, 0:P, :] = (qr * c1r - qi * c1i).astype(BF16)
    f_ref[0, P:, :] = (-(qr * c1i + qi * c1r)).astype(BF16)


def _s5_prep(lam_re, lam_im, log_dt, b_re, b_im, c_re, c_im):
    g = lam_re.shape[0]
    P, J, L = S5_STATE, S5_GROUP, S5_CHUNK
    dbl = lambda a: jnp.concatenate([a, a], axis=-1)
    lr_row = dbl(lam_re).reshape(g, 1, 2 * P)
    li_row = dbl(lam_im).reshape(g, 1, 2 * P)
    lr_col = lam_re.reshape(g, P, 1)
    li_col = lam_im.reshape(g, P, 1)
    ldt = log_dt.reshape(g, 1, 1)
    btr = dbl(jnp.swapaxes(b_re, 1, 2))
    bti = dbl(jnp.swapaxes(b_im, 1, 2))
    ctr = jnp.tile(jnp.swapaxes(c_re, 1, 2), (1, 1, L))
    cti = jnp.tile(jnp.swapaxes(c_im, 1, 2), (1, 1, L))
    blk = lambda *s: pl.BlockSpec((1,) + s, lambda i: (i, 0, 0))
    return pl.pallas_call(
        _s5_prep_kernel,
        grid=(g,),
        in_specs=[blk(1, 2 * P), blk(1, 2 * P), blk(P, 1), blk(P, 1), blk(1, 1),
                  blk(J, 2 * P), blk(J, 2 * P), blk(P, L * J), blk(P, L * J)],
        out_specs=[blk(L * J, L * J + 2 * P), blk(2 * P, L * J), blk(V7X_SUBLANES, 2 * P)],
        out_shape=[jax.ShapeDtypeStruct((g, L * J, L * J + 2 * P), BF16),
                   jax.ShapeDtypeStruct((g, 2 * P, L * J), BF16),
                   jax.ShapeDtypeStruct((g, V7X_SUBLANES, 2 * P), F32)],
        compiler_params=_params("parallel"),
        name="s5_prep",
    )(lr_row, li_row, lr_col, li_col, ldt, btr, bti, ctr, cti)


def _s5_block_weights(w1, fmat, coef):
    g = w1.shape[0]
    L, J, P, Q = S5_CHUNK, S5_GROUP, S5_STATE, S5_LANE_GROUPS
    nb = g // Q
    eye = jnp.eye(Q, dtype=w1.dtype)
    t = w1[:, :, :L * J].reshape(nb, Q, L, J, L, 1, J) * eye.reshape(1, Q, 1, 1, 1, Q, 1)
    t = t.transpose(0, 2, 1, 3, 4, 5, 6).reshape(nb, L * Q * J, L * Q * J)
    e = w1[:, :, L * J:].reshape(nb, Q, L, J, 2, 1, P) * eye.reshape(1, Q, 1, 1, 1, Q, 1)
    e = e.transpose(0, 2, 1, 3, 4, 5, 6).reshape(nb, L * Q * J, 2 * Q * P)
    wbig = jnp.concatenate([t, e], axis=-1)
    f = fmat.reshape(nb, Q, 2, P, L, 1, J) * eye.reshape(1, Q, 1, 1, 1, Q, 1)
    fbig = f.transpose(0, 2, 1, 3, 4, 5, 6).reshape(nb, 2 * Q * P, L * Q * J)
    cbig = (coef.reshape(nb, Q, V7X_SUBLANES, 2, P).transpose(0, 2, 3, 1, 4)
            .reshape(nb, V7X_SUBLANES, 2 * Q * P))
    return wbig, fbig, cbig


def _s5_main_kernel(u_ref, w_ref, f_ref, coef_ref, o_ref, ys_ref, sp_ref, st_ref, *, nchunk, nb):
    width = u_ref.shape[2]
    half = st_ref.shape[1] // 2

    @pl.when(pl.program_id(1) == 0)
    def _():
        st_ref[...] = jnp.zeros_like(st_ref)

    ys_ref[...] = jnp.dot(u_ref[0], w_ref[0], preferred_element_type=F32)
    a1, a2 = coef_ref[0, 0:1, :], coef_ref[0, 1:2, :]

    def body(c, s):
        rows = pl.ds(pl.multiple_of(c * nb, nb), nb)
        sp_ref[rows, :] = s
        swapped = jnp.concatenate([s[:, half:], s[:, :half]], axis=1)
        return a1 * s + a2 * swapped + ys_ref[rows, width:]

    st_ref[...] = lax.fori_loop(0, nchunk, body, st_ref[...], unroll=4)
    y = ys_ref[:, 0:width] + jnp.dot(sp_ref[...].astype(BF16), f_ref[0], preferred_element_type=F32)
    o_ref[0] = y.astype(o_ref.dtype)


def _s5_main(u_c, wbig, fbig, cbig, *, nb, rows_per_step):
    nblk, rows, width = u_c.shape
    nstate = fbig.shape[1]
    nsteps = rows // rows_per_step
    fixed = lambda *s: pl.BlockSpec((1,) + s, lambda i, r: (i, 0, 0))
    moving = pl.BlockSpec((1, rows_per_step, width), lambda i, r: (i, r, 0))
    return pl.pallas_call(
        functools.partial(_s5_main_kernel, nchunk=rows_per_step // nb, nb=nb),
        grid=(nblk, nsteps),
        in_specs=[moving, fixed(width, width + nstate), fixed(nstate, width),
                  fixed(V7X_SUBLANES, nstate)],
        out_specs=moving,
        out_shape=jax.ShapeDtypeStruct((nblk, rows, width), BF16),
        scratch_shapes=[pltpu.VMEM((rows_per_step, width + nstate), F32),
                        pltpu.VMEM((rows_per_step, nstate), F32),
                        pltpu.VMEM((nb, nstate), F32)],
        compiler_params=_params("parallel", "arbitrary"),
        name="s5_main",
    )(u_c, wbig, fbig, cbig)


def _s5_glu_kernel(ys_ref, u_ref, d_ref, w_ref, o_ref):
    y = jax.nn.gelu(ys_ref[...].astype(F32) + d_ref[...] * u_ref[...], approximate=True)
    z = jnp.dot(y.astype(BF16), w_ref[...], preferred_element_type=F32)
    o_ref[...] = (y * jax.nn.sigmoid(z)).astype(o_ref.dtype)


def _s5_glu(y_ssm, h, d_skip, w_glu, *, tm):
    m, n = y_ssm.shape
    return pl.pallas_call(
        _s5_glu_kernel,
        grid=(m // tm,),
        in_specs=[pl.BlockSpec((tm, n), lambda i: (i, 0)),
                  pl.BlockSpec((tm, n), lambda i: (i, 0)),
                  pl.BlockSpec((1, n), lambda i: (0, 0)),
                  pl.BlockSpec((n, n), lambda i: (0, 0))],
        out_specs=pl.BlockSpec((tm, n), lambda i: (i, 0)),
        out_shape=jax.ShapeDtypeStruct((m, n), BF16),
        compiler_params=_params("parallel"),
        name="s5_glu",
    )(y_ssm, h, d_skip.reshape(1, n), w_glu)


def _pad_cols(w, n):
    return jnp.pad(w, ((0, 0), (0, n - w.shape[1])))


def _ffn_block(x_f32, x_bf, w_gate, w_up, w_down, layer, ln_g, ln_b, *, emit_bf16):
    hidden = w_gate.shape[2]
    hpad = -(-hidden // FFN_DOWN_TK) * FFN_DOWN_TK
    wd = jnp.pad(w_down[layer].astype(BF16), ((0, hpad - hidden), (0, 0)))
    a = _ffn_up(x_bf, w_gate, w_up, layer, tm=1024, tn=256, n_out=hpad)
    return _matmul_residual_ln([a], wd, x_f32, ln_g, ln_b, tm=512, tk=FFN_DOWN_TK, emit_bf16=emit_bf16,
                               name="ffn_down_ln")


def _even_layer(x_f32, x_bf, bsz, seq, e, w_in, w_out, lam_re, lam_im, log_dt, b_re, b_im, c_re, c_im,
                d_skip, w_glu, w_alpha, b_alpha, norm_g, ln_g, ln_b):
    n = bsz * seq
    s5w = d_skip.shape[0]
    groups = s5w // S5_GROUP
    kw = GLA_HEADS * GLA_DK
    vw = GLA_HEADS * GLA_DV
    main = s5w + 2 * kw + 2 * vw
    h = _matmul(x_bf, w_in, e, n=main, tm=1024, tn=512, out_dtype=F32, name="even_in")
    alr = _matmul(x_bf, _pad_cols(w_in[e, :, main:], V7X_LANES)[None], 0, n=V7X_LANES, tm=1024,
                  tn=V7X_LANES, out_dtype=F32, name="even_in_gate")

    w1, fmat, coef = _s5_prep(lam_re, lam_im, log_dt, b_re, b_im, c_re, c_im)
    wbig, fbig, cbig = _s5_block_weights(w1, fmat, coef)
    nchunk = seq // S5_CHUNK
    nblk = s5w // V7X_LANES
    u_c = (h[:, :s5w].astype(BF16).reshape(bsz, nchunk, S5_CHUNK, nblk, V7X_LANES)
           .transpose(3, 1, 0, 2, 4).reshape(nblk, nchunk * bsz, S5_CHUNK * V7X_LANES))
    y_c = _s5_main(u_c, wbig, fbig, cbig, nb=bsz, rows_per_step=256)
    y_ssm = (y_c.reshape(nblk, nchunk, bsz, S5_CHUNK, V7X_LANES)
             .transpose(2, 1, 3, 0, 4).reshape(n, s5w))
    y_s5 = _s5_glu(y_ssm, h, d_skip, w_glu.astype(BF16), tm=512)

    wa = jnp.pad(w_alpha, ((0, V7X_LANES - w_alpha.shape[0]), (0, 0))).astype(BF16)
    y_gla = _gla(h, alr, wa, b_alpha.reshape(1, kw), norm_g.reshape(1, GLA_DV), bsz=bsz, seq=seq,
                 col_q=s5w, col_k=s5w + kw, col_v=s5w + 2 * kw, col_g=s5w + 2 * kw + vw,
                 tt=256, hb=4, unroll=2)

    return _matmul_residual_ln([y_s5, y_gla], w_out.astype(BF16), x_f32, ln_g, ln_b, tm=512, tk=512,
                               emit_bf16=True, name="even_out_ln")


def _odd_layer(x_f32, x_bf, bsz, seq, layer, o, w_in, w_out, lb_table, norm_g, ln_g, ln_b):
    heads = w_out.shape[0] // HG_DV
    h = _matmul(x_bf, w_in, o, n=w_in.shape[2], tm=1024, tn=512, out_dtype=F32, name="odd_in")
    y = _hgrn(h, lb_table, norm_g.reshape(1, HG_DV), layer=layer, bsz=bsz, seq=seq, heads=heads,
              tt=256, hb=8, unroll=2)
    return _matmul_residual_ln([y], w_out.astype(BF16), x_f32, ln_g, ln_b, tm=512, tk=512,
                               emit_bf16=True, name="odd_out_ln")


def kernel(x, ev_w_in, ev_w_out, s5_lam_re, s5_lam_im, s5_log_dt, s5_b_re, s5_b_im, s5_c_re, s5_c_im,
           s5_d, s5_w_glu, gla_w_alpha, gla_b_alpha, gla_norm_g, od_w_in, od_w_out, hg_lb_table,
           hg_norm_g, ln_mix_g, ln_mix_b, ln_ffn_g, ln_ffn_b, ffn_w_gate, ffn_w_up, ffn_w_down):
    bsz, seq, d = x.shape
    depth = ln_mix_g.shape[0]
    assert depth == DEPTH
    xf = x.reshape(bsz * seq, d)
    xb = xf.astype(BF16)
    for layer in range(depth):
        if layer % 2 == 0:
            e = layer // 2
            xf, xb = _even_layer(xf, xb, bsz, seq, e, ev_w_in, ev_w_out[e], s5_lam_re[e], s5_lam_im[e],
                                 s5_log_dt[e], s5_b_re[e], s5_b_im[e], s5_c_re[e], s5_c_im[e], s5_d[e],
                                 s5_w_glu[e], gla_w_alpha[e], gla_b_alpha[e], gla_norm_g[e],
                                 ln_mix_g[layer], ln_mix_b[layer])
        else:
            o = layer // 2
            xf, xb = _odd_layer(xf, xb, bsz, seq, layer, o, od_w_in, od_w_out[o], hg_lb_table,
                                hg_norm_g[o], ln_mix_g[layer], ln_mix_b[layer])
        last = layer == depth - 1
        res = _ffn_block(xf, xb, ffn_w_gate, ffn_w_up, ffn_w_down, layer,
                         ln_ffn_g[layer], ln_ffn_b[layer], emit_bf16=not last)
        xf, xb = (res[0], None) if last else res
    return xf.reshape(bsz, seq, d).astype(x.dtype)
```

```python
import functools
import math

import jax
import jax.numpy as jnp
from jax import lax
from jax.experimental import pallas as pl
from jax.experimental.pallas import tpu as pltpu

F32 = jnp.float32
BF16 = jnp.bfloat16

DEPTH = 2
S5_GROUP = 16
S5_STATE = 64
GLA_HEADS = 8
GLA_DK = 128
GLA_DV = 256
GLA_GATE_RANK = 16
GLA_TAU = 16.0
HG_DK = 128
HG_DV = 128
CHUNK = 64
DEEPNORM_ALPHA = (2.0 * DEPTH) ** 0.25
NORM_EPS = 1e-5

V7X_LANES = 128
V7X_SUBLANES = 8
V7X_VMEM_LIMIT_BYTES = 56 * 1024 * 1024

MM_ROW_CHUNK = 256
FFN_DOWN_TK = 512
LN_COL_BLOCK = 1024
LN_ROW_BLOCK = 16

S5_CHUNK = 16
S5_ROW = S5_CHUNK * S5_GROUP
S5_LANE_GROUPS = V7X_LANES // S5_GROUP


def _params(*sem):
    return pltpu.CompilerParams(dimension_semantics=sem, vmem_limit_bytes=V7X_VMEM_LIMIT_BYTES)


def _mm_kernel(a_ref, w_ref, o_ref, wb_ref):
    @pl.when(pl.program_id(1) == 0)
    def _():
        wb_ref[...] = w_ref[...].astype(BF16)

    w = wb_ref[...]
    for r0 in range(0, a_ref.shape[0], MM_ROW_CHUNK):
        rows = slice(r0, r0 + MM_ROW_CHUNK)
        o_ref[rows, :] = jnp.dot(a_ref[rows, :], w, preferred_element_type=F32).astype(o_ref.dtype)


def _matmul(a, w, layer, *, n, tm, tn, out_dtype, name):
    m, k = a.shape
    return pl.pallas_call(
        _mm_kernel,
        grid=(n // tn, m // tm),
        in_specs=[pl.BlockSpec((tm, k), lambda j, i: (i, 0)),
                  pl.BlockSpec((None, k, tn), lambda j, i: (layer, 0, j))],
        out_specs=pl.BlockSpec((tm, tn), lambda j, i: (i, j)),
        out_shape=jax.ShapeDtypeStruct((m, n), out_dtype),
        scratch_shapes=[pltpu.VMEM((k, tn), BF16)],
        compiler_params=_params("arbitrary", "arbitrary"),
        name=name,
    )(a, w)


def _ffn_up_kernel(x_ref, wg_ref, wu_ref, o_ref, wgb_ref, wub_ref, *, n_real):
    j = pl.program_id(0)

    @pl.when(j < n_real)
    def _():
        @pl.when(pl.program_id(1) == 0)
        def _():
            wgb_ref[...] = wg_ref[...].astype(BF16)
            wub_ref[...] = wu_ref[...].astype(BF16)

        wg, wu = wgb_ref[...], wub_ref[...]
        for r0 in range(0, x_ref.shape[0], MM_ROW_CHUNK):
            rows = slice(r0, r0 + MM_ROW_CHUNK)
            x = x_ref[rows, :]
            g = jnp.dot(x, wg, preferred_element_type=F32)
            u = jnp.dot(x, wu, preferred_element_type=F32)
            o_ref[rows, :] = (g * jax.nn.sigmoid(g) * u).astype(o_ref.dtype)

    @pl.when(j >= n_real)
    def _():
        o_ref[...] = jnp.zeros_like(o_ref)


def _ffn_up(x_bf, wg, wu, layer, *, tm, tn, n_out):
    m, k = x_bf.shape
    n_real = wg.shape[2] // tn
    w_spec = pl.BlockSpec((None, k, tn), lambda j, i: (layer, 0, jnp.minimum(j, n_real - 1)))
    return pl.pallas_call(
        functools.partial(_ffn_up_kernel, n_real=n_real),
        grid=(n_out // tn, m // tm),
        in_specs=[pl.BlockSpec((tm, k), lambda j, i: (i, 0)), w_spec, w_spec],
        out_specs=pl.BlockSpec((tm, tn), lambda j, i: (i, j)),
        out_shape=jax.ShapeDtypeStruct((m, n_out), BF16),
        scratch_shapes=[pltpu.VMEM((k, tn), BF16), pltpu.VMEM((k, tn), BF16)],
        compiler_params=_params("arbitrary", "arbitrary"),
        name="ffn_up",
    )(x_bf, wg, wu)


def _mm_ln_kernel(*refs, nk_parts):
    n_a = len(nk_parts)
    a_refs = refs[:n_a]
    w_ref, r_ref, g_ref, b_ref, of_ref = refs[n_a:n_a + 5]
    n_out = len(refs) - n_a - 6
    maybe_ob_ref = refs[n_a + 5:n_a + 5 + n_out - 1]
    mu_ref, rs_ref = refs[-2:]
    k = pl.program_id(1)

    @pl.when(k == 0)
    def _():
        of_ref[...] = DEEPNORM_ALPHA * r_ref[...]

    def accumulate(a_ref):
        a = a_ref[...]
        for c0 in range(0, of_ref.shape[1], LN_COL_BLOCK):
            cols = slice(c0, c0 + LN_COL_BLOCK)
            of_ref[:, cols] += jnp.dot(a, w_ref[:, cols], preferred_element_type=F32)

    if n_a == 1:
        accumulate(a_refs[0])
    else:
        k0 = 0
        for a_ref, nkp in zip(a_refs, nk_parts):
            pl.when(jnp.logical_and(k >= k0, k < k0 + nkp))(functools.partial(accumulate, a_ref))
            k0 += nkp

    @pl.when(k == sum(nk_parts) - 1)
    def _():
        inv_n = 1.0 / of_ref.shape[1]
        nblk = of_ref.shape[0] // LN_ROW_BLOCK
        lanes = mu_ref.shape[1]
        nrep = of_ref.shape[1] // lanes

        def block(r):
            return pl.ds(pl.multiple_of(r * LN_ROW_BLOCK, LN_ROW_BLOCK), LN_ROW_BLOCK)

        def mean_rows(r, carry):
            rows = block(r)
            mu = jnp.sum(of_ref[rows, :], axis=-1, keepdims=True) * inv_n
            mu_ref[rows, :] = jnp.broadcast_to(mu, (LN_ROW_BLOCK, lanes))
            return carry

        def rstd_rows(r, carry):
            rows = block(r)
            zc = of_ref[rows, :] - jnp.tile(mu_ref[rows, :], (1, nrep))
            var = jnp.sum(zc * zc, axis=-1, keepdims=True) * inv_n
            rs_ref[rows, :] = jnp.broadcast_to(lax.rsqrt(var + NORM_EPS), (LN_ROW_BLOCK, lanes))
            return carry

        def norm_rows(r, carry):
            rows = block(r)
            mu = jnp.tile(mu_ref[rows, :], (1, LN_COL_BLOCK // lanes))
            rs = jnp.tile(rs_ref[rows, :], (1, LN_COL_BLOCK // lanes))
            for c0 in range(0, of_ref.shape[1], LN_COL_BLOCK):
                cols = slice(c0, c0 + LN_COL_BLOCK)
                y = (of_ref[rows, cols] - mu) * rs * g_ref[:, cols] + b_ref[:, cols]
                of_ref[rows, cols] = y
                for ob_ref in maybe_ob_ref:
                    ob_ref[rows, cols] = y.astype(ob_ref.dtype)
            return carry

        lax.fori_loop(0, nblk, mean_rows, 0, unroll=4)
        lax.fori_loop(0, nblk, rstd_rows, 0, unroll=4)
        lax.fori_loop(0, nblk, norm_rows, 0, unroll=2)


def _matmul_residual_ln(a_parts, w, resid, gain, bias, *, tm, tk, emit_bf16, name):
    m = resid.shape[0]
    n = w.shape[1]
    nk_parts = tuple(a.shape[1] // tk for a in a_parts)
    row_spec = pl.BlockSpec((tm, n), lambda i, k: (i, 0))
    vec_spec = pl.BlockSpec((1, n), lambda i, k: (0, 0))
    a_specs, k0 = [], 0
    for nkp in nk_parts:
        a_specs.append(pl.BlockSpec((tm, tk), lambda i, k, k0=k0, nkp=nkp: (i, jnp.clip(k - k0, 0, nkp - 1))))
        k0 += nkp
    out_shape = [jax.ShapeDtypeStruct((m, n), F32)]
    out_specs = [row_spec]
    if emit_bf16:
        out_shape.append(jax.ShapeDtypeStruct((m, n), BF16))
        out_specs.append(row_spec)
    return pl.pallas_call(
        functools.partial(_mm_ln_kernel, nk_parts=nk_parts),
        grid=(m // tm, sum(nk_parts)),
        in_specs=a_specs + [pl.BlockSpec((tk, n), lambda i, k: (k, 0)),
                            row_spec,
                            vec_spec, vec_spec],
        out_specs=out_specs,
        out_shape=out_shape,
        scratch_shapes=[pltpu.VMEM((tm, V7X_LANES), F32), pltpu.VMEM((tm, V7X_LANES), F32)],
        compiler_params=_params("parallel", "arbitrary"),
        name=name,
    )(*a_parts, w, resid, gain.reshape(1, n), bias.reshape(1, n))


def _chunk_steps(qs, ks, vs, las, st_ref, tril_bf, causal):
    c = qs[0].shape[0]
    nt = (((1,), (1,)), ((), ()))
    tn = (((0,), (0,)), ((), ()))
    heads = range(len(qs))
    bs = []
    for la in las:
        la_hi = la.astype(BF16)
        la_lo = (la - la_hi.astype(F32)).astype(BF16)
        bs.append(jnp.dot(tril_bf, la_hi, preferred_element_type=F32)
                  + jnp.dot(tril_bf, la_lo, preferred_element_type=F32))
    q_decs, scores, kvs, inters, decays = [], [], [], [], []
    for hd in heads:
        b = bs[hd]
        b_end = b[c - 1:c, :]
        q_dec = (qs[hd] * jnp.exp(b)).astype(BF16)
        k_inv = (ks[hd] * jnp.exp(-b)).astype(BF16)
        k_end = (ks[hd] * jnp.exp(b_end - b)).astype(BF16)
        v_bf = vs[hd].astype(BF16)
        vs[hd] = v_bf
        scores.append(lax.dot_general(q_dec, k_inv, nt, preferred_element_type=F32))
        kvs.append(lax.dot_general(v_bf, k_end, tn, preferred_element_type=F32))
        inters.append(lax.dot_general(q_dec, st_ref[hd].astype(BF16), nt, preferred_element_type=F32))
        decays.append(jnp.exp(b_end))
    outs = []
    for hd in heads:
        sc = jnp.where(causal, scores[hd], 0.0).astype(BF16)
        outs.append(jnp.dot(sc, vs[hd], preferred_element_type=F32) + inters[hd])
        st_ref[hd] = decays[hd] * st_ref[hd] + kvs[hd]
    return outs


def _gated_rmsnorm(o, gain, gate):
    ms = jnp.mean(o * o, axis=-1, keepdims=True)
    return o * lax.rsqrt(ms + NORM_EPS) * gain * (gate * jax.nn.sigmoid(gate))


def _chunk_masks():
    row = lax.broadcasted_iota(jnp.int32, (CHUNK, CHUNK), 0)
    col = lax.broadcasted_iota(jnp.int32, (CHUNK, CHUNK), 1)
    causal = row >= col
    return jnp.where(causal, 1.0, 0.0).astype(BF16), causal


def _log_sigmoid(z):
    return jnp.minimum(z, 0.0) - jnp.log1p(jnp.exp(-jnp.abs(z)))


def _gla_kernel(q_ref, k_ref, v_ref, g_ref, alr_ref, wa_ref, ba_ref, gain_ref, o_ref, st_ref,
                *, hb, nchunks, unroll):
    @pl.when(pl.program_id(2) == 0)
    def _():
        st_ref[...] = jnp.zeros_like(st_ref)

    tril_bf, causal = _chunk_masks()
    gain = gain_ref[...]
    ba = ba_ref[...]
    wa = wa_ref[...]

    def body(c, carry):
        rows = pl.ds(pl.multiple_of(c * CHUNK, CHUNK), CHUNK)
        z = jnp.dot(alr_ref[rows, :].astype(BF16), wa, preferred_element_type=F32) + ba
        la_all = _log_sigmoid(z) / GLA_TAU
        kcs = [slice(hd * GLA_DK, (hd + 1) * GLA_DK) for hd in range(hb)]
        vcs = [slice(hd * GLA_DV, (hd + 1) * GLA_DV) for hd in range(hb)]
        outs = _chunk_steps([q_ref[rows, kc] * (GLA_DK ** -0.5) for kc in kcs],
                            [k_ref[rows, kc] for kc in kcs], [v_ref[rows, vc] for vc in vcs],
                            [la_all[:, kc] for kc in kcs], st_ref, tril_bf, causal)
        for o, vc in zip(outs, vcs):
            o_ref[rows, vc] = _gated_rmsnorm(o, gain, g_ref[rows, vc]).astype(o_ref.dtype)
        return carry

    lax.fori_loop(0, nchunks, body, 0, unroll=unroll)


def _gla(h, alr, w_alpha, b_alpha, norm_g, *, bsz, seq, col_q, col_k, col_v, col_g, tt, hb, unroll):
    n = bsz * seq
    nt = seq // tt
    kw, vw = hb * GLA_DK, hb * GLA_DV
    hg = GLA_HEADS // hb

    def rows(b, g, t):
        return b * nt + t

    return pl.pallas_call(
        functools.partial(_gla_kernel, hb=hb, nchunks=tt // CHUNK, unroll=unroll),
        grid=(bsz, hg, nt),
        in_specs=[pl.BlockSpec((tt, kw), lambda b, g, t: (rows(b, g, t), col_q // kw + g)),
                  pl.BlockSpec((tt, kw), lambda b, g, t: (rows(b, g, t), col_k // kw + g)),
                  pl.BlockSpec((tt, vw), lambda b, g, t: (rows(b, g, t), col_v // vw + g)),
                  pl.BlockSpec((tt, vw), lambda b, g, t: (rows(b, g, t), col_g // vw + g)),
                  pl.BlockSpec((tt, V7X_LANES), lambda b, g, t: (rows(b, g, t), 0)),
                  pl.BlockSpec((V7X_LANES, kw), lambda b, g, t: (0, g)),
                  pl.BlockSpec((1, kw), lambda b, g, t: (0, g)),
                  pl.BlockSpec((1, GLA_DV), lambda b, g, t: (0, 0))],
        out_specs=pl.BlockSpec((tt, vw), lambda b, g, t: (rows(b, g, t), g)),
        out_shape=jax.ShapeDtypeStruct((n, GLA_HEADS * GLA_DV), BF16),
        scratch_shapes=[pltpu.VMEM((hb, GLA_DV, GLA_DK), F32)],
        compiler_params=_params("parallel", "parallel", "arbitrary"),
        name="gla",
    )(h, h, h, h, alr, w_alpha, b_alpha, norm_g)


def _hgrn_kernel(q_ref, f_ref, i_ref, g_ref, tab_ref, gain_ref, o_ref, st_ref, *, layer, hb, nchunks,
                 unroll):
    @pl.when(pl.program_id(2) == 0)
    def _():
        st_ref[...] = jnp.zeros_like(st_ref)

    tab = tab_ref[...]
    e = jnp.exp(tab - jnp.max(tab, axis=0, keepdims=True))
    soft = e / jnp.sum(e, axis=0, keepdims=True)
    lb_all = jnp.sum(soft[0:layer + 1], axis=0, keepdims=True) - soft[0:1]
    tril_bf, causal = _chunk_masks()
    gain = gain_ref[...]

    def body(c, carry):
        rows = pl.ds(pl.multiple_of(c * CHUNK, CHUNK), CHUNK)
        kcs = [slice(hd * HG_DK, (hd + 1) * HG_DK) for hd in range(hb)]
        vcs = [slice(hd * HG_DV, (hd + 1) * HG_DV) for hd in range(hb)]
        fs = [lb_all[:, kc] + (1.0 - lb_all[:, kc]) * jax.nn.sigmoid(f_ref[rows, kc]) for kc in kcs]
        ivs = [i_ref[rows, vc] for vc in vcs]
        outs = _chunk_steps([q_ref[rows, kc] for kc in kcs], [1.0 - f for f in fs],
                            [iv * jax.nn.sigmoid(iv) for iv in ivs], [jnp.log(f) for f in fs],
                            st_ref, tril_bf, causal)
        for o, vc in zip(outs, vcs):
            o_ref[rows, vc] = _gated_rmsnorm(o, gain, g_ref[rows, vc]).astype(o_ref.dtype)
        return carry

    lax.fori_loop(0, nchunks, body, 0, unroll=unroll)


def _hgrn(h, lb_table, norm_g, *, layer, bsz, seq, heads, tt, hb, unroll):
    n = bsz * seq
    nt = seq // tt
    w = hb * HG_DK
    hg = heads // hb

    def spec(seg):
        return pl.BlockSpec((tt, w), lambda b, g, t: (b * nt + t, seg * hg + g))

    return pl.pallas_call(
        functools.partial(_hgrn_kernel, layer=layer, hb=hb, nchunks=tt // CHUNK, unroll=unroll),
        grid=(bsz, hg, nt),
        in_specs=[spec(0), spec(1), spec(2), spec(3),
                  pl.BlockSpec((DEPTH, w), lambda b, g, t: (0, g)),
                  pl.BlockSpec((1, HG_DV), lambda b, g, t: (0, 0))],
        out_specs=pl.BlockSpec((tt, w), lambda b, g, t: (b * nt + t, g)),
        out_shape=jax.ShapeDtypeStruct((n, heads * HG_DV), BF16),
        scratch_shapes=[pltpu.VMEM((hb, HG_DV, HG_DK), F32)],
        compiler_params=_params("parallel", "parallel", "arbitrary"),
        name="hgrn2",
    )(h, h, h, h, lb_table, norm_g)


def _s5_prep_kernel(lr_row_ref, li_row_ref, lr_col_ref, li_col_ref, ldt_ref,
                    btr_ref, bti_ref, ctr_ref, cti_ref, w1_ref, f_ref, coef_ref):
    L, J, P = S5_CHUNK, S5_GROUP, S5_STATE
    hi = lax.Precision.HIGHEST
    dt = jnp.exp(ldt_ref[0])
    lr, li = lr_row_ref[0], li_row_ref[0]
    npow = ((L - 1) - lax.broadcasted_iota(jnp.int32, (L, 2 * P), 0)).astype(F32)
    mag = jnp.exp(npow * (lr * dt))
    th = npow * (li * dt)
    ar, ai = mag * jnp.cos(th), mag * jnp.sin(th)
    half = lax.broadcasted_iota(jnp.int32, (L, 2 * P), 1) < P
    a1r, a1i = ar[L - 2:L - 1], ai[L - 2:L - 1]
    nr, ni = a1r - 1.0, a1i
    den = lr * lr + li * li
    fr, fi = (nr * lr + ni * li) / den, (ni * lr - nr * li) / den
    btr, bti = btr_ref[0], bti_ref[0]
    bbr, bbi = fr * btr - fi * bti, fr * bti + fi * btr
    pa, pb = jnp.where(half, ar, ai), jnp.where(half, -ai, ar)
    for s in range(L):
        rows = slice(s * J, (s + 1) * J)
        w1_ref[0, rows, L * J:] = (bbr * pa[s:s + 1] + bbi * pb[s:s + 1]).astype(BF16)
    alr_, ali_ = ar[0:1] * a1r - ai[0:1] * a1i, ar[0:1] * a1i + ai[0:1] * a1r
    half1 = half[0:1]
    coef_ref[0, 0:1, :] = alr_
    coef_ref[0, 1:2, :] = jnp.where(half1, -ali_, ali_)
    coef_ref[0, 2:, :] = jnp.zeros((V7X_SUBLANES - 2, 2 * P), F32)

    lrc, lic = lr_col_ref[0], li_col_ref[0]
    lane = lax.broadcasted_iota(jnp.int32, (P, L * J), 1)
    lag = jnp.right_shift(lane, int(math.log2(J))).astype(F32)
    magq = jnp.exp(lag * (lrc * dt))
    thq = lag * (lic * dt)
    aqr, aqi = magq * jnp.cos(thq), magq * jnp.sin(thq)
    cr, ci = ctr_ref[0], cti_ref[0]
    qr, qi = cr * aqr - ci * aqi, cr * aqi + ci * aqr
    r = (jnp.dot(bbr[:, :P], qr, precision=hi, preferred_element_type=F32)
         - jnp.dot(bbi[:, :P], qi, precision=hi, preferred_element_type=F32))
    lane_r = lax.broadcasted_iota(jnp.int32, (J, L * J), 1)
    for s in range(L):
        blk = r if s == 0 else jnp.where(lane_r >= s * J, pltpu.roll(r, s * J, 1), 0.0)
        w1_ref[0, s * J:(s + 1) * J, 0:L * J] = blk.astype(BF16)
    m1 = jnp.exp(lrc * dt)
    c1r, c1i = m1 * jnp.cos(lic * dt), m1 * jnp.sin(lic * dt)
    f_ref[0, 0:P, :] = (qr * c1r - qi * c1i).astype(BF16)
    f_ref[0, P:, :] = (-(qr * c1i + qi * c1r)).astype(BF16)


def _s5_prep(lam_re, lam_im, log_dt, b_re, b_im, c_re, c_im):
    g = lam_re.shape[0]
    P, J, L = S5_STATE, S5_GROUP, S5_CHUNK
    dbl = lambda a: jnp.concatenate([a, a], axis=-1)
    lr_row = dbl(lam_re).reshape(g, 1, 2 * P)
    li_row = dbl(lam_im).reshape(g, 1, 2 * P)
    lr_col = lam_re.reshape(g, P, 1)
    li_col = lam_im.reshape(g, P, 1)
    ldt = log_dt.reshape(g, 1, 1)
    btr = dbl(jnp.swapaxes(b_re, 1, 2))
    bti = dbl(jnp.swapaxes(b_im, 1, 2))
    ctr = jnp.tile(jnp.swapaxes(c_re, 1, 2), (1, 1, L))
    cti = jnp.tile(jnp.swapaxes(c_im, 1, 2), (1, 1, L))
    blk = lambda *s: pl.BlockSpec((1,) + s, lambda i: (i, 0, 0))
    return pl.pallas_call(
        _s5_prep_kernel,
        grid=(g,),
        in_specs=[blk(1, 2 * P), blk(1, 2 * P), blk(P, 1), blk(P, 1), blk(1, 1),
                  blk(J, 2 * P), blk(J, 2 * P), blk(P, L * J), blk(P, L * J)],
        out_specs=[blk(L * J, L * J + 2 * P), blk(2 * P, L * J), blk(V7X_SUBLANES, 2 * P)],
        out_shape=[jax.ShapeDtypeStruct((g, L * J, L * J + 2 * P), BF16),
                   jax.ShapeDtypeStruct((g, 2 * P, L * J), BF16),
                   jax.ShapeDtypeStruct((g, V7X_SUBLANES, 2 * P), F32)],
        compiler_params=_params("parallel"),
        name="s5_prep",
    )(lr_row, li_row, lr_col, li_col, ldt, btr, bti, ctr, cti)


def _s5_assemble_kernel(w1_ref, f_ref, coef_ref, wbig_ref, fbig_ref, cbig_ref):
    L, J, P, Q = S5_CHUNK, S5_GROUP, S5_STATE, S5_LANE_GROUPS
    lanes = Q * J
    width = L * lanes
    nstate = 2 * Q * P
    jbits, pbits = int(math.log2(J)), int(math.log2(P))

    def iota(shape, axis):
        return lax.broadcasted_iota(jnp.int32, shape, axis)

    a, b = iota((L * J, width), 0), iota((L * J, width), 1)
    sel_sc = jnp.where(((b >> int(math.log2(lanes))) == (a >> jbits)) & ((b & (J - 1)) == (a & (J - 1))),
                       1.0, 0.0).astype(BF16)
    a, b = iota((2 * P, nstate), 0), iota((2 * P, nstate), 1)
    sel_st = jnp.where(((b >> int(math.log2(Q * P))) == (a >> pbits)) & ((b & (P - 1)) == (a & (P - 1))),
                       1.0, 0.0).astype(BF16)
    col_group_sc = (iota((lanes, width), 1) >> jbits) & (Q - 1)
    col_group_st = (iota((lanes, nstate), 1) >> pbits) & (Q - 1)
    row_group_j = iota((lanes, 1), 0) >> jbits

    r_stack = jnp.concatenate([w1_ref[g, 0:J, 0:L * J] for g in range(Q)], axis=0)
    r_big = jnp.dot(r_stack, sel_sc, preferred_element_type=F32)
    r_big = jnp.where(col_group_sc == row_group_j, r_big, 0.0).astype(BF16)
    for s in range(L):
        rows = slice(s * lanes, (s + 1) * lanes)
        if s:
            wbig_ref[0, rows, 0:s * lanes] = jnp.zeros((lanes, s * lanes), BF16)
        wbig_ref[0, rows, s * lanes:width] = r_big[:, 0:width - s * lanes]
        e_stack = jnp.concatenate([w1_ref[g, s * J:(s + 1) * J, L * J:] for g in range(Q)], axis=0)
        e_big = jnp.dot(e_stack, sel_st, preferred_element_type=F32)
        wbig_ref[0, rows, width:] = jnp.where(col_group_st == row_group_j, e_big, 0.0).astype(BF16)

    row_in_pair = iota((lanes, 1), 0) >> pbits
    for m in range(nstate // lanes):
        ri, g0 = divmod(m * (lanes // P), Q)
        f_stack = jnp.concatenate([f_ref[g0 + t, ri * P:(ri + 1) * P, :] for t in range(lanes // P)], axis=0)
        f_big = jnp.dot(f_stack, sel_sc, preferred_element_type=F32)
        fbig_ref[0, m * lanes:(m + 1) * lanes, :] = jnp.where(
            col_group_sc == g0 + row_in_pair, f_big, 0.0).astype(BF16)

    low = iota((V7X_SUBLANES, lanes), 1) < P
    for k in range(Q // 2):
        c0, c1 = coef_ref[2 * k], coef_ref[2 * k + 1]
        row_is_a2 = iota((V7X_SUBLANES, lanes), 0) == 1
        re_half = jnp.where(low, c0, jnp.where(row_is_a2, -c1, c1))
        im_half = jnp.where(low, jnp.where(row_is_a2, -c0, c0), c1)
        cbig_ref[0, :, k * lanes:(k + 1) * lanes] = re_half
        cbig_ref[0, :, Q * P + k * lanes:Q * P + (k + 1) * lanes] = im_half


def _s5_assemble(w1, fmat, coef):
    g = w1.shape[0]
    L, J, P, Q = S5_CHUNK, S5_GROUP, S5_STATE, S5_LANE_GROUPS
    nb = g // Q
    width, nstate = L * Q * J, 2 * Q * P
    grp = lambda *s: pl.BlockSpec((Q,) + s, lambda i: (i, 0, 0))
    out = lambda *s: pl.BlockSpec((1,) + s, lambda i: (i, 0, 0))
    return pl.pallas_call(
        _s5_assemble_kernel,
        grid=(nb,),
        in_specs=[grp(L * J, L * J + 2 * P), grp(2 * P, L * J), grp(V7X_SUBLANES, 2 * P)],
        out_specs=[out(width, width + nstate), out(nstate, width), out(V7X_SUBLANES, nstate)],
        out_shape=[jax.ShapeDtypeStruct((nb, width, width + nstate), BF16),
                   jax.ShapeDtypeStruct((nb, nstate, width), BF16),
                   jax.ShapeDtypeStruct((nb, V7X_SUBLANES, nstate), F32)],
        compiler_params=_params("parallel"),
        name="s5_assemble",
    )(w1, fmat, coef)


def _s5_main_kernel(u_ref, w_ref, f_ref, coef_ref, o_ref, ys_ref, sp_ref, st_ref, *, nchunk, nb):
    width = u_ref.shape[2]
    half = st_ref.shape[1] // 2

    @pl.when(pl.program_id(1) == 0)
    def _():
        st_ref[...] = jnp.zeros_like(st_ref)

    ys_ref[...] = jnp.dot(u_ref[0], w_ref[0], preferred_element_type=F32)
    a1, a2 = coef_ref[0, 0:1, :], coef_ref[0, 1:2, :]

    def body(c, s):
        rows = pl.ds(pl.multiple_of(c * nb, nb), nb)
        sp_ref[rows, :] = s
        swapped = jnp.concatenate([s[:, half:], s[:, :half]], axis=1)
        return a1 * s + a2 * swapped + ys_ref[rows, width:]

    st_ref[...] = lax.fori_loop(0, nchunk, body, st_ref[...], unroll=4)
    y = ys_ref[:, 0:width] + jnp.dot(sp_ref[...].astype(BF16), f_ref[0], preferred_element_type=F32)
    o_ref[0] = y.astype(o_ref.dtype)


def _s5_main(u_c, wbig, fbig, cbig, *, nb, rows_per_step):
    nblk, rows, width = u_c.shape
    nstate = fbig.shape[1]
    nsteps = rows // rows_per_step
    fixed = lambda *s: pl.BlockSpec((1,) + s, lambda i, r: (i, 0, 0))
    moving = pl.BlockSpec((1, rows_per_step, width), lambda i, r: (i, r, 0))
    return pl.pallas_call(
        functools.partial(_s5_main_kernel, nchunk=rows_per_step // nb, nb=nb),
        grid=(nblk, nsteps),
        in_specs=[moving, fixed(width, width + nstate), fixed(nstate, width),
                  fixed(V7X_SUBLANES, nstate)],
        out_specs=moving,
        out_shape=jax.ShapeDtypeStruct((nblk, rows, width), BF16),
        scratch_shapes=[pltpu.VMEM((rows_per_step, width + nstate), F32),
                        pltpu.VMEM((rows_per_step, nstate), F32),
                        pltpu.VMEM((nb, nstate), F32)],
        compiler_params=_params("parallel", "arbitrary"),
        name="s5_main",
    )(u_c, wbig, fbig, cbig)


def _s5_glu_kernel(ys_ref, u_ref, d_ref, w_ref, o_ref):
    y = jax.nn.gelu(ys_ref[...].astype(F32) + d_ref[...] * u_ref[...], approximate=True)
    z = jnp.dot(y.astype(BF16), w_ref[...], preferred_element_type=F32)
    o_ref[...] = (y * jax.nn.sigmoid(z)).astype(o_ref.dtype)


def _s5_glu(y_ssm, h, d_skip, w_glu, *, tm):
    m, n = y_ssm.shape
    return pl.pallas_call(
        _s5_glu_kernel,
        grid=(m // tm,),
        in_specs=[pl.BlockSpec((tm, n), lambda i: (i, 0)),
                  pl.BlockSpec((tm, n), lambda i: (i, 0)),
                  pl.BlockSpec((1, n), lambda i: (0, 0)),
                  pl.BlockSpec((n, n), lambda i: (0, 0))],
        out_specs=pl.BlockSpec((tm, n), lambda i: (i, 0)),
        out_shape=jax.ShapeDtypeStruct((m, n), BF16),
        compiler_params=_params("parallel"),
        name="s5_glu",
    )(y_ssm, h, d_skip.reshape(1, n), w_glu)


def _pad_cols(w, n):
    return jnp.pad(w, ((0, 0), (0, n - w.shape[1])))


def _ffn_block(x_f32, x_bf, w_gate, w_up, w_down, layer, ln_g, ln_b, *, emit_bf16):
    hidden = w_gate.shape[2]
    hpad = -(-hidden // FFN_DOWN_TK) * FFN_DOWN_TK
    wd = jnp.pad(w_down[layer].astype(BF16), ((0, hpad - hidden), (0, 0)))
    a = _ffn_up(x_bf, w_gate, w_up, layer, tm=1024, tn=256, n_out=hpad)
    return _matmul_residual_ln([a], wd, x_f32, ln_g, ln_b, tm=512, tk=FFN_DOWN_TK, emit_bf16=emit_bf16,
                               name="ffn_down_ln")


def _even_layer(x_f32, x_bf, bsz, seq, e, w_in, w_out, lam_re, lam_im, log_dt, b_re, b_im, c_re, c_im,
                d_skip, w_glu, w_alpha, b_alpha, norm_g, ln_g, ln_b):
    n = bsz * seq
    s5w = d_skip.shape[0]
    groups = s5w // S5_GROUP
    kw = GLA_HEADS * GLA_DK
    vw = GLA_HEADS * GLA_DV
    main = s5w + 2 * kw + 2 * vw
    h = _matmul(x_bf, w_in, e, n=main, tm=1024, tn=512, out_dtype=F32, name="even_in")
    alr = _matmul(x_bf, _pad_cols(w_in[e, :, main:], V7X_LANES)[None], 0, n=V7X_LANES, tm=1024,
                  tn=V7X_LANES, out_dtype=F32, name="even_in_gate")

    w1, fmat, coef = _s5_prep(lam_re, lam_im, log_dt, b_re, b_im, c_re, c_im)
    wbig, fbig, cbig = _s5_assemble(w1, fmat, coef)
    nchunk = seq // S5_CHUNK
    nblk = s5w // V7X_LANES
    u_c = (h[:, :s5w].astype(BF16).reshape(bsz, nchunk, S5_CHUNK, nblk, V7X_LANES)
           .transpose(3, 1, 0, 2, 4).reshape(nblk, nchunk * bsz, S5_CHUNK * V7X_LANES))
    y_c = _s5_main(u_c, wbig, fbig, cbig, nb=bsz, rows_per_step=256)
    y_ssm = (y_c.reshape(nblk, nchunk, bsz, S5_CHUNK, V7X_LANES)
             .transpose(2, 1, 3, 0, 4).reshape(n, s5w))
    y_s5 = _s5_glu(y_ssm, h, d_skip, w_glu.astype(BF16), tm=512)

    wa = jnp.pad(w_alpha, ((0, V7X_LANES - w_alpha.shape[0]), (0, 0))).astype(BF16)
    y_gla = _gla(h, alr, wa, b_alpha.reshape(1, kw), norm_g.reshape(1, GLA_DV), bsz=bsz, seq=seq,
                 col_q=s5w, col_k=s5w + kw, col_v=s5w + 2 * kw, col_g=s5w + 2 * kw + vw,
                 tt=256, hb=4, unroll=2)

    return _matmul_residual_ln([y_s5, y_gla], w_out.astype(BF16), x_f32, ln_g, ln_b, tm=512, tk=512,
                               emit_bf16=True, name="even_out_ln")


def _odd_layer(x_f32, x_bf, bsz, seq, layer, o, w_in, w_out, lb_table, norm_g, ln_g, ln_b):
    heads = w_out.shape[0] // HG_DV
    h = _matmul(x_bf, w_in, o, n=w_in.shape[2], tm=1024, tn=512, out_dtype=F32, name="odd_in")
    y = _hgrn(h, lb_table, norm_g.reshape(1, HG_DV), layer=layer, bsz=bsz, seq=seq, heads=heads,
              tt=256, hb=8, unroll=2)
    return _matmul_residual_ln([y], w_out.astype(BF16), x_f32, ln_g, ln_b, tm=512, tk=512,
                               emit_bf16=True, name="odd_out_ln")


def kernel(x, ev_w_in, ev_w_out, s5_lam_re, s5_lam_im, s5_log_dt, s5_b_re, s5_b_im, s5_c_re, s5_c_im,
           s5_d, s5_w_glu, gla_w_alpha, gla_b_alpha, gla_norm_g, od_w_in, od_w_out, hg_lb_table,
           hg_norm_g, ln_mix_g, ln_mix_b, ln_ffn_g, ln_ffn_b, ffn_w_gate, ffn_w_up, ffn_w_down):
    bsz, seq, d = x.shape
    depth = ln_mix_g.shape[0]
    assert depth == DEPTH
    xf = x.reshape(bsz * seq, d)
    xb = xf.astype(BF16)
    for layer in range(depth):
        if layer % 2 == 0:
            e = layer // 2
            xf, xb = _even_layer(xf, xb, bsz, seq, e, ev_w_in, ev_w_out[e], s5_lam_re[e], s5_lam_im[e],
                                 s5_log_dt[e], s5_b_re[e], s5_b_im[e], s5_c_re[e], s5_c_im[e], s5_d[e],
                                 s5_w_glu[e], gla_w_alpha[e], gla_b_alpha[e], gla_norm_g[e],
                                 ln_mix_g[layer], ln_mix_b[layer])
        else:
            o = layer // 2
            xf, xb = _odd_layer(xf, xb, bsz, seq, layer, o, od_w_in, od_w_out[o], hg_lb_table,
                                hg_norm_g[o], ln_mix_g[layer], ln_mix_b[layer])
        last = layer == depth - 1
        res = _ffn_block(xf, xb, ffn_w_gate, ffn_w_up, ffn_w_down, layer,
                         ln_ffn_g[layer], ln_ffn_b[layer], emit_bf16=not last)
        xf, xb = (res[0], None) if last else res
    return xf.reshape(bsz, seq, d).astype(x.dtype)
```

```python
import functools
import math

import jax
import jax.numpy as jnp
from jax import lax
from jax.experimental import pallas as pl
from jax.experimental.pallas import tpu as pltpu

F32 = jnp.float32
BF16 = jnp.bfloat16

DEPTH = 2
S5_GROUP = 16
S5_STATE = 64
GLA_HEADS = 8
GLA_DK = 128
GLA_DV = 256
GLA_GATE_RANK = 16
GLA_TAU = 16.0
HG_DK = 128
HG_DV = 128
CHUNK = 64
DEEPNORM_ALPHA = (2.0 * DEPTH) ** 0.25
NORM_EPS = 1e-5

V7X_LANES = 128
V7X_SUBLANES = 8
V7X_VMEM_LIMIT_BYTES = 56 * 1024 * 1024

MM_ROW_CHUNK = 256
FFN_DOWN_TK = 2816
LN_TM = 256
LN_COL_BLOCK = 1024
LN_ROW_BLOCK = 16

S5_CHUNK = 16
S5_ROW = S5_CHUNK * S5_GROUP
S5_LANE_GROUPS = V7X_LANES // S5_GROUP


def _params(*sem):
    return pltpu.CompilerParams(dimension_semantics=sem, vmem_limit_bytes=V7X_VMEM_LIMIT_BYTES)


def _mm_kernel(a_ref, w_ref, o_ref, wb_ref):
    @pl.when(pl.program_id(1) == 0)
    def _():
        wb_ref[...] = w_ref[...].astype(BF16)

    w = wb_ref[...]
    for r0 in range(0, a_ref.shape[0], MM_ROW_CHUNK):
        rows = slice(r0, r0 + MM_ROW_CHUNK)
        o_ref[rows, :] = jnp.dot(a_ref[rows, :], w, preferred_element_type=F32).astype(o_ref.dtype)


def _matmul(a, w, layer, *, n, tm, tn, out_dtype, name):
    m, k = a.shape
    return pl.pallas_call(
        _mm_kernel,
        grid=(n // tn, m // tm),
        in_specs=[pl.BlockSpec((tm, k), lambda j, i: (i, 0)),
                  pl.BlockSpec((None, k, tn), lambda j, i: (layer, 0, j))],
        out_specs=pl.BlockSpec((tm, tn), lambda j, i: (i, j)),
        out_shape=jax.ShapeDtypeStruct((m, n), out_dtype),
        scratch_shapes=[pltpu.VMEM((k, tn), BF16)],
        compiler_params=_params("arbitrary", "arbitrary"),
        name=name,
    )(a, w)


def _ffn_up_kernel(x_ref, wg_ref, wu_ref, o_ref, wgb_ref, wub_ref, *, n_real):
    j = pl.program_id(0)

    @pl.when(j < n_real)
    def _():
        @pl.when(pl.program_id(1) == 0)
        def _():
            wgb_ref[...] = wg_ref[...].astype(BF16)
            wub_ref[...] = wu_ref[...].astype(BF16)

        wg, wu = wgb_ref[...], wub_ref[...]
        for r0 in range(0, x_ref.shape[0], MM_ROW_CHUNK):
            rows = slice(r0, r0 + MM_ROW_CHUNK)
            x = x_ref[rows, :]
            g = jnp.dot(x, wg, preferred_element_type=F32)
            u = jnp.dot(x, wu, preferred_element_type=F32)
            o_ref[rows, :] = (g * jax.nn.sigmoid(g) * u).astype(o_ref.dtype)

    @pl.when(j >= n_real)
    def _():
        o_ref[...] = jnp.zeros_like(o_ref)


def _ffn_up(x_bf, wg, wu, layer, *, tm, tn, n_out):
    m, k = x_bf.shape
    n_real = wg.shape[2] // tn
    w_spec = pl.BlockSpec((None, k, tn), lambda j, i: (layer, 0, jnp.minimum(j, n_real - 1)))
    return pl.pallas_call(
        functools.partial(_ffn_up_kernel, n_real=n_real),
        grid=(n_out // tn, m // tm),
        in_specs=[pl.BlockSpec((tm, k), lambda j, i: (i, 0)), w_spec, w_spec],
        out_specs=pl.BlockSpec((tm, tn), lambda j, i: (i, j)),
        out_shape=jax.ShapeDtypeStruct((m, n_out), BF16),
        scratch_shapes=[pltpu.VMEM((k, tn), BF16), pltpu.VMEM((k, tn), BF16)],
        compiler_params=_params("arbitrary", "arbitrary"),
        name="ffn_up",
    )(x_bf, wg, wu)


def _mm_resid_kernel(*refs, k_parts, nk):
    n_a = len(k_parts)
    a_refs = refs[:n_a]
    w_ref, r_ref, o_ref = refs[n_a:]
    k = pl.program_id(2)

    def product():
        acc, k0 = None, 0
        for a_ref, kp in zip(a_refs, k_parts):
            d = jnp.dot(a_ref[...], w_ref[k0:k0 + kp, :], preferred_element_type=F32)
            acc = d if acc is None else acc + d
            k0 += kp
        return acc

    if nk == 1:
        o_ref[...] = product() + DEEPNORM_ALPHA * r_ref[...]
    else:
        @pl.when(k == 0)
        def _():
            o_ref[...] = product() + DEEPNORM_ALPHA * r_ref[...]

        @pl.when(k > 0)
        def _():
            o_ref[...] += product()


def _matmul_resid(a_parts, w, resid, *, tm, tn, tk, name):
    m, n = resid.shape
    kdim = w.shape[0]
    if len(a_parts) > 1:
        assert tk == kdim
        k_parts = tuple(a.shape[1] for a in a_parts)
        a_specs = [pl.BlockSpec((tm, kp), lambda i, j, k: (i, 0)) for kp in k_parts]
    else:
        k_parts = (tk,)
        a_specs = [pl.BlockSpec((tm, tk), lambda i, j, k: (i, k))]
    nk = kdim // tk
    tile = pl.BlockSpec((tm, tn), lambda i, j, k: (i, j))
    return pl.pallas_call(
        functools.partial(_mm_resid_kernel, k_parts=k_parts, nk=nk),
        grid=(m // tm, n // tn, nk),
        in_specs=a_specs + [pl.BlockSpec((tk, tn), lambda i, j, k: (k, j)), tile],
        out_specs=tile,
        out_shape=jax.ShapeDtypeStruct((m, n), F32),
        compiler_params=_params("parallel", "parallel", "arbitrary"),
        name=name,
    )(*a_parts, w, resid)


def _ln_kernel(z_ref, g_ref, b_ref, of_ref, *rest):
    maybe_ob_ref, (mu_ref, rs_ref) = rest[:-2], rest[-2:]
    inv_n = 1.0 / z_ref.shape[1]
    nblk = z_ref.shape[0] // LN_ROW_BLOCK
    lanes = mu_ref.shape[1]
    nrep = z_ref.shape[1] // lanes

    def block(r):
        return pl.ds(pl.multiple_of(r * LN_ROW_BLOCK, LN_ROW_BLOCK), LN_ROW_BLOCK)

    def mean_rows(r, carry):
        rows = block(r)
        mu = jnp.sum(z_ref[rows, :], axis=-1, keepdims=True) * inv_n
        mu_ref[rows, :] = jnp.broadcast_to(mu, (LN_ROW_BLOCK, lanes))
        return carry

    def rstd_rows(r, carry):
        rows = block(r)
        zc = z_ref[rows, :] - jnp.tile(mu_ref[rows, :], (1, nrep))
        var = jnp.sum(zc * zc, axis=-1, keepdims=True) * inv_n
        rs_ref[rows, :] = jnp.broadcast_to(lax.rsqrt(var + NORM_EPS), (LN_ROW_BLOCK, lanes))
        return carry

    def norm_rows(r, carry):
        rows = block(r)
        mu = jnp.tile(mu_ref[rows, :], (1, LN_COL_BLOCK // lanes))
        rs = jnp.tile(rs_ref[rows, :], (1, LN_COL_BLOCK // lanes))
        for c0 in range(0, z_ref.shape[1], LN_COL_BLOCK):
            cols = slice(c0, c0 + LN_COL_BLOCK)
            y = (z_ref[rows, cols] - mu) * rs * g_ref[:, cols] + b_ref[:, cols]
            of_ref[rows, cols] = y
            for ob_ref in maybe_ob_ref:
                ob_ref[rows, cols] = y.astype(ob_ref.dtype)
        return carry

    lax.fori_loop(0, nblk, mean_rows, 0, unroll=4)
    lax.fori_loop(0, nblk, rstd_rows, 0, unroll=4)
    lax.fori_loop(0, nblk, norm_rows, 0, unroll=2)


def _layernorm(z, gain, bias, *, tm, emit_bf16):
    m, n = z.shape
    row_spec = pl.BlockSpec((tm, n), lambda i: (i, 0))
    vec_spec = pl.BlockSpec((1, n), lambda i: (0, 0))
    out_shape = [jax.ShapeDtypeStruct((m, n), F32)]
    if emit_bf16:
        out_shape.append(jax.ShapeDtypeStruct((m, n), BF16))
    return pl.pallas_call(
        _ln_kernel,
        grid=(m // tm,),
        in_specs=[row_spec, vec_spec, vec_spec],
        out_specs=[row_spec] * len(out_shape),
        out_shape=out_shape,
        scratch_shapes=[pltpu.VMEM((tm, V7X_LANES), F32), pltpu.VMEM((tm, V7X_LANES), F32)],
        compiler_params=_params("parallel"),
        name="layernorm",
    )(z, gain.reshape(1, n), bias.reshape(1, n))


def _chunk_steps(qs, ks, vs, las, st_ref, tril_bf, causal):
    c = qs[0].shape[0]
    nt = (((1,), (1,)), ((), ()))
    tn = (((0,), (0,)), ((), ()))
    heads = range(len(qs))
    bs = []
    for la in las:
        la_hi = la.astype(BF16)
        la_lo = (la - la_hi.astype(F32)).astype(BF16)
        bs.append(jnp.dot(tril_bf, la_hi, preferred_element_type=F32)
                  + jnp.dot(tril_bf, la_lo, preferred_element_type=F32))
    q_decs, scores, kvs, inters, decays = [], [], [], [], []
    for hd in heads:
        b = bs[hd]
        b_end = b[c - 1:c, :]
        q_dec = (qs[hd] * jnp.exp(b)).astype(BF16)
        k_inv = (ks[hd] * jnp.exp(-b)).astype(BF16)
        k_end = (ks[hd] * jnp.exp(b_end - b)).astype(BF16)
        v_bf = vs[hd].astype(BF16)
        vs[hd] = v_bf
        scores.append(lax.dot_general(q_dec, k_inv, nt, preferred_element_type=F32))
        kvs.append(lax.dot_general(v_bf, k_end, tn, preferred_element_type=F32))
        inters.append(lax.dot_general(q_dec, st_ref[hd].astype(BF16), nt, preferred_element_type=F32))
        decays.append(jnp.exp(b_end))
    outs = []
    for hd in heads:
        sc = jnp.where(causal, scores[hd], 0.0).astype(BF16)
        outs.append(jnp.dot(sc, vs[hd], preferred_element_type=F32) + inters[hd])
        st_ref[hd] = decays[hd] * st_ref[hd] + kvs[hd]
    return outs


def _gated_rmsnorm(o, gain, gate):
    ms = jnp.mean(o * o, axis=-1, keepdims=True)
    return o * lax.rsqrt(ms + NORM_EPS) * gain * (gate * jax.nn.sigmoid(gate))


def _chunk_masks():
    row = lax.broadcasted_iota(jnp.int32, (CHUNK, CHUNK), 0)
    col = lax.broadcasted_iota(jnp.int32, (CHUNK, CHUNK), 1)
    causal = row >= col
    return jnp.where(causal, 1.0, 0.0).astype(BF16), causal


def _log_sigmoid(z):
    return jnp.minimum(z, 0.0) - jnp.log1p(jnp.exp(-jnp.abs(z)))


def _gla_kernel(q_ref, k_ref, v_ref, g_ref, alr_ref, wa_ref, ba_ref, gain_ref, o_ref, st_ref,
                *, hb, nchunks, unroll):
    @pl.when(pl.program_id(2) == 0)
    def _():
        st_ref[...] = jnp.zeros_like(st_ref)

    tril_bf, causal = _chunk_masks()
    gain = gain_ref[...]
    ba = ba_ref[...]
    wa = wa_ref[...]

    def body(c, carry):
        rows = pl.ds(pl.multiple_of(c * CHUNK, CHUNK), CHUNK)
        z = jnp.dot(alr_ref[rows, :].astype(BF16), wa, preferred_element_type=F32) + ba
        la_all = _log_sigmoid(z) / GLA_TAU
        kcs = [slice(hd * GLA_DK, (hd + 1) * GLA_DK) for hd in range(hb)]
        vcs = [slice(hd * GLA_DV, (hd + 1) * GLA_DV) for hd in range(hb)]
        outs = _chunk_steps([q_ref[rows, kc] * (GLA_DK ** -0.5) for kc in kcs],
                            [k_ref[rows, kc] for kc in kcs], [v_ref[rows, vc] for vc in vcs],
                            [la_all[:, kc] for kc in kcs], st_ref, tril_bf, causal)
        for o, vc in zip(outs, vcs):
            o_ref[rows, vc] = _gated_rmsnorm(o, gain, g_ref[rows, vc]).astype(o_ref.dtype)
        return carry

    lax.fori_loop(0, nchunks, body, 0, unroll=unroll)


def _gla(h, alr, w_alpha, b_alpha, norm_g, *, bsz, seq, col_q, col_k, col_v, col_g, tt, hb, unroll):
    n = bsz * seq
    nt = seq // tt
    kw, vw = hb * GLA_DK, hb * GLA_DV
    hg = GLA_HEADS // hb

    def rows(b, g, t):
        return b * nt + t

    return pl.pallas_call(
        functools.partial(_gla_kernel, hb=hb, nchunks=tt // CHUNK, unroll=unroll),
        grid=(bsz, hg, nt),
        in_specs=[pl.BlockSpec((tt, kw), lambda b, g, t: (rows(b, g, t), col_q // kw + g)),
                  pl.BlockSpec((tt, kw), lambda b, g, t: (rows(b, g, t), col_k // kw + g)),
                  pl.BlockSpec((tt, vw), lambda b, g, t: (rows(b, g, t), col_v // vw + g)),
                  pl.BlockSpec((tt, vw), lambda b, g, t: (rows(b, g, t), col_g // vw + g)),
                  pl.BlockSpec((tt, V7X_LANES), lambda b, g, t: (rows(b, g, t), 0)),
                  pl.BlockSpec((V7X_LANES, kw), lambda b, g, t: (0, g)),
                  pl.BlockSpec((1, kw), lambda b, g, t: (0, g)),
                  pl.BlockSpec((1, GLA_DV), lambda b, g, t: (0, 0))],
        out_specs=pl.BlockSpec((tt, vw), lambda b, g, t: (rows(b, g, t), g)),
        out_shape=jax.ShapeDtypeStruct((n, GLA_HEADS * GLA_DV), BF16),
        scratch_shapes=[pltpu.VMEM((hb, GLA_DV, GLA_DK), F32)],
        compiler_params=_params("parallel", "parallel", "arbitrary"),
        name="gla",
    )(h, h, h, h, alr, w_alpha, b_alpha, norm_g)


def _hgrn_kernel(q_ref, f_ref, i_ref, g_ref, tab_ref, gain_ref, o_ref, st_ref, *, layer, hb, nchunks,
                 unroll):
    @pl.when(pl.program_id(2) == 0)
    def _():
        st_ref[...] = jnp.zeros_like(st_ref)

    tab = tab_ref[...]
    e = jnp.exp(tab - jnp.max(tab, axis=0, keepdims=True))
    soft = e / jnp.sum(e, axis=0, keepdims=True)
    lb_all = jnp.sum(soft[0:layer + 1], axis=0, keepdims=True) - soft[0:1]
    tril_bf, causal = _chunk_masks()
    gain = gain_ref[...]

    def body(c, carry):
        rows = pl.ds(pl.multiple_of(c * CHUNK, CHUNK), CHUNK)
        kcs = [slice(hd * HG_DK, (hd + 1) * HG_DK) for hd in range(hb)]
        vcs = [slice(hd * HG_DV, (hd + 1) * HG_DV) for hd in range(hb)]
        fs = [lb_all[:, kc] + (1.0 - lb_all[:, kc]) * jax.nn.sigmoid(f_ref[rows, kc]) for kc in kcs]
        ivs = [i_ref[rows, vc] for vc in vcs]
        outs = _chunk_steps([q_ref[rows, kc] for kc in kcs], [1.0 - f for f in fs],
                            [iv * jax.nn.sigmoid(iv) for iv in ivs], [jnp.log(f) for f in fs],
                            st_ref, tril_bf, causal)
        for o, vc in zip(outs, vcs):
            o_ref[rows, vc] = _gated_rmsnorm(o, gain, g_ref[rows, vc]).astype(o_ref.dtype)
        return carry

    lax.fori_loop(0, nchunks, body, 0, unroll=unroll)


def _hgrn(h, lb_table, norm_g, *, layer, bsz, seq, heads, tt, hb, unroll):
    n = bsz * seq
    nt = seq // tt
    w = hb * HG_DK
    hg = heads // hb

    def spec(seg):
        return pl.BlockSpec((tt, w), lambda b, g, t: (b * nt + t, seg * hg + g))

    return pl.pallas_call(
        functools.partial(_hgrn_kernel, layer=layer, hb=hb, nchunks=tt // CHUNK, unroll=unroll),
        grid=(bsz, hg, nt),
        in_specs=[spec(0), spec(1), spec(2), spec(3),
                  pl.BlockSpec((DEPTH, w), lambda b, g, t: (0, g)),
                  pl.BlockSpec((1, HG_DV), lambda b, g, t: (0, 0))],
        out_specs=pl.BlockSpec((tt, w), lambda b, g, t: (b * nt + t, g)),
        out_shape=jax.ShapeDtypeStruct((n, heads * HG_DV), BF16),
        scratch_shapes=[pltpu.VMEM((hb, HG_DV, HG_DK), F32)],
        compiler_params=_params("parallel", "parallel", "arbitrary"),
        name="hgrn2",
    )(h, h, h, h, lb_table, norm_g)


def _s5_prep_kernel(lr_row_ref, li_row_ref, lr_col_ref, li_col_ref, ldt_ref,
                    btr_ref, bti_ref, ctr_ref, cti_ref, w1_ref, f_ref, coef_ref):
    L, J, P = S5_CHUNK, S5_GROUP, S5_STATE
    hi = lax.Precision.HIGHEST
    dt = jnp.exp(ldt_ref[0])
    lr, li = lr_row_ref[0], li_row_ref[0]
    npow = ((L - 1) - lax.broadcasted_iota(jnp.int32, (L, 2 * P), 0)).astype(F32)
    mag = jnp.exp(npow * (lr * dt))
    th = npow * (li * dt)
    ar, ai = mag * jnp.cos(th), mag * jnp.sin(th)
    half = lax.broadcasted_iota(jnp.int32, (L, 2 * P), 1) < P
    a1r, a1i = ar[L - 2:L - 1], ai[L - 2:L - 1]
    nr, ni = a1r - 1.0, a1i
    den = lr * lr + li * li
    fr, fi = (nr * lr + ni * li) / den, (ni * lr - nr * li) / den
    btr, bti = btr_ref[0], bti_ref[0]
    bbr, bbi = fr * btr - fi * bti, fr * bti + fi * btr
    pa, pb = jnp.where(half, ar, ai), jnp.where(half, -ai, ar)
    for s in range(L):
        rows = slice(s * J, (s + 1) * J)
        w1_ref[0, rows, L * J:] = (bbr * pa[s:s + 1] + bbi * pb[s:s + 1]).astype(BF16)
    alr_, ali_ = ar[0:1] * a1r - ai[0:1] * a1i, ar[0:1] * a1i + ai[0:1] * a1r
    half1 = half[0:1]
    coef_ref[0, 0:1, :] = alr_
    coef_ref[0, 1:2, :] = jnp.where(half1, -ali_, ali_)
    coef_ref[0, 2:, :] = jnp.zeros((V7X_SUBLANES - 2, 2 * P), F32)

    lrc, lic = lr_col_ref[0], li_col_ref[0]
    lane = lax.broadcasted_iota(jnp.int32, (P, L * J), 1)
    lag = jnp.right_shift(lane, int(math.log2(J))).astype(F32)
    magq = jnp.exp(lag * (lrc * dt))
    thq = lag * (lic * dt)
    aqr, aqi = magq * jnp.cos(thq), magq * jnp.sin(thq)
    cr, ci = ctr_ref[0], cti_ref[0]
    qr, qi = cr * aqr - ci * aqi, cr * aqi + ci * aqr
    r = (jnp.dot(bbr[:, :P], qr, precision=hi, preferred_element_type=F32)
         - jnp.dot(bbi[:, :P], qi, precision=hi, preferred_element_type=F32))
    lane_r = lax.broadcasted_iota(jnp.int32, (J, L * J), 1)
    for s in range(L):
        blk = r if s == 0 else jnp.where(lane_r >= s * J, pltpu.roll(r, s * J, 1), 0.0)
        w1_ref[0, s * J:(s + 1) * J, 0:L * J] = blk.astype(BF16)
    m1 = jnp.exp(lrc * dt)
    c1r, c1i = m1 * jnp.cos(lic * dt), m1 * jnp.sin(lic * dt)
    f_ref[0, 0:P, :] = (qr * c1r - qi * c1i).astype(BF16)
    f_ref[0, P:, :] = (-(qr * c1i + qi * c1r)).astype(BF16)


def _s5_prep(lam_re, lam_im, log_dt, b_re, b_im, c_re, c_im):
    g = lam_re.shape[0]
    P, J, L = S5_STATE, S5_GROUP, S5_CHUNK
    dbl = lambda a: jnp.concatenate([a, a], axis=-1)
    lr_row = dbl(lam_re).reshape(g, 1, 2 * P)
    li_row = dbl(lam_im).reshape(g, 1, 2 * P)
    lr_col = lam_re.reshape(g, P, 1)
    li_col = lam_im.reshape(g, P, 1)
    ldt = log_dt.reshape(g, 1, 1)
    btr = dbl(jnp.swapaxes(b_re, 1, 2))
    bti = dbl(jnp.swapaxes(b_im, 1, 2))
    ctr = jnp.tile(jnp.swapaxes(c_re, 1, 2), (1, 1, L))
    cti = jnp.tile(jnp.swapaxes(c_im, 1, 2), (1, 1, L))
    blk = lambda *s: pl.BlockSpec((1,) + s, lambda i: (i, 0, 0))
    return pl.pallas_call(
        _s5_prep_kernel,
        grid=(g,),
        in_specs=[blk(1, 2 * P), blk(1, 2 * P), blk(P, 1), blk(P, 1), blk(1, 1),
                  blk(J, 2 * P), blk(J, 2 * P), blk(P, L * J), blk(P, L * J)],
        out_specs=[blk(L * J, L * J + 2 * P), blk(2 * P, L * J), blk(V7X_SUBLANES, 2 * P)],
        out_shape=[jax.ShapeDtypeStruct((g, L * J, L * J + 2 * P), BF16),
                   jax.ShapeDtypeStruct((g, 2 * P, L * J), BF16),
                   jax.ShapeDtypeStruct((g, V7X_SUBLANES, 2 * P), F32)],
        compiler_params=_params("parallel"),
        name="s5_prep",
    )(lr_row, li_row, lr_col, li_col, ldt, btr, bti, ctr, cti)


def _s5_assemble_kernel(w1_ref, f_ref, coef_ref, wbig_ref, fbig_ref, cbig_ref):
    L, J, P, Q = S5_CHUNK, S5_GROUP, S5_STATE, S5_LANE_GROUPS
    lanes = Q * J
    width = L * lanes
    nstate = 2 * Q * P
    jbits, pbits = int(math.log2(J)), int(math.log2(P))

    def iota(shape, axis):
        return lax.broadcasted_iota(jnp.int32, shape, axis)

    a, b = iota((L * J, width), 0), iota((L * J, width), 1)
    sel_sc = jnp.where(((b >> int(math.log2(lanes))) == (a >> jbits)) & ((b & (J - 1)) == (a & (J - 1))),
                       1.0, 0.0).astype(BF16)
    a, b = iota((2 * P, nstate), 0), iota((2 * P, nstate), 1)
    sel_st = jnp.where(((b >> int(math.log2(Q * P))) == (a >> pbits)) & ((b & (P - 1)) == (a & (P - 1))),
                       1.0, 0.0).astype(BF16)
    col_group_sc = (iota((lanes, width), 1) >> jbits) & (Q - 1)
    col_group_st = (iota((lanes, nstate), 1) >> pbits) & (Q - 1)
    row_group_j = iota((lanes, 1), 0) >> jbits

    r_stack = jnp.concatenate([w1_ref[g, 0:J, 0:L * J] for g in range(Q)], axis=0)
    r_big = jnp.dot(r_stack, sel_sc, preferred_element_type=F32)
    r_big = jnp.where(col_group_sc == row_group_j, r_big, 0.0).astype(BF16)
    for s in range(L):
        rows = slice(s * lanes, (s + 1) * lanes)
        if s:
            wbig_ref[0, rows, 0:s * lanes] = jnp.zeros((lanes, s * lanes), BF16)
        wbig_ref[0, rows, s * lanes:width] = r_big[:, 0:width - s * lanes]
        e_stack = jnp.concatenate([w1_ref[g, s * J:(s + 1) * J, L * J:] for g in range(Q)], axis=0)
        e_big = jnp.dot(e_stack, sel_st, preferred_element_type=F32)
        wbig_ref[0, rows, width:] = jnp.where(col_group_st == row_group_j, e_big, 0.0).astype(BF16)

    row_in_pair = iota((lanes, 1), 0) >> pbits
    for m in range(nstate // lanes):
        ri, g0 = divmod(m * (lanes // P), Q)
        f_stack = jnp.concatenate([f_ref[g0 + t, ri * P:(ri + 1) * P, :] for t in range(lanes // P)], axis=0)
        f_big = jnp.dot(f_stack, sel_sc, preferred_element_type=F32)
        fbig_ref[0, m * lanes:(m + 1) * lanes, :] = jnp.where(
            col_group_sc == g0 + row_in_pair, f_big, 0.0).astype(BF16)

    low = iota((V7X_SUBLANES, lanes), 1) < P
    for k in range(Q // 2):
        c0, c1 = coef_ref[2 * k], coef_ref[2 * k + 1]
        row_is_a2 = iota((V7X_SUBLANES, lanes), 0) == 1
        re_half = jnp.where(low, c0, jnp.where(row_is_a2, -c1, c1))
        im_half = jnp.where(low, jnp.where(row_is_a2, -c0, c0), c1)
        cbig_ref[0, :, k * lanes:(k + 1) * lanes] = re_half
        cbig_ref[0, :, Q * P + k * lanes:Q * P + (k + 1) * lanes] = im_half


def _s5_assemble(w1, fmat, coef):
    g = w1.shape[0]
    L, J, P, Q = S5_CHUNK, S5_GROUP, S5_STATE, S5_LANE_GROUPS
    nb = g // Q
    width, nstate = L * Q * J, 2 * Q * P
    grp = lambda *s: pl.BlockSpec((Q,) + s, lambda i: (i, 0, 0))
    out = lambda *s: pl.BlockSpec((1,) + s, lambda i: (i, 0, 0))
    return pl.pallas_call(
        _s5_assemble_kernel,
        grid=(nb,),
        in_specs=[grp(L * J, L * J + 2 * P), grp(2 * P, L * J), grp(V7X_SUBLANES, 2 * P)],
        out_specs=[out(width, width + nstate), out(nstate, width), out(V7X_SUBLANES, nstate)],
        out_shape=[jax.ShapeDtypeStruct((nb, width, width + nstate), BF16),
                   jax.ShapeDtypeStruct((nb, nstate, width), BF16),
                   jax.ShapeDtypeStruct((nb, V7X_SUBLANES, nstate), F32)],
        compiler_params=_params("parallel"),
        name="s5_assemble",
    )(w1, fmat, coef)


def _s5_main_kernel(u_ref, w_ref, f_ref, coef_ref, o_ref, ys_ref, sp_ref, st_ref, *, nchunk, nb):
    width = u_ref.shape[2]
    half = st_ref.shape[1] // 2

    @pl.when(pl.program_id(1) == 0)
    def _():
        st_ref[...] = jnp.zeros_like(st_ref)

    ys_ref[...] = jnp.dot(u_ref[0], w_ref[0], preferred_element_type=F32)
    a1, a2 = coef_ref[0, 0:1, :], coef_ref[0, 1:2, :]

    def body(c, s):
        rows = pl.ds(pl.multiple_of(c * nb, nb), nb)
        sp_ref[rows, :] = s
        swapped = jnp.concatenate([s[:, half:], s[:, :half]], axis=1)
        return a1 * s + a2 * swapped + ys_ref[rows, width:]

    st_ref[...] = lax.fori_loop(0, nchunk, body, st_ref[...], unroll=4)
    y = ys_ref[:, 0:width] + jnp.dot(sp_ref[...].astype(BF16), f_ref[0], preferred_element_type=F32)
    o_ref[0] = y.astype(o_ref.dtype)


def _s5_main(u_c, wbig, fbig, cbig, *, nb, rows_per_step):
    nblk, rows, width = u_c.shape
    nstate = fbig.shape[1]
    nsteps = rows // rows_per_step
    fixed = lambda *s: pl.BlockSpec((1,) + s, lambda i, r: (i, 0, 0))
    moving = pl.BlockSpec((1, rows_per_step, width), lambda i, r: (i, r, 0))
    return pl.pallas_call(
        functools.partial(_s5_main_kernel, nchunk=rows_per_step // nb, nb=nb),
        grid=(nblk, nsteps),
        in_specs=[moving, fixed(width, width + nstate), fixed(nstate, width),
                  fixed(V7X_SUBLANES, nstate)],
        out_specs=moving,
        out_shape=jax.ShapeDtypeStruct((nblk, rows, width), BF16),
        scratch_shapes=[pltpu.VMEM((rows_per_step, width + nstate), F32),
                        pltpu.VMEM((rows_per_step, nstate), F32),
                        pltpu.VMEM((nb, nstate), F32)],
        compiler_params=_params("parallel", "arbitrary"),
        name="s5_main",
    )(u_c, wbig, fbig, cbig)


def _s5_glu_kernel(ys_ref, u_ref, d_ref, w_ref, o_ref):
    y = jax.nn.gelu(ys_ref[...].astype(F32) + d_ref[...] * u_ref[...], approximate=True)
    z = jnp.dot(y.astype(BF16), w_ref[...], preferred_element_type=F32)
    o_ref[...] = (y * jax.nn.sigmoid(z)).astype(o_ref.dtype)


def _s5_glu(y_ssm, h, d_skip, w_glu, *, tm):
    m, n = y_ssm.shape
    return pl.pallas_call(
        _s5_glu_kernel,
        grid=(m // tm,),
        in_specs=[pl.BlockSpec((tm, n), lambda i: (i, 0)),
                  pl.BlockSpec((tm, n), lambda i: (i, 0)),
                  pl.BlockSpec((1, n), lambda i: (0, 0)),
                  pl.BlockSpec((n, n), lambda i: (0, 0))],
        out_specs=pl.BlockSpec((tm, n), lambda i: (i, 0)),
        out_shape=jax.ShapeDtypeStruct((m, n), BF16),
        compiler_params=_params("parallel"),
        name="s5_glu",
    )(y_ssm, h, d_skip.reshape(1, n), w_glu)


def _pad_cols(w, n):
    return jnp.pad(w, ((0, 0), (0, n - w.shape[1])))


def _ffn_block(x_f32, x_bf, w_gate, w_up, w_down, layer, ln_g, ln_b, *, emit_bf16):
    hidden = w_gate.shape[2]
    hpad = -(-hidden // FFN_DOWN_TK) * FFN_DOWN_TK
    wd = jnp.pad(w_down[layer].astype(BF16), ((0, hpad - hidden), (0, 0)))
    a = _ffn_up(x_bf, w_gate, w_up, layer, tm=1024, tn=256, n_out=hpad)
    z = _matmul_resid([a], wd, x_f32, tm=1024, tn=1024, tk=FFN_DOWN_TK, name="ffn_down")
    return _layernorm(z, ln_g, ln_b, tm=LN_TM, emit_bf16=emit_bf16)


def _even_layer(x_f32, x_bf, bsz, seq, e, w_in, w_out, lam_re, lam_im, log_dt, b_re, b_im, c_re, c_im,
                d_skip, w_glu, w_alpha, b_alpha, norm_g, ln_g, ln_b):
    n = bsz * seq
    s5w = d_skip.shape[0]
    groups = s5w // S5_GROUP
    kw = GLA_HEADS * GLA_DK
    vw = GLA_HEADS * GLA_DV
    main = s5w + 2 * kw + 2 * vw
    h = _matmul(x_bf, w_in, e, n=main, tm=1024, tn=512, out_dtype=F32, name="even_in")
    alr = _matmul(x_bf, _pad_cols(w_in[e, :, main:], V7X_LANES)[None], 0, n=V7X_LANES, tm=1024,
                  tn=V7X_LANES, out_dtype=F32, name="even_in_gate")

    w1, fmat, coef = _s5_prep(lam_re, lam_im, log_dt, b_re, b_im, c_re, c_im)
    wbig, fbig, cbig = _s5_assemble(w1, fmat, coef)
    nchunk = seq // S5_CHUNK
    nblk = s5w // V7X_LANES
    u_c = (h[:, :s5w].astype(BF16).reshape(bsz, nchunk, S5_CHUNK, nblk, V7X_LANES)
           .transpose(3, 1, 0, 2, 4).reshape(nblk, nchunk * bsz, S5_CHUNK * V7X_LANES))
    y_c = _s5_main(u_c, wbig, fbig, cbig, nb=bsz, rows_per_step=256)
    y_ssm = (y_c.reshape(nblk, nchunk, bsz, S5_CHUNK, V7X_LANES)
             .transpose(2, 1, 3, 0, 4).reshape(n, s5w))
    y_s5 = _s5_glu(y_ssm, h, d_skip, w_glu.astype(BF16), tm=512)

    wa = jnp.pad(w_alpha, ((0, V7X_LANES - w_alpha.shape[0]), (0, 0))).astype(BF16)
    y_gla = _gla(h, alr, wa, b_alpha.reshape(1, kw), norm_g.reshape(1, GLA_DV), bsz=bsz, seq=seq,
                 col_q=s5w, col_k=s5w + kw, col_v=s5w + 2 * kw, col_g=s5w + 2 * kw + vw,
                 tt=256, hb=4, unroll=2)

    z = _matmul_resid([y_s5, y_gla], w_out.astype(BF16), x_f32, tm=1024, tn=1024, tk=w_out.shape[0],
                      name="even_out")
    return _layernorm(z, ln_g, ln_b, tm=LN_TM, emit_bf16=True)


def _odd_layer(x_f32, x_bf, bsz, seq, layer, o, w_in, w_out, lb_table, norm_g, ln_g, ln_b):
    heads = w_out.shape[0] // HG_DV
    h = _matmul(x_bf, w_in, o, n=w_in.shape[2], tm=1024, tn=512, out_dtype=F32, name="odd_in")
    y = _hgrn(h, lb_table, norm_g.reshape(1, HG_DV), layer=layer, bsz=bsz, seq=seq, heads=heads,
              tt=256, hb=8, unroll=2)
    z = _matmul_resid([y], w_out.astype(BF16), x_f32, tm=1024, tn=1024, tk=w_out.shape[0], name="odd_out")
    return _layernorm(z, ln_g, ln_b, tm=LN_TM, emit_bf16=True)


def kernel(x, ev_w_in, ev_w_out, s5_lam_re, s5_lam_im, s5_log_dt, s5_b_re, s5_b_im, s5_c_re, s5_c_im,
           s5_d, s5_w_glu, gla_w_alpha, gla_b_alpha, gla_norm_g, od_w_in, od_w_out, hg_lb_table,
           hg_norm_g, ln_mix_g, ln_mix_b, ln_ffn_g, ln_ffn_b, ffn_w_gate, ffn_w_up, ffn_w_down):
    bsz, seq, d = x.shape
    depth = ln_mix_g.shape[0]
    assert depth == DEPTH
    xf = x.reshape(bsz * seq, d)
    xb = xf.astype(BF16)
    for layer in range(depth):
        if layer % 2 == 0:
            e = layer // 2
            xf, xb = _even_layer(xf, xb, bsz, seq, e, ev_w_in, ev_w_out[e], s5_lam_re[e], s5_lam_im[e],
                                 s5_log_dt[e], s5_b_re[e], s5_b_im[e], s5_c_re[e], s5_c_im[e], s5_d[e],
                                 s5_w_glu[e], gla_w_alpha[e], gla_b_alpha[e], gla_norm_g[e],
                                 ln_mix_g[layer], ln_mix_b[layer])
        else:
            o = layer // 2
            xf, xb = _odd_layer(xf, xb, bsz, seq, layer, o, od_w_in, od_w_out[o], hg_lb_table,
                                hg_norm_g[o], ln_mix_g[layer], ln_mix_b[layer])
        last = layer == depth - 1
        res = _ffn_block(xf, xb, ffn_w_gate, ffn_w_up, ffn_w_down, layer,
                         ln_ffn_g[layer], ln_ffn_b[layer], emit_bf16=not last)
        xf, xb = (res[0], None) if last else res
    return xf.reshape(bsz, seq, d).astype(x.dtype)
```

```python
import functools
import math

import jax
import jax.numpy as jnp
from jax import lax
from jax.experimental import pallas as pl
from jax.experimental.pallas import tpu as pltpu

F32 = jnp.float32
BF16 = jnp.bfloat16

DEPTH = 2
S5_GROUP = 16
S5_STATE = 64
GLA_HEADS = 8
GLA_DK = 128
GLA_DV = 256
GLA_GATE_RANK = 16
GLA_TAU = 16.0
HG_DK = 128
HG_DV = 128
CHUNK = 64
DEEPNORM_ALPHA = (2.0 * DEPTH) ** 0.25
NORM_EPS = 1e-5

V7X_LANES = 128
V7X_SUBLANES = 8
V7X_VMEM_LIMIT_BYTES = 56 * 1024 * 1024
V7X_VMEM_COMPILER_TEMP_BYTES = 12 * 1024 * 1024

MM_ROW_CHUNK = 256
FFN_DOWN_TK = 2816
LN_TM = 256
LN_COL_BLOCK = 1024
LN_ROW_BLOCK = 16

S5_CHUNK = 16
S5_ROW = S5_CHUNK * S5_GROUP
S5_LANE_GROUPS = V7X_LANES // S5_GROUP


def _nbytes(shape, dtype):
    return math.prod(shape) * jnp.dtype(dtype).itemsize


def _params(*sem, buffers=None):
    limit = V7X_VMEM_LIMIT_BYTES
    if buffers is not None:
        need = sum(_nbytes(shape, dtype) * count for shape, dtype, count in buffers)
        limit = min(limit, need + V7X_VMEM_COMPILER_TEMP_BYTES)
    return pltpu.CompilerParams(dimension_semantics=sem, vmem_limit_bytes=limit)


def _stage_weight_tile(w_hbm, stage_ref, wb_ref, sem, *, layer, n_valid, nj):
    j = pl.program_id(0)
    tn = stage_ref.shape[1]
    last = nj - 1
    last_w = n_valid - last * tn

    def full_copy(jj):
        cols = pl.ds(pl.multiple_of(jj * tn, tn), tn)
        return pltpu.make_async_copy(w_hbm.at[layer, :, cols], stage_ref, sem)

    def last_copy():
        return pltpu.make_async_copy(w_hbm.at[layer, :, pl.ds(last * tn, last_w)],
                                     stage_ref.at[:, pl.ds(0, last_w)], sem)

    def start(jj):
        if last_w == tn:
            full_copy(jj).start()
        else:
            pl.when(jj < last)(lambda: full_copy(jj).start())
            pl.when(jj == last)(lambda: last_copy().start())

    @pl.when(pl.program_id(1) == 0)
    def _():
        pl.when(j == 0)(lambda: start(j))
        if last_w == tn:
            full_copy(j).wait()
            wb_ref[...] = stage_ref[...].astype(BF16)
        else:
            @pl.when(j < last)
            def _():
                full_copy(j).wait()
                wb_ref[...] = stage_ref[...].astype(BF16)

            @pl.when(j == last)
            def _():
                last_copy().wait()
                wb_ref[:, 0:last_w] = stage_ref[:, 0:last_w].astype(BF16)
                wb_ref[:, last_w:] = jnp.zeros((wb_ref.shape[0], tn - last_w), BF16)

        pl.when(j < last)(lambda: start(j + 1))


def _mm_kernel(a_ref, w_hbm, o_ref, stage_ref, wb_ref, sem, *, layer, n_valid, nj):
    _stage_weight_tile(w_hbm, stage_ref, wb_ref, sem, layer=layer, n_valid=n_valid, nj=nj)
    w = wb_ref[...]
    for r0 in range(0, a_ref.shape[0], MM_ROW_CHUNK):
        rows = slice(r0, r0 + MM_ROW_CHUNK)
        o_ref[rows, :] = jnp.dot(a_ref[rows, :], w, preferred_element_type=F32).astype(o_ref.dtype)


def _matmul(a, w, layer, *, n, tm, tn, out_dtype, name):
    m, k = a.shape
    nj = n // tn
    return pl.pallas_call(
        functools.partial(_mm_kernel, layer=layer, n_valid=n, nj=nj),
        grid=(nj, m // tm),
        in_specs=[pl.BlockSpec((tm, k), lambda j, i: (i, 0)),
                  pl.BlockSpec(memory_space=pl.ANY)],
        out_specs=pl.BlockSpec((tm, tn), lambda j, i: (i, j)),
        out_shape=jax.ShapeDtypeStruct((m, n), out_dtype),
        scratch_shapes=[pltpu.VMEM((k, tn), F32), pltpu.VMEM((k, tn), BF16), pltpu.SemaphoreType.DMA],
        compiler_params=_params("arbitrary", "arbitrary", buffers=[
            ((tm, k), BF16, 2), ((k, tn), F32, 1), ((k, tn), BF16, 1), ((tm, tn), out_dtype, 2)]),
        name=name,
    )(a, w)


def _ffn_up_kernel(x_ref, wg_hbm, wu_hbm, o_ref, sg_ref, su_ref, wgb_ref, wub_ref, sems, *,
                   layer, n_valid, nj):
    _stage_weight_tile(wg_hbm, sg_ref, wgb_ref, sems.at[0], layer=layer, n_valid=n_valid, nj=nj)
    _stage_weight_tile(wu_hbm, su_ref, wub_ref, sems.at[1], layer=layer, n_valid=n_valid, nj=nj)
    wg, wu = wgb_ref[...], wub_ref[...]
    for r0 in range(0, x_ref.shape[0], MM_ROW_CHUNK):
        rows = slice(r0, r0 + MM_ROW_CHUNK)
        x = x_ref[rows, :]
        g = jnp.dot(x, wg, preferred_element_type=F32)
        u = jnp.dot(x, wu, preferred_element_type=F32)
        o_ref[rows, :] = (g * jax.nn.sigmoid(g) * u).astype(o_ref.dtype)


def _ffn_up(x_bf, wg, wu, layer, *, tm, tn, n_out):
    m, k = x_bf.shape
    nj = n_out // tn
    hbm = pl.BlockSpec(memory_space=pl.ANY)
    return pl.pallas_call(
        functools.partial(_ffn_up_kernel, layer=layer, n_valid=wg.shape[2], nj=nj),
        grid=(nj, m // tm),
        in_specs=[pl.BlockSpec((tm, k), lambda j, i: (i, 0)), hbm, hbm],
        out_specs=pl.BlockSpec((tm, tn), lambda j, i: (i, j)),
        out_shape=jax.ShapeDtypeStruct((m, n_out), BF16),
        scratch_shapes=[pltpu.VMEM((k, tn), F32), pltpu.VMEM((k, tn), F32),
                        pltpu.VMEM((k, tn), BF16), pltpu.VMEM((k, tn), BF16),
                        pltpu.SemaphoreType.DMA((2,))],
        compiler_params=_params("arbitrary", "arbitrary", buffers=[
            ((tm, k), BF16, 2), ((k, tn), F32, 2), ((k, tn), BF16, 2), ((tm, tn), BF16, 2)]),
        name="ffn_up",
    )(x_bf, wg, wu)


def _mm_resid_kernel(*refs, k_parts, nk):
    n_a = len(k_parts)
    a_refs = refs[:n_a]
    w_ref, r_ref, o_ref = refs[n_a:]
    k = pl.program_id(2)

    def product():
        acc, k0 = None, 0
        for a_ref, kp in zip(a_refs, k_parts):
            d = jnp.dot(a_ref[...], w_ref[k0:k0 + kp, :], preferred_element_type=F32)
            acc = d if acc is None else acc + d
            k0 += kp
        return acc

    if nk == 1:
        o_ref[...] = product() + DEEPNORM_ALPHA * r_ref[...]
    else:
        @pl.when(k == 0)
        def _():
            o_ref[...] = product() + DEEPNORM_ALPHA * r_ref[...]

        @pl.when(k > 0)
        def _():
            o_ref[...] += product()


def _matmul_resid(a_parts, w, resid, *, tm, tn, tk, name):
    m, n = resid.shape
    kdim = w.shape[0]
    if len(a_parts) > 1:
        assert tk == kdim
        k_parts = tuple(a.shape[1] for a in a_parts)
        a_specs = [pl.BlockSpec((tm, kp), lambda i, j, k: (i, 0)) for kp in k_parts]
    else:
        k_parts = (tk,)
        a_specs = [pl.BlockSpec((tm, tk), lambda i, j, k: (i, k))]
    nk = kdim // tk
    tile = pl.BlockSpec((tm, tn), lambda i, j, k: (i, j))
    return pl.pallas_call(
        functools.partial(_mm_resid_kernel, k_parts=k_parts, nk=nk),
        grid=(m // tm, n // tn, nk),
        in_specs=a_specs + [pl.BlockSpec((tk, tn), lambda i, j, k: (k, j)), tile],
        out_specs=tile,
        out_shape=jax.ShapeDtypeStruct((m, n), F32),
        compiler_params=_params("parallel", "parallel", "arbitrary"),
        name=name,
    )(*a_parts, w, resid)


def _ln_kernel(z_ref, g_ref, b_ref, of_ref, *rest):
    maybe_ob_ref, (mu_ref, rs_ref) = rest[:-2], rest[-2:]
    inv_n = 1.0 / z_ref.shape[1]
    nblk = z_ref.shape[0] // LN_ROW_BLOCK
    lanes = mu_ref.shape[1]
    nrep = z_ref.shape[1] // lanes

    def block(r):
        return pl.ds(pl.multiple_of(r * LN_ROW_BLOCK, LN_ROW_BLOCK), LN_ROW_BLOCK)

    def mean_rows(r, carry):
        rows = block(r)
        mu = jnp.sum(z_ref[rows, :], axis=-1, keepdims=True) * inv_n
        mu_ref[rows, :] = jnp.broadcast_to(mu, (LN_ROW_BLOCK, lanes))
        return carry

    def rstd_rows(r, carry):
        rows = block(r)
        zc = z_ref[rows, :] - jnp.tile(mu_ref[rows, :], (1, nrep))
        var = jnp.sum(zc * zc, axis=-1, keepdims=True) * inv_n
        rs_ref[rows, :] = jnp.broadcast_to(lax.rsqrt(var + NORM_EPS), (LN_ROW_BLOCK, lanes))
        return carry

    def norm_rows(r, carry):
        rows = block(r)
        mu = jnp.tile(mu_ref[rows, :], (1, LN_COL_BLOCK // lanes))
        rs = jnp.tile(rs_ref[rows, :], (1, LN_COL_BLOCK // lanes))
        for c0 in range(0, z_ref.shape[1], LN_COL_BLOCK):
            cols = slice(c0, c0 + LN_COL_BLOCK)
            y = (z_ref[rows, cols] - mu) * rs * g_ref[:, cols] + b_ref[:, cols]
            of_ref[rows, cols] = y
            for ob_ref in maybe_ob_ref:
                ob_ref[rows, cols] = y.astype(ob_ref.dtype)
        return carry

    lax.fori_loop(0, nblk, mean_rows, 0, unroll=4)
    lax.fori_loop(0, nblk, rstd_rows, 0, unroll=4)
    lax.fori_loop(0, nblk, norm_rows, 0, unroll=2)


def _layernorm(z, gain, bias, *, tm, emit_bf16):
    m, n = z.shape
    row_spec = pl.BlockSpec((tm, n), lambda i: (i, 0))
    vec_spec = pl.BlockSpec((1, n), lambda i: (0, 0))
    out_shape = [jax.ShapeDtypeStruct((m, n), F32)]
    if emit_bf16:
        out_shape.append(jax.ShapeDtypeStruct((m, n), BF16))
    return pl.pallas_call(
        _ln_kernel,
        grid=(m // tm,),
        in_specs=[row_spec, vec_spec, vec_spec],
        out_specs=[row_spec] * len(out_shape),
        out_shape=out_shape,
        scratch_shapes=[pltpu.VMEM((tm, V7X_LANES), F32), pltpu.VMEM((tm, V7X_LANES), F32)],
        compiler_params=_params("parallel"),
        name="layernorm",
    )(z, gain.reshape(1, n), bias.reshape(1, n))


def _chunk_steps(qs, ks, vs, las, st_ref, tril_bf, causal):
    c = qs[0].shape[0]
    nt = (((1,), (1,)), ((), ()))
    tn = (((0,), (0,)), ((), ()))
    heads = range(len(qs))
    bs = []
    for la in las:
        la_hi = la.astype(BF16)
        la_lo = (la - la_hi.astype(F32)).astype(BF16)
        bs.append(jnp.dot(tril_bf, la_hi, preferred_element_type=F32)
                  + jnp.dot(tril_bf, la_lo, preferred_element_type=F32))
    q_decs, scores, kvs, inters, decays = [], [], [], [], []
    for hd in heads:
        b = bs[hd]
        b_end = b[c - 1:c, :]
        q_dec = (qs[hd] * jnp.exp(b)).astype(BF16)
        k_inv = (ks[hd] * jnp.exp(-b)).astype(BF16)
        k_end = (ks[hd] * jnp.exp(b_end - b)).astype(BF16)
        v_bf = vs[hd].astype(BF16)
        vs[hd] = v_bf
        scores.append(lax.dot_general(q_dec, k_inv, nt, preferred_element_type=F32))
        kvs.append(lax.dot_general(v_bf, k_end, tn, preferred_element_type=F32))
        inters.append(lax.dot_general(q_dec, st_ref[hd].astype(BF16), nt, preferred_element_type=F32))
        decays.append(jnp.exp(b_end))
    outs = []
    for hd in heads:
        sc = jnp.where(causal, scores[hd], 0.0).astype(BF16)
        outs.append(jnp.dot(sc, vs[hd], preferred_element_type=F32) + inters[hd])
        st_ref[hd] = decays[hd] * st_ref[hd] + kvs[hd]
    return outs


def _gated_rmsnorm(o, gain, gate):
    ms = jnp.mean(o * o, axis=-1, keepdims=True)
    return o * lax.rsqrt(ms + NORM_EPS) * gain * (gate * jax.nn.sigmoid(gate))


def _chunk_masks():
    row = lax.broadcasted_iota(jnp.int32, (CHUNK, CHUNK), 0)
    col = lax.broadcasted_iota(jnp.int32, (CHUNK, CHUNK), 1)
    causal = row >= col
    return jnp.where(causal, 1.0, 0.0).astype(BF16), causal


def _log_sigmoid(z):
    return jnp.minimum(z, 0.0) - jnp.log1p(jnp.exp(-jnp.abs(z)))


def _gla_kernel(q_ref, k_ref, v_ref, g_ref, alr_ref, wa_ref, ba_ref, gain_ref, o_ref, st_ref,
                *, hb, nchunks, unroll):
    @pl.when(pl.program_id(2) == 0)
    def _():
        st_ref[...] = jnp.zeros_like(st_ref)

    tril_bf, causal = _chunk_masks()
    gain = gain_ref[...]
    ba = ba_ref[...]
    wa = wa_ref[...]

    def body(c, carry):
        rows = pl.ds(pl.multiple_of(c * CHUNK, CHUNK), CHUNK)
        z = jnp.dot(alr_ref[rows, :].astype(BF16), wa, preferred_element_type=F32) + ba
        la_all = _log_sigmoid(z) / GLA_TAU
        kcs = [slice(hd * GLA_DK, (hd + 1) * GLA_DK) for hd in range(hb)]
        vcs = [slice(hd * GLA_DV, (hd + 1) * GLA_DV) for hd in range(hb)]
        outs = _chunk_steps([q_ref[rows, kc] * (GLA_DK ** -0.5) for kc in kcs],
                            [k_ref[rows, kc] for kc in kcs], [v_ref[rows, vc] for vc in vcs],
                            [la_all[:, kc] for kc in kcs], st_ref, tril_bf, causal)
        for o, vc in zip(outs, vcs):
            o_ref[rows, vc] = _gated_rmsnorm(o, gain, g_ref[rows, vc]).astype(o_ref.dtype)
        return carry

    lax.fori_loop(0, nchunks, body, 0, unroll=unroll)


def _gla(h, alr, w_alpha, b_alpha, norm_g, *, bsz, seq, col_q, col_k, col_v, col_g, tt, hb, unroll):
    n = bsz * seq
    nt = seq // tt
    kw, vw = hb * GLA_DK, hb * GLA_DV
    hg = GLA_HEADS // hb

    def rows(b, g, t):
        return b * nt + t

    return pl.pallas_call(
        functools.partial(_gla_kernel, hb=hb, nchunks=tt // CHUNK, unroll=unroll),
        grid=(bsz, hg, nt),
        in_specs=[pl.BlockSpec((tt, kw), lambda b, g, t: (rows(b, g, t), col_q // kw + g)),
                  pl.BlockSpec((tt, kw), lambda b, g, t: (rows(b, g, t), col_k // kw + g)),
                  pl.BlockSpec((tt, vw), lambda b, g, t: (rows(b, g, t), col_v // vw + g)),
                  pl.BlockSpec((tt, vw), lambda b, g, t: (rows(b, g, t), col_g // vw + g)),
                  pl.BlockSpec((tt, V7X_LANES), lambda b, g, t: (rows(b, g, t), 0)),
                  pl.BlockSpec((V7X_LANES, kw), lambda b, g, t: (0, g)),
                  pl.BlockSpec((1, kw), lambda b, g, t: (0, g)),
                  pl.BlockSpec((1, GLA_DV), lambda b, g, t: (0, 0))],
        out_specs=pl.BlockSpec((tt, vw), lambda b, g, t: (rows(b, g, t), g)),
        out_shape=jax.ShapeDtypeStruct((n, GLA_HEADS * GLA_DV), BF16),
        scratch_shapes=[pltpu.VMEM((hb, GLA_DV, GLA_DK), F32)],
        compiler_params=_params("parallel", "parallel", "arbitrary"),
        name="gla",
    )(h, h, h, h, alr, w_alpha, b_alpha, norm_g)


def _hgrn_kernel(q_ref, f_ref, i_ref, g_ref, tab_ref, gain_ref, o_ref, st_ref, *, layer, hb, nchunks,
                 unroll):
    @pl.when(pl.program_id(2) == 0)
    def _():
        st_ref[...] = jnp.zeros_like(st_ref)

    tab = tab_ref[...]
    e = jnp.exp(tab - jnp.max(tab, axis=0, keepdims=True))
    soft = e / jnp.sum(e, axis=0, keepdims=True)
    lb_all = jnp.sum(soft[0:layer + 1], axis=0, keepdims=True) - soft[0:1]
    tril_bf, causal = _chunk_masks()
    gain = gain_ref[...]

    def body(c, carry):
        rows = pl.ds(pl.multiple_of(c * CHUNK, CHUNK), CHUNK)
        kcs = [slice(hd * HG_DK, (hd + 1) * HG_DK) for hd in range(hb)]
        vcs = [slice(hd * HG_DV, (hd + 1) * HG_DV) for hd in range(hb)]
        fs = [lb_all[:, kc] + (1.0 - lb_all[:, kc]) * jax.nn.sigmoid(f_ref[rows, kc]) for kc in kcs]
        ivs = [i_ref[rows, vc] for vc in vcs]
        outs = _chunk_steps([q_ref[rows, kc] for kc in kcs], [1.0 - f for f in fs],
                            [iv * jax.nn.sigmoid(iv) for iv in ivs], [jnp.log(f) for f in fs],
                            st_ref, tril_bf, causal)
        for o, vc in zip(outs, vcs):
            o_ref[rows, vc] = _gated_rmsnorm(o, gain, g_ref[rows, vc]).astype(o_ref.dtype)
        return carry

    lax.fori_loop(0, nchunks, body, 0, unroll=unroll)


def _hgrn(h, lb_table, norm_g, *, layer, bsz, seq, heads, tt, hb, unroll):
    n = bsz * seq
    nt = seq // tt
    w = hb * HG_DK
    hg = heads // hb

    def spec(seg):
        return pl.BlockSpec((tt, w), lambda b, g, t: (b * nt + t, seg * hg + g))

    return pl.pallas_call(
        functools.partial(_hgrn_kernel, layer=layer, hb=hb, nchunks=tt // CHUNK, unroll=unroll),
        grid=(bsz, hg, nt),
        in_specs=[spec(0), spec(1), spec(2), spec(3),
                  pl.BlockSpec((DEPTH, w), lambda b, g, t: (0, g)),
                  pl.BlockSpec((1, HG_DV), lambda b, g, t: (0, 0))],
        out_specs=pl.BlockSpec((tt, w), lambda b, g, t: (b * nt + t, g)),
        out_shape=jax.ShapeDtypeStruct((n, heads * HG_DV), BF16),
        scratch_shapes=[pltpu.VMEM((hb, HG_DV, HG_DK), F32)],
        compiler_params=_params("parallel", "parallel", "arbitrary"),
        name="hgrn2",
    )(h, h, h, h, lb_table, norm_g)


def _s5_prep_kernel(lr_row_ref, li_row_ref, lr_col_ref, li_col_ref, ldt_ref,
                    btr_ref, bti_ref, ctr_ref, cti_ref, w1_ref, f_ref, coef_ref):
    L, J, P = S5_CHUNK, S5_GROUP, S5_STATE
    hi = lax.Precision.HIGHEST
    dt = jnp.exp(ldt_ref[0])
    lr, li = lr_row_ref[0], li_row_ref[0]
    npow = ((L - 1) - lax.broadcasted_iota(jnp.int32, (L, 2 * P), 0)).astype(F32)
    mag = jnp.exp(npow * (lr * dt))
    th = npow * (li * dt)
    ar, ai = mag * jnp.cos(th), mag * jnp.sin(th)
    half = lax.broadcasted_iota(jnp.int32, (L, 2 * P), 1) < P
    a1r, a1i = ar[L - 2:L - 1], ai[L - 2:L - 1]
    nr, ni = a1r - 1.0, a1i
    den = lr * lr + li * li
    fr, fi = (nr * lr + ni * li) / den, (ni * lr - nr * li) / den
    btr, bti = btr_ref[0], bti_ref[0]
    bbr, bbi = fr * btr - fi * bti, fr * bti + fi * btr
    pa, pb = jnp.where(half, ar, ai), jnp.where(half, -ai, ar)
    for s in range(L):
        rows = slice(s * J, (s + 1) * J)
        w1_ref[0, rows, L * J:] = (bbr * pa[s:s + 1] + bbi * pb[s:s + 1]).astype(BF16)
    alr_, ali_ = ar[0:1] * a1r - ai[0:1] * a1i, ar[0:1] * a1i + ai[0:1] * a1r
    half1 = half[0:1]
    coef_ref[0, 0:1, :] = alr_
    coef_ref[0, 1:2, :] = jnp.where(half1, -ali_, ali_)
    coef_ref[0, 2:, :] = jnp.zeros((V7X_SUBLANES - 2, 2 * P), F32)

    lrc, lic = lr_col_ref[0], li_col_ref[0]
    lane = lax.broadcasted_iota(jnp.int32, (P, L * J), 1)
    lag = jnp.right_shift(lane, int(math.log2(J))).astype(F32)
    magq = jnp.exp(lag * (lrc * dt))
    thq = lag * (lic * dt)
    aqr, aqi = magq * jnp.cos(thq), magq * jnp.sin(thq)
    cr, ci = ctr_ref[0], cti_ref[0]
    qr, qi = cr * aqr - ci * aqi, cr * aqi + ci * aqr
    r = (jnp.dot(bbr[:, :P], qr, precision=hi, preferred_element_type=F32)
         - jnp.dot(bbi[:, :P], qi, precision=hi, preferred_element_type=F32))
    lane_r = lax.broadcasted_iota(jnp.int32, (J, L * J), 1)
    for s in range(L):
        blk = r if s == 0 else jnp.where(lane_r >= s * J, pltpu.roll(r, s * J, 1), 0.0)
        w1_ref[0, s * J:(s + 1) * J, 0:L * J] = blk.astype(BF16)
    m1 = jnp.exp(lrc * dt)
    c1r, c1i = m1 * jnp.cos(lic * dt), m1 * jnp.sin(lic * dt)
    f_ref[0, 0:P, :] = (qr * c1r - qi * c1i).astype(BF16)
    f_ref[0, P:, :] = (-(qr * c1i + qi * c1r)).astype(BF16)


def _s5_prep(lam_re, lam_im, log_dt, b_re, b_im, c_re, c_im):
    g = lam_re.shape[0]
    P, J, L = S5_STATE, S5_GROUP, S5_CHUNK
    dbl = lambda a: jnp.concatenate([a, a], axis=-1)
    lr_row = dbl(lam_re).reshape(g, 1, 2 * P)
    li_row = dbl(lam_im).reshape(g, 1, 2 * P)
    lr_col = lam_re.reshape(g, P, 1)
    li_col = lam_im.reshape(g, P, 1)
    ldt = log_dt.reshape(g, 1, 1)
    btr = dbl(jnp.swapaxes(b_re, 1, 2))
    bti = dbl(jnp.swapaxes(b_im, 1, 2))
    ctr = jnp.tile(jnp.swapaxes(c_re, 1, 2), (1, 1, L))
    cti = jnp.tile(jnp.swapaxes(c_im, 1, 2), (1, 1, L))
    blk = lambda *s: pl.BlockSpec((1,) + s, lambda i: (i, 0, 0))
    return pl.pallas_call(
        _s5_prep_kernel,
        grid=(g,),
        in_specs=[blk(1, 2 * P), blk(1, 2 * P), blk(P, 1), blk(P, 1), blk(1, 1),
                  blk(J, 2 * P), blk(J, 2 * P), blk(P, L * J), blk(P, L * J)],
        out_specs=[blk(L * J, L * J + 2 * P), blk(2 * P, L * J), blk(V7X_SUBLANES, 2 * P)],
        out_shape=[jax.ShapeDtypeStruct((g, L * J, L * J + 2 * P), BF16),
                   jax.ShapeDtypeStruct((g, 2 * P, L * J), BF16),
                   jax.ShapeDtypeStruct((g, V7X_SUBLANES, 2 * P), F32)],
        compiler_params=_params("parallel"),
        name="s5_prep",
    )(lr_row, li_row, lr_col, li_col, ldt, btr, bti, ctr, cti)


def _s5_assemble_kernel(w1_ref, f_ref, coef_ref, wbig_ref, fbig_ref, cbig_ref):
    L, J, P, Q = S5_CHUNK, S5_GROUP, S5_STATE, S5_LANE_GROUPS
    lanes = Q * J
    width = L * lanes
    nstate = 2 * Q * P
    jbits, pbits = int(math.log2(J)), int(math.log2(P))

    def iota(shape, axis):
        return lax.broadcasted_iota(jnp.int32, shape, axis)

    a, b = iota((L * J, width), 0), iota((L * J, width), 1)
    sel_sc = jnp.where(((b >> int(math.log2(lanes))) == (a >> jbits)) & ((b & (J - 1)) == (a & (J - 1))),
                       1.0, 0.0).astype(BF16)
    a, b = iota((2 * P, nstate), 0), iota((2 * P, nstate), 1)
    sel_st = jnp.where(((b >> int(math.log2(Q * P))) == (a >> pbits)) & ((b & (P - 1)) == (a & (P - 1))),
                       1.0, 0.0).astype(BF16)
    col_group_sc = (iota((lanes, width), 1) >> jbits) & (Q - 1)
    col_group_st = (iota((lanes, nstate), 1) >> pbits) & (Q - 1)
    row_group_j = iota((lanes, 1), 0) >> jbits

    r_stack = jnp.concatenate([w1_ref[g, 0:J, 0:L * J] for g in range(Q)], axis=0)
    r_big = jnp.dot(r_stack, sel_sc, preferred_element_type=F32)
    r_big = jnp.where(col_group_sc == row_group_j, r_big, 0.0).astype(BF16)
    for s in range(L):
        rows = slice(s * lanes, (s + 1) * lanes)
        if s:
            wbig_ref[0, rows, 0:s * lanes] = jnp.zeros((lanes, s * lanes), BF16)
        wbig_ref[0, rows, s * lanes:width] = r_big[:, 0:width - s * lanes]
        e_stack = jnp.concatenate([w1_ref[g, s * J:(s + 1) * J, L * J:] for g in range(Q)], axis=0)
        e_big = jnp.dot(e_stack, sel_st, preferred_element_type=F32)
        wbig_ref[0, rows, width:] = jnp.where(col_group_st == row_group_j, e_big, 0.0).astype(BF16)

    row_in_pair = iota((lanes, 1), 0) >> pbits
    for m in range(nstate // lanes):
        ri, g0 = divmod(m * (lanes // P), Q)
        f_stack = jnp.concatenate([f_ref[g0 + t, ri * P:(ri + 1) * P, :] for t in range(lanes // P)], axis=0)
        f_big = jnp.dot(f_stack, sel_sc, preferred_element_type=F32)
        fbig_ref[0, m * lanes:(m + 1) * lanes, :] = jnp.where(
            col_group_sc == g0 + row_in_pair, f_big, 0.0).astype(BF16)

    low = iota((V7X_SUBLANES, lanes), 1) < P
    for k in range(Q // 2):
        c0, c1 = coef_ref[2 * k], coef_ref[2 * k + 1]
        row_is_a2 = iota((V7X_SUBLANES, lanes), 0) == 1
        re_half = jnp.where(low, c0, jnp.where(row_is_a2, -c1, c1))
        im_half = jnp.where(low, jnp.where(row_is_a2, -c0, c0), c1)
        cbig_ref[0, :, k * lanes:(k + 1) * lanes] = re_half
        cbig_ref[0, :, Q * P + k * lanes:Q * P + (k + 1) * lanes] = im_half


def _s5_assemble(w1, fmat, coef):
    g = w1.shape[0]
    L, J, P, Q = S5_CHUNK, S5_GROUP, S5_STATE, S5_LANE_GROUPS
    nb = g // Q
    width, nstate = L * Q * J, 2 * Q * P
    grp = lambda *s: pl.BlockSpec((Q,) + s, lambda i: (i, 0, 0))
    out = lambda *s: pl.BlockSpec((1,) + s, lambda i: (i, 0, 0))
    return pl.pallas_call(
        _s5_assemble_kernel,
        grid=(nb,),
        in_specs=[grp(L * J, L * J + 2 * P), grp(2 * P, L * J), grp(V7X_SUBLANES, 2 * P)],
        out_specs=[out(width, width + nstate), out(nstate, width), out(V7X_SUBLANES, nstate)],
        out_shape=[jax.ShapeDtypeStruct((nb, width, width + nstate), BF16),
                   jax.ShapeDtypeStruct((nb, nstate, width), BF16),
                   jax.ShapeDtypeStruct((nb, V7X_SUBLANES, nstate), F32)],
        compiler_params=_params("parallel"),
        name="s5_assemble",
    )(w1, fmat, coef)


def _s5_main_kernel(u_ref, w_ref, f_ref, coef_ref, o_ref, ys_ref, sp_ref, st_ref, *, nchunk, nb):
    width = u_ref.shape[2]
    half = st_ref.shape[1] // 2

    @pl.when(pl.program_id(1) == 0)
    def _():
        st_ref[...] = jnp.zeros_like(st_ref)

    ys_ref[...] = jnp.dot(u_ref[0], w_ref[0], preferred_element_type=F32)
    a1, a2 = coef_ref[0, 0:1, :], coef_ref[0, 1:2, :]

    def body(c, s):
        rows = pl.ds(pl.multiple_of(c * nb, nb), nb)
        sp_ref[rows, :] = s
        swapped = jnp.concatenate([s[:, half:], s[:, :half]], axis=1)
        return a1 * s + a2 * swapped + ys_ref[rows, width:]

    st_ref[...] = lax.fori_loop(0, nchunk, body, st_ref[...], unroll=4)
    y = ys_ref[:, 0:width] + jnp.dot(sp_ref[...].astype(BF16), f_ref[0], preferred_element_type=F32)
    o_ref[0] = y.astype(o_ref.dtype)


def _s5_main(u_c, wbig, fbig, cbig, *, nb, rows_per_step):
    nblk, rows, width = u_c.shape
    nstate = fbig.shape[1]
    nsteps = rows // rows_per_step
    fixed = lambda *s: pl.BlockSpec((1,) + s, lambda i, r: (i, 0, 0))
    moving = pl.BlockSpec((1, rows_per_step, width), lambda i, r: (i, r, 0))
    return pl.pallas_call(
        functools.partial(_s5_main_kernel, nchunk=rows_per_step // nb, nb=nb),
        grid=(nblk, nsteps),
        in_specs=[moving, fixed(width, width + nstate), fixed(nstate, width),
                  fixed(V7X_SUBLANES, nstate)],
        out_specs=moving,
        out_shape=jax.ShapeDtypeStruct((nblk, rows, width), BF16),
        scratch_shapes=[pltpu.VMEM((rows_per_step, width + nstate), F32),
                        pltpu.VMEM((rows_per_step, nstate), F32),
                        pltpu.VMEM((nb, nstate), F32)],
        compiler_params=_params("parallel", "arbitrary"),
        name="s5_main",
    )(u_c, wbig, fbig, cbig)


def _s5_glu_kernel(ys_ref, u_ref, d_ref, w_ref, o_ref):
    y = jax.nn.gelu(ys_ref[...].astype(F32) + d_ref[...] * u_ref[...], approximate=True)
    z = jnp.dot(y.astype(BF16), w_ref[...], preferred_element_type=F32)
    o_ref[...] = (y * jax.nn.sigmoid(z)).astype(o_ref.dtype)


def _s5_glu(y_ssm, h, d_skip, w_glu, *, tm):
    m, n = y_ssm.shape
    return pl.pallas_call(
        _s5_glu_kernel,
        grid=(m // tm,),
        in_specs=[pl.BlockSpec((tm, n), lambda i: (i, 0)),
                  pl.BlockSpec((tm, n), lambda i: (i, 0)),
                  pl.BlockSpec((1, n), lambda i: (0, 0)),
                  pl.BlockSpec((n, n), lambda i: (0, 0))],
        out_specs=pl.BlockSpec((tm, n), lambda i: (i, 0)),
        out_shape=jax.ShapeDtypeStruct((m, n), BF16),
        compiler_params=_params("parallel"),
        name="s5_glu",
    )(y_ssm, h, d_skip.reshape(1, n), w_glu)


def _pad_cols(w, n):
    return jnp.pad(w, ((0, 0), (0, n - w.shape[1])))


def _ffn_block(x_f32, x_bf, w_gate, w_up, w_down, layer, ln_g, ln_b, *, emit_bf16):
    hidden = w_gate.shape[2]
    hpad = -(-hidden // FFN_DOWN_TK) * FFN_DOWN_TK
    wd = jnp.concatenate([w_down[layer].astype(BF16), jnp.zeros((hpad - hidden, w_down.shape[2]), BF16)])
    a = _ffn_up(x_bf, w_gate, w_up, layer, tm=1024, tn=512, n_out=hpad)
    z = _matmul_resid([a], wd, x_f32, tm=1024, tn=1024, tk=FFN_DOWN_TK, name="ffn_down")
    return _layernorm(z, ln_g, ln_b, tm=LN_TM, emit_bf16=emit_bf16)


def _even_layer(x_f32, x_bf, bsz, seq, e, w_in, w_out, lam_re, lam_im, log_dt, b_re, b_im, c_re, c_im,
                d_skip, w_glu, w_alpha, b_alpha, norm_g, ln_g, ln_b):
    n = bsz * seq
    s5w = d_skip.shape[0]
    groups = s5w // S5_GROUP
    kw = GLA_HEADS * GLA_DK
    vw = GLA_HEADS * GLA_DV
    main = s5w + 2 * kw + 2 * vw
    h = _matmul(x_bf, w_in, e, n=main, tm=1024, tn=512, out_dtype=F32, name="even_in")
    alr = _matmul(x_bf, _pad_cols(w_in[e, :, main:], V7X_LANES)[None], 0, n=V7X_LANES, tm=1024,
                  tn=V7X_LANES, out_dtype=F32, name="even_in_gate")

    w1, fmat, coef = _s5_prep(lam_re, lam_im, log_dt, b_re, b_im, c_re, c_im)
    wbig, fbig, cbig = _s5_assemble(w1, fmat, coef)
    nchunk = seq // S5_CHUNK
    nblk = s5w // V7X_LANES
    u_c = (h[:, :s5w].astype(BF16).reshape(bsz, nchunk, S5_CHUNK, nblk, V7X_LANES)
           .transpose(3, 1, 0, 2, 4).reshape(nblk, nchunk * bsz, S5_CHUNK * V7X_LANES))
    y_c = _s5_main(u_c, wbig, fbig, cbig, nb=bsz, rows_per_step=256)
    y_ssm = (y_c.reshape(nblk, nchunk, bsz, S5_CHUNK, V7X_LANES)
             .transpose(2, 1, 3, 0, 4).reshape(n, s5w))
    y_s5 = _s5_glu(y_ssm, h, d_skip, w_glu.astype(BF16), tm=512)

    wa = jnp.pad(w_alpha, ((0, V7X_LANES - w_alpha.shape[0]), (0, 0))).astype(BF16)
    y_gla = _gla(h, alr, wa, b_alpha.reshape(1, kw), norm_g.reshape(1, GLA_DV), bsz=bsz, seq=seq,
                 col_q=s5w, col_k=s5w + kw, col_v=s5w + 2 * kw, col_g=s5w + 2 * kw + vw,
                 tt=256, hb=4, unroll=2)

    z = _matmul_resid([y_s5, y_gla], w_out.astype(BF16), x_f32, tm=1024, tn=1024, tk=w_out.shape[0],
                      name="even_out")
    return _layernorm(z, ln_g, ln_b, tm=LN_TM, emit_bf16=True)


def _odd_layer(x_f32, x_bf, bsz, seq, layer, o, w_in, w_out, lb_table, norm_g, ln_g, ln_b):
    heads = w_out.shape[0] // HG_DV
    h = _matmul(x_bf, w_in, o, n=w_in.shape[2], tm=1024, tn=512, out_dtype=F32, name="odd_in")
    y = _hgrn(h, lb_table, norm_g.reshape(1, HG_DV), layer=layer, bsz=bsz, seq=seq, heads=heads,
              tt=256, hb=8, unroll=2)
    z = _matmul_resid([y], w_out.astype(BF16), x_f32, tm=1024, tn=1024, tk=w_out.shape[0], name="odd_out")
    return _layernorm(z, ln_g, ln_b, tm=LN_TM, emit_bf16=True)


def kernel(x, ev_w_in, ev_w_out, s5_lam_re, s5_lam_im, s5_log_dt, s5_b_re, s5_b_im, s5_c_re, s5_c_im,
           s5_d, s5_w_glu, gla_w_alpha, gla_b_alpha, gla_norm_g, od_w_in, od_w_out, hg_lb_table,
           hg_norm_g, ln_mix_g, ln_mix_b, ln_ffn_g, ln_ffn_b, ffn_w_gate, ffn_w_up, ffn_w_down):
    bsz, seq, d = x.shape
    depth = ln_mix_g.shape[0]
    assert depth == DEPTH
    xf = x.reshape(bsz * seq, d)
    xb = xf.astype(BF16)
    for layer in range(depth):
        if layer % 2 == 0:
            e = layer // 2
            xf, xb = _even_layer(xf, xb, bsz, seq, e, ev_w_in, ev_w_out[e], s5_lam_re[e], s5_lam_im[e],
                                 s5_log_dt[e], s5_b_re[e], s5_b_im[e], s5_c_re[e], s5_c_im[e], s5_d[e],
                                 s5_w_glu[e], gla_w_alpha[e], gla_b_alpha[e], gla_norm_g[e],
                                 ln_mix_g[layer], ln_mix_b[layer])
        else:
            o = layer // 2
            xf, xb = _odd_layer(xf, xb, bsz, seq, layer, o, od_w_in, od_w_out[o], hg_lb_table,
                                hg_norm_g[o], ln_mix_g[layer], ln_mix_b[layer])
        last = layer == depth - 1
        res = _ffn_block(xf, xb, ffn_w_gate, ffn_w_up, ffn_w_down, layer,
                         ln_ffn_g[layer], ln_ffn_b[layer], emit_bf16=not last)
        xf, xb = (res[0], None) if last else res
    return xf.reshape(bsz, seq, d).astype(x.dtype)
```

```python
import functools
import math

import jax
import jax.numpy as jnp
from jax import lax
from jax.experimental import pallas as pl
from jax.experimental.pallas import tpu as pltpu

F32 = jnp.float32
BF16 = jnp.bfloat16

DEPTH = 2
S5_GROUP = 16
S5_STATE = 64
GLA_HEADS = 8
GLA_DK = 128
GLA_DV = 256
GLA_GATE_RANK = 16
GLA_TAU = 16.0
HG_DK = 128
HG_DV = 128
CHUNK = 64
DEEPNORM_ALPHA = (2.0 * DEPTH) ** 0.25
NORM_EPS = 1e-5

V7X_LANES = 128
V7X_SUBLANES = 8
V7X_VMEM_LIMIT_BYTES = 56 * 1024 * 1024
V7X_VMEM_COMPILER_TEMP_BYTES = 12 * 1024 * 1024

MM_ROW_CHUNK = 256
FFN_DOWN_TK = 2816
LN_TM = 256
LN_COL_BLOCK = 1024
LN_ROW_BLOCK = 16

S5_CHUNK = 16
S5_ROW = S5_CHUNK * S5_GROUP
S5_LANE_GROUPS = V7X_LANES // S5_GROUP


def _nbytes(shape, dtype):
    return math.prod(shape) * jnp.dtype(dtype).itemsize


def _params(*sem, buffers=None):
    limit = V7X_VMEM_LIMIT_BYTES
    if buffers is not None:
        need = sum(_nbytes(shape, dtype) * count for shape, dtype, count in buffers)
        limit = min(limit, need + V7X_VMEM_COMPILER_TEMP_BYTES)
    return pltpu.CompilerParams(dimension_semantics=sem, vmem_limit_bytes=limit)


def _stage_weight_tile(w_hbm, stage_ref, wb_ref, sem, *, layer, n_valid, nj):
    j = pl.program_id(0)
    tn = stage_ref.shape[1]
    last = nj - 1
    last_w = n_valid - last * tn

    def full_copy(jj):
        cols = pl.ds(pl.multiple_of(jj * tn, tn), tn)
        return pltpu.make_async_copy(w_hbm.at[layer, :, cols], stage_ref, sem)

    def last_copy():
        return pltpu.make_async_copy(w_hbm.at[layer, :, pl.ds(last * tn, last_w)],
                                     stage_ref.at[:, pl.ds(0, last_w)], sem)

    def start(jj):
        if last_w == tn:
            full_copy(jj).start()
        else:
            pl.when(jj < last)(lambda: full_copy(jj).start())
            pl.when(jj == last)(lambda: last_copy().start())

    @pl.when(pl.program_id(1) == 0)
    def _():
        pl.when(j == 0)(lambda: start(j))
        if last_w == tn:
            full_copy(j).wait()
            wb_ref[...] = stage_ref[...].astype(BF16)
        else:
            @pl.when(j < last)
            def _():
                full_copy(j).wait()
                wb_ref[...] = stage_ref[...].astype(BF16)

            @pl.when(j == last)
            def _():
                last_copy().wait()
                wb_ref[:, 0:last_w] = stage_ref[:, 0:last_w].astype(BF16)
                wb_ref[:, last_w:] = jnp.zeros((wb_ref.shape[0], tn - last_w), BF16)

        pl.when(j < last)(lambda: start(j + 1))


def _mm_kernel(a_ref, w_hbm, o_ref, stage_ref, wb_ref, sem, *, layer, n_valid, nj):
    _stage_weight_tile(w_hbm, stage_ref, wb_ref, sem, layer=layer, n_valid=n_valid, nj=nj)
    w = wb_ref[...]
    for r0 in range(0, a_ref.shape[0], MM_ROW_CHUNK):
        rows = slice(r0, r0 + MM_ROW_CHUNK)
        o_ref[rows, :] = jnp.dot(a_ref[rows, :], w, preferred_element_type=F32).astype(o_ref.dtype)


def _matmul(a, w, layer, *, n, tm, tn, out_dtype, name):
    m, k = a.shape
    nj = n // tn
    return pl.pallas_call(
        functools.partial(_mm_kernel, layer=layer, n_valid=n, nj=nj),
        grid=(nj, m // tm),
        in_specs=[pl.BlockSpec((tm, k), lambda j, i: (i, 0)),
                  pl.BlockSpec(memory_space=pl.ANY)],
        out_specs=pl.BlockSpec((tm, tn), lambda j, i: (i, j)),
        out_shape=jax.ShapeDtypeStruct((m, n), out_dtype),
        scratch_shapes=[pltpu.VMEM((k, tn), F32), pltpu.VMEM((k, tn), BF16), pltpu.SemaphoreType.DMA],
        compiler_params=_params("arbitrary", "arbitrary", buffers=[
            ((tm, k), BF16, 2), ((k, tn), F32, 1), ((k, tn), BF16, 1), ((tm, tn), out_dtype, 2)]),
        name=name,
    )(a, w)


def _ffn_up_kernel(x_ref, wg_hbm, wu_hbm, wd_ref, o_ref, wdb_ref, sg_ref, su_ref, wgb_ref, wub_ref, sems,
                   *, layer, n_valid, nj, wd_blocks):
    _stage_weight_tile(wg_hbm, sg_ref, wgb_ref, sems.at[0], layer=layer, n_valid=n_valid, nj=nj)
    _stage_weight_tile(wu_hbm, su_ref, wub_ref, sems.at[1], layer=layer, n_valid=n_valid, nj=nj)
    wg, wu = wgb_ref[...], wub_ref[...]
    for r0 in range(0, x_ref.shape[0], MM_ROW_CHUNK):
        rows = slice(r0, r0 + MM_ROW_CHUNK)
        x = x_ref[rows, :]
        g = jnp.dot(x, wg, preferred_element_type=F32)
        u = jnp.dot(x, wu, preferred_element_type=F32)
        o_ref[rows, :] = (g * jax.nn.sigmoid(g) * u).astype(o_ref.dtype)
    step = pl.program_id(0) * pl.num_programs(1) + pl.program_id(1)
    wdb_ref[...] = jnp.where(step < wd_blocks, wd_ref[...], 0.0).astype(wdb_ref.dtype)


def _ffn_up(x_bf, wg, wu, wd, layer, *, tm, tn, n_out):
    m, k = x_bf.shape
    hidden, dout = wd.shape[1:]
    nj, ni = n_out // tn, m // tm
    slab = n_out // (nj * ni)
    assert slab * nj * ni == n_out and hidden % slab == 0 and slab % (2 * V7X_SUBLANES) == 0
    wd_blocks = hidden // slab
    hbm = pl.BlockSpec(memory_space=pl.ANY)
    return pl.pallas_call(
        functools.partial(_ffn_up_kernel, layer=layer, n_valid=wg.shape[2], nj=nj, wd_blocks=wd_blocks),
        grid=(nj, ni),
        in_specs=[pl.BlockSpec((tm, k), lambda j, i: (i, 0)), hbm, hbm,
                  pl.BlockSpec((None, slab, dout),
                               lambda j, i: (layer, jnp.minimum(j * ni + i, wd_blocks - 1), 0))],
        out_specs=[pl.BlockSpec((tm, tn), lambda j, i: (i, j)),
                   pl.BlockSpec((slab, dout), lambda j, i: (j * ni + i, 0))],
        out_shape=[jax.ShapeDtypeStruct((m, n_out), BF16), jax.ShapeDtypeStruct((n_out, dout), BF16)],
        scratch_shapes=[pltpu.VMEM((k, tn), F32), pltpu.VMEM((k, tn), F32),
                        pltpu.VMEM((k, tn), BF16), pltpu.VMEM((k, tn), BF16),
                        pltpu.SemaphoreType.DMA((2,))],
        compiler_params=_params("arbitrary", "arbitrary", buffers=[
            ((tm, k), BF16, 2), ((k, tn), F32, 2), ((k, tn), BF16, 2), ((tm, tn), BF16, 2),
            ((slab, dout), F32, 2), ((slab, dout), BF16, 2)]),
        name="ffn_up",
    )(x_bf, wg, wu, wd)


def _mm_resid_kernel(*refs, k_parts, nk):
    n_a = len(k_parts)
    a_refs = refs[:n_a]
    w_ref, r_ref, o_ref = refs[n_a:]
    k = pl.program_id(2)

    def product():
        acc, k0 = None, 0
        for a_ref, kp in zip(a_refs, k_parts):
            d = jnp.dot(a_ref[...], w_ref[k0:k0 + kp, :], preferred_element_type=F32)
            acc = d if acc is None else acc + d
            k0 += kp
        return acc

    if nk == 1:
        o_ref[...] = product() + DEEPNORM_ALPHA * r_ref[...]
    else:
        @pl.when(k == 0)
        def _():
            o_ref[...] = product() + DEEPNORM_ALPHA * r_ref[...]

        @pl.when(k > 0)
        def _():
            o_ref[...] += product()


def _matmul_resid(a_parts, w, resid, *, tm, tn, tk, name):
    m, n = resid.shape
    kdim = w.shape[0]
    if len(a_parts) > 1:
        assert tk == kdim
        k_parts = tuple(a.shape[1] for a in a_parts)
        a_specs = [pl.BlockSpec((tm, kp), lambda i, j, k: (i, 0)) for kp in k_parts]
    else:
        k_parts = (tk,)
        a_specs = [pl.BlockSpec((tm, tk), lambda i, j, k: (i, k))]
    nk = kdim // tk
    tile = pl.BlockSpec((tm, tn), lambda i, j, k: (i, j))
    return pl.pallas_call(
        functools.partial(_mm_resid_kernel, k_parts=k_parts, nk=nk),
        grid=(m // tm, n // tn, nk),
        in_specs=a_specs + [pl.BlockSpec((tk, tn), lambda i, j, k: (k, j)), tile],
        out_specs=tile,
        out_shape=jax.ShapeDtypeStruct((m, n), F32),
        compiler_params=_params("parallel", "parallel", "arbitrary"),
        name=name,
    )(*a_parts, w, resid)


def _ln_kernel(z_ref, g_ref, b_ref, of_ref, *rest):
    maybe_ob_ref, (mu_ref, rs_ref) = rest[:-2], rest[-2:]
    inv_n = 1.0 / z_ref.shape[1]
    nblk = z_ref.shape[0] // LN_ROW_BLOCK
    lanes = mu_ref.shape[1]
    nrep = z_ref.shape[1] // lanes

    def block(r):
        return pl.ds(pl.multiple_of(r * LN_ROW_BLOCK, LN_ROW_BLOCK), LN_ROW_BLOCK)

    def mean_rows(r, carry):
        rows = block(r)
        mu = jnp.sum(z_ref[rows, :], axis=-1, keepdims=True) * inv_n
        mu_ref[rows, :] = jnp.broadcast_to(mu, (LN_ROW_BLOCK, lanes))
        return carry

    def rstd_rows(r, carry):
        rows = block(r)
        zc = z_ref[rows, :] - jnp.tile(mu_ref[rows, :], (1, nrep))
        var = jnp.sum(zc * zc, axis=-1, keepdims=True) * inv_n
        rs_ref[rows, :] = jnp.broadcast_to(lax.rsqrt(var + NORM_EPS), (LN_ROW_BLOCK, lanes))
        return carry

    def norm_rows(r, carry):
        rows = block(r)
        mu = jnp.tile(mu_ref[rows, :], (1, LN_COL_BLOCK // lanes))
        rs = jnp.tile(rs_ref[rows, :], (1, LN_COL_BLOCK // lanes))
        for c0 in range(0, z_ref.shape[1], LN_COL_BLOCK):
            cols = slice(c0, c0 + LN_COL_BLOCK)
            y = (z_ref[rows, cols] - mu) * rs * g_ref[:, cols] + b_ref[:, cols]
            of_ref[rows, cols] = y
            for ob_ref in maybe_ob_ref:
                ob_ref[rows, cols] = y.astype(ob_ref.dtype)
        return carry

    lax.fori_loop(0, nblk, mean_rows, 0, unroll=4)
    lax.fori_loop(0, nblk, rstd_rows, 0, unroll=4)
    lax.fori_loop(0, nblk, norm_rows, 0, unroll=2)


def _layernorm(z, gain, bias, *, tm, emit_bf16):
    m, n = z.shape
    row_spec = pl.BlockSpec((tm, n), lambda i: (i, 0))
    vec_spec = pl.BlockSpec((1, n), lambda i: (0, 0))
    out_shape = [jax.ShapeDtypeStruct((m, n), F32)]
    if emit_bf16:
        out_shape.append(jax.ShapeDtypeStruct((m, n), BF16))
    return pl.pallas_call(
        _ln_kernel,
        grid=(m // tm,),
        in_specs=[row_spec, vec_spec, vec_spec],
        out_specs=[row_spec] * len(out_shape),
        out_shape=out_shape,
        scratch_shapes=[pltpu.VMEM((tm, V7X_LANES), F32), pltpu.VMEM((tm, V7X_LANES), F32)],
        compiler_params=_params("parallel"),
        name="layernorm",
    )(z, gain.reshape(1, n), bias.reshape(1, n))


def _chunk_steps(qs, ks, vs, las, st_ref, tril_bf, causal):
    c = qs[0].shape[0]
    nt = (((1,), (1,)), ((), ()))
    tn = (((0,), (0,)), ((), ()))
    heads = range(len(qs))
    bs = []
    for la in las:
        la_hi = la.astype(BF16)
        la_lo = (la - la_hi.astype(F32)).astype(BF16)
        bs.append(jnp.dot(tril_bf, la_hi, preferred_element_type=F32)
                  + jnp.dot(tril_bf, la_lo, preferred_element_type=F32))
    q_decs, scores, kvs, inters, decays = [], [], [], [], []
    for hd in heads:
        b = bs[hd]
        b_end = b[c - 1:c, :]
        decay = jnp.exp(b_end)
        q_dec = (qs[hd] * jnp.exp(b)).astype(BF16)
        k_inv = ks[hd] * jnp.exp(-b)
        k_end = (k_inv * decay).astype(BF16)
        k_inv = k_inv.astype(BF16)
        v_bf = vs[hd].astype(BF16)
        vs[hd] = v_bf
        scores.append(lax.dot_general(q_dec, k_inv, nt, preferred_element_type=F32))
        kvs.append(lax.dot_general(v_bf, k_end, tn, preferred_element_type=F32))
        inters.append(lax.dot_general(q_dec, st_ref[hd].astype(BF16), nt, preferred_element_type=F32))
        decays.append(decay)
    outs = []
    for hd in heads:
        sc = jnp.where(causal, scores[hd], 0.0).astype(BF16)
        outs.append(jnp.dot(sc, vs[hd], preferred_element_type=F32) + inters[hd])
        st_ref[hd] = decays[hd] * st_ref[hd] + kvs[hd]
    return outs


def _gated_rmsnorm(o, gain, gate):
    ms = jnp.mean(o * o, axis=-1, keepdims=True)
    return o * lax.rsqrt(ms + NORM_EPS) * gain * (gate * jax.nn.sigmoid(gate))


def _chunk_masks():
    row = lax.broadcasted_iota(jnp.int32, (CHUNK, CHUNK), 0)
    col = lax.broadcasted_iota(jnp.int32, (CHUNK, CHUNK), 1)
    causal = row >= col
    return jnp.where(causal, 1.0, 0.0).astype(BF16), causal


def _log_sigmoid(z):
    return jnp.minimum(z, 0.0) - jnp.log1p(jnp.exp(-jnp.abs(z)))


def _gla_kernel(q_ref, k_ref, v_ref, g_ref, alr_ref, wa_ref, ba_ref, gain_ref, o_ref, st_ref,
                *, hb, nchunks, unroll):
    @pl.when(pl.program_id(2) == 0)
    def _():
        st_ref[...] = jnp.zeros_like(st_ref)

    tril_bf, causal = _chunk_masks()
    gain = gain_ref[...]
    ba = ba_ref[...]
    wa = wa_ref[...]

    def body(c, carry):
        rows = pl.ds(pl.multiple_of(c * CHUNK, CHUNK), CHUNK)
        z = jnp.dot(alr_ref[rows, :].astype(BF16), wa, preferred_element_type=F32) + ba
        la_all = _log_sigmoid(z) / GLA_TAU
        kcs = [slice(hd * GLA_DK, (hd + 1) * GLA_DK) for hd in range(hb)]
        vcs = [slice(hd * GLA_DV, (hd + 1) * GLA_DV) for hd in range(hb)]
        outs = _chunk_steps([q_ref[rows, kc] * (GLA_DK ** -0.5) for kc in kcs],
                            [k_ref[rows, kc] for kc in kcs], [v_ref[rows, vc] for vc in vcs],
                            [la_all[:, kc] for kc in kcs], st_ref, tril_bf, causal)
        for o, vc in zip(outs, vcs):
            o_ref[rows, vc] = _gated_rmsnorm(o, gain, g_ref[rows, vc]).astype(o_ref.dtype)
        return carry

    lax.fori_loop(0, nchunks, body, 0, unroll=unroll)


def _gla(h, alr, w_alpha, b_alpha, norm_g, *, bsz, seq, col_q, col_k, col_v, col_g, tt, hb, unroll):
    n = bsz * seq
    nt = seq // tt
    kw, vw = hb * GLA_DK, hb * GLA_DV
    hg = GLA_HEADS // hb

    def rows(b, g, t):
        return b * nt + t

    return pl.pallas_call(
        functools.partial(_gla_kernel, hb=hb, nchunks=tt // CHUNK, unroll=unroll),
        grid=(bsz, hg, nt),
        in_specs=[pl.BlockSpec((tt, kw), lambda b, g, t: (rows(b, g, t), col_q // kw + g)),
                  pl.BlockSpec((tt, kw), lambda b, g, t: (rows(b, g, t), col_k // kw + g)),
                  pl.BlockSpec((tt, vw), lambda b, g, t: (rows(b, g, t), col_v // vw + g)),
                  pl.BlockSpec((tt, vw), lambda b, g, t: (rows(b, g, t), col_g // vw + g)),
                  pl.BlockSpec((tt, V7X_LANES), lambda b, g, t: (rows(b, g, t), 0)),
                  pl.BlockSpec((V7X_LANES, kw), lambda b, g, t: (0, g)),
                  pl.BlockSpec((1, kw), lambda b, g, t: (0, g)),
                  pl.BlockSpec((1, GLA_DV), lambda b, g, t: (0, 0))],
        out_specs=pl.BlockSpec((tt, vw), lambda b, g, t: (rows(b, g, t), g)),
        out_shape=jax.ShapeDtypeStruct((n, GLA_HEADS * GLA_DV), BF16),
        scratch_shapes=[pltpu.VMEM((hb, GLA_DV, GLA_DK), F32)],
        compiler_params=_params("parallel", "parallel", "arbitrary"),
        name="gla",
    )(h, h, h, h, alr, w_alpha, b_alpha, norm_g)


def _hgrn_kernel(q_ref, f_ref, i_ref, g_ref, tab_ref, gain_ref, o_ref, st_ref, *, layer, hb, nchunks,
                 unroll):
    @pl.when(pl.program_id(2) == 0)
    def _():
        st_ref[...] = jnp.zeros_like(st_ref)

    tab = tab_ref[...]
    e = jnp.exp(tab - jnp.max(tab, axis=0, keepdims=True))
    soft = e / jnp.sum(e, axis=0, keepdims=True)
    lb_all = jnp.sum(soft[0:layer + 1], axis=0, keepdims=True) - soft[0:1]
    tril_bf, causal = _chunk_masks()
    gain = gain_ref[...]

    def body(c, carry):
        rows = pl.ds(pl.multiple_of(c * CHUNK, CHUNK), CHUNK)
        kcs = [slice(hd * HG_DK, (hd + 1) * HG_DK) for hd in range(hb)]
        vcs = [slice(hd * HG_DV, (hd + 1) * HG_DV) for hd in range(hb)]
        fs = [lb_all[:, kc] + (1.0 - lb_all[:, kc]) * jax.nn.sigmoid(f_ref[rows, kc]) for kc in kcs]
        ivs = [i_ref[rows, vc] for vc in vcs]
        outs = _chunk_steps([q_ref[rows, kc] for kc in kcs], [1.0 - f for f in fs],
                            [iv * jax.nn.sigmoid(iv) for iv in ivs], [jnp.log(f) for f in fs],
                            st_ref, tril_bf, causal)
        for o, vc in zip(outs, vcs):
            o_ref[rows, vc] = _gated_rmsnorm(o, gain, g_ref[rows, vc]).astype(o_ref.dtype)
        return carry

    lax.fori_loop(0, nchunks, body, 0, unroll=unroll)


def _hgrn(h, lb_table, norm_g, *, layer, bsz, seq, heads, tt, hb, unroll):
    n = bsz * seq
    nt = seq // tt
    w = hb * HG_DK
    hg = heads // hb

    def spec(seg):
        return pl.BlockSpec((tt, w), lambda b, g, t: (b * nt + t, seg * hg + g))

    return pl.pallas_call(
        functools.partial(_hgrn_kernel, layer=layer, hb=hb, nchunks=tt // CHUNK, unroll=unroll),
        grid=(bsz, hg, nt),
        in_specs=[spec(0), spec(1), spec(2), spec(3),
                  pl.BlockSpec((DEPTH, w), lambda b, g, t: (0, g)),
                  pl.BlockSpec((1, HG_DV), lambda b, g, t: (0, 0))],
        out_specs=pl.BlockSpec((tt, w), lambda b, g, t: (b * nt + t, g)),
        out_shape=jax.ShapeDtypeStruct((n, heads * HG_DV), BF16),
        scratch_shapes=[pltpu.VMEM((hb, HG_DV, HG_DK), F32)],
        compiler_params=_params("parallel", "parallel", "arbitrary"),
        name="hgrn2",
    )(h, h, h, h, lb_table, norm_g)


def _s5_prep_kernel(lr_row_ref, li_row_ref, lr_col_ref, li_col_ref, ldt_ref,
                    btr_ref, bti_ref, ctr_ref, cti_ref, w1_ref, f_ref, coef_ref):
    L, J, P = S5_CHUNK, S5_GROUP, S5_STATE
    hi = lax.Precision.HIGHEST
    dt = jnp.exp(ldt_ref[0])
    lr, li = lr_row_ref[0], li_row_ref[0]
    npow = ((L - 1) - lax.broadcasted_iota(jnp.int32, (L, 2 * P), 0)).astype(F32)
    mag = jnp.exp(npow * (lr * dt))
    th = npow * (li * dt)
    ar, ai = mag * jnp.cos(th), mag * jnp.sin(th)
    half = lax.broadcasted_iota(jnp.int32, (L, 2 * P), 1) < P
    a1r, a1i = ar[L - 2:L - 1], ai[L - 2:L - 1]
    nr, ni = a1r - 1.0, a1i
    den = lr * lr + li * li
    fr, fi = (nr * lr + ni * li) / den, (ni * lr - nr * li) / den
    btr, bti = btr_ref[0], bti_ref[0]
    bbr, bbi = fr * btr - fi * bti, fr * bti + fi * btr
    pa, pb = jnp.where(half, ar, ai), jnp.where(half, -ai, ar)
    for s in range(L):
        rows = slice(s * J, (s + 1) * J)
        w1_ref[0, rows, L * J:] = (bbr * pa[s:s + 1] + bbi * pb[s:s + 1]).astype(BF16)
    alr_, ali_ = ar[0:1] * a1r - ai[0:1] * a1i, ar[0:1] * a1i + ai[0:1] * a1r
    half1 = half[0:1]
    coef_ref[0, 0:1, :] = alr_
    coef_ref[0, 1:2, :] = jnp.where(half1, -ali_, ali_)
    coef_ref[0, 2:, :] = jnp.zeros((V7X_SUBLANES - 2, 2 * P), F32)

    lrc, lic = lr_col_ref[0], li_col_ref[0]
    lane = lax.broadcasted_iota(jnp.int32, (P, L * J), 1)
    lag = jnp.right_shift(lane, int(math.log2(J))).astype(F32)
    magq = jnp.exp(lag * (lrc * dt))
    thq = lag * (lic * dt)
    aqr, aqi = magq * jnp.cos(thq), magq * jnp.sin(thq)
    cr, ci = ctr_ref[0], cti_ref[0]
    qr, qi = cr * aqr - ci * aqi, cr * aqi + ci * aqr
    r = (jnp.dot(bbr[:, :P], qr, precision=hi, preferred_element_type=F32)
         - jnp.dot(bbi[:, :P], qi, precision=hi, preferred_element_type=F32))
    lane_r = lax.broadcasted_iota(jnp.int32, (J, L * J), 1)
    for s in range(L):
        blk = r if s == 0 else jnp.where(lane_r >= s * J, pltpu.roll(r, s * J, 1), 0.0)
        w1_ref[0, s * J:(s + 1) * J, 0:L * J] = blk.astype(BF16)
    m1 = jnp.exp(lrc * dt)
    c1r, c1i = m1 * jnp.cos(lic * dt), m1 * jnp.sin(lic * dt)
    f_ref[0, 0:P, :] = (qr * c1r - qi * c1i).astype(BF16)
    f_ref[0, P:, :] = (-(qr * c1i + qi * c1r)).astype(BF16)


def _s5_prep(lam_re, lam_im, log_dt, b_re, b_im, c_re, c_im):
    g = lam_re.shape[0]
    P, J, L = S5_STATE, S5_GROUP, S5_CHUNK
    dbl = lambda a: jnp.concatenate([a, a], axis=-1)
    lr_row = dbl(lam_re).reshape(g, 1, 2 * P)
    li_row = dbl(lam_im).reshape(g, 1, 2 * P)
    lr_col = lam_re.reshape(g, P, 1)
    li_col = lam_im.reshape(g, P, 1)
    ldt = log_dt.reshape(g, 1, 1)
    btr = dbl(jnp.swapaxes(b_re, 1, 2))
    bti = dbl(jnp.swapaxes(b_im, 1, 2))
    ctr = jnp.tile(jnp.swapaxes(c_re, 1, 2), (1, 1, L))
    cti = jnp.tile(jnp.swapaxes(c_im, 1, 2), (1, 1, L))
    blk = lambda *s: pl.BlockSpec((1,) + s, lambda i: (i, 0, 0))
    return pl.pallas_call(
        _s5_prep_kernel,
        grid=(g,),
        in_specs=[blk(1, 2 * P), blk(1, 2 * P), blk(P, 1), blk(P, 1), blk(1, 1),
                  blk(J, 2 * P), blk(J, 2 * P), blk(P, L * J), blk(P, L * J)],
        out_specs=[blk(L * J, L * J + 2 * P), blk(2 * P, L * J), blk(V7X_SUBLANES, 2 * P)],
        out_shape=[jax.ShapeDtypeStruct((g, L * J, L * J + 2 * P), BF16),
                   jax.ShapeDtypeStruct((g, 2 * P, L * J), BF16),
                   jax.ShapeDtypeStruct((g, V7X_SUBLANES, 2 * P), F32)],
        compiler_params=_params("parallel"),
        name="s5_prep",
    )(lr_row, li_row, lr_col, li_col, ldt, btr, bti, ctr, cti)


def _s5_assemble_kernel(w1_ref, f_ref, coef_ref, wbig_ref, fbig_ref, cbig_ref):
    L, J, P, Q = S5_CHUNK, S5_GROUP, S5_STATE, S5_LANE_GROUPS
    lanes = Q * J
    width = L * lanes
    nstate = 2 * Q * P
    jbits, pbits = int(math.log2(J)), int(math.log2(P))

    def iota(shape, axis):
        return lax.broadcasted_iota(jnp.int32, shape, axis)

    a, b = iota((L * J, width), 0), iota((L * J, width), 1)
    sel_sc = jnp.where(((b >> int(math.log2(lanes))) == (a >> jbits)) & ((b & (J - 1)) == (a & (J - 1))),
                       1.0, 0.0).astype(BF16)
    a, b = iota((2 * P, nstate), 0), iota((2 * P, nstate), 1)
    sel_st = jnp.where(((b >> int(math.log2(Q * P))) == (a >> pbits)) & ((b & (P - 1)) == (a & (P - 1))),
                       1.0, 0.0).astype(BF16)
    col_group_sc = (iota((lanes, width), 1) >> jbits) & (Q - 1)
    col_group_st = (iota((lanes, nstate), 1) >> pbits) & (Q - 1)
    row_group_j = iota((lanes, 1), 0) >> jbits

    r_stack = jnp.concatenate([w1_ref[g, 0:J, 0:L * J] for g in range(Q)], axis=0)
    r_big = jnp.dot(r_stack, sel_sc, preferred_element_type=F32)
    r_big = jnp.where(col_group_sc == row_group_j, r_big, 0.0).astype(BF16)
    for s in range(L):
        rows = slice(s * lanes, (s + 1) * lanes)
        if s:
            wbig_ref[0, rows, 0:s * lanes] = jnp.zeros((lanes, s * lanes), BF16)
        wbig_ref[0, rows, s * lanes:width] = r_big[:, 0:width - s * lanes]
        e_stack = jnp.concatenate([w1_ref[g, s * J:(s + 1) * J, L * J:] for g in range(Q)], axis=0)
        e_big = jnp.dot(e_stack, sel_st, preferred_element_type=F32)
        wbig_ref[0, rows, width:] = jnp.where(col_group_st == row_group_j, e_big, 0.0).astype(BF16)

    row_in_pair = iota((lanes, 1), 0) >> pbits
    for m in range(nstate // lanes):
        ri, g0 = divmod(m * (lanes // P), Q)
        f_stack = jnp.concatenate([f_ref[g0 + t, ri * P:(ri + 1) * P, :] for t in range(lanes // P)], axis=0)
        f_big = jnp.dot(f_stack, sel_sc, preferred_element_type=F32)
        fbig_ref[0, m * lanes:(m + 1) * lanes, :] = jnp.where(
            col_group_sc == g0 + row_in_pair, f_big, 0.0).astype(BF16)

    low = iota((V7X_SUBLANES, lanes), 1) < P
    for k in range(Q // 2):
        c0, c1 = coef_ref[2 * k], coef_ref[2 * k + 1]
        row_is_a2 = iota((V7X_SUBLANES, lanes), 0) == 1
        re_half = jnp.where(low, c0, jnp.where(row_is_a2, -c1, c1))
        im_half = jnp.where(low, jnp.where(row_is_a2, -c0, c0), c1)
        cbig_ref[0, :, k * lanes:(k + 1) * lanes] = re_half
        cbig_ref[0, :, Q * P + k * lanes:Q * P + (k + 1) * lanes] = im_half


def _s5_assemble(w1, fmat, coef):
    g = w1.shape[0]
    L, J, P, Q = S5_CHUNK, S5_GROUP, S5_STATE, S5_LANE_GROUPS
    nb = g // Q
    width, nstate = L * Q * J, 2 * Q * P
    grp = lambda *s: pl.BlockSpec((Q,) + s, lambda i: (i, 0, 0))
    out = lambda *s: pl.BlockSpec((1,) + s, lambda i: (i, 0, 0))
    return pl.pallas_call(
        _s5_assemble_kernel,
        grid=(nb,),
        in_specs=[grp(L * J, L * J + 2 * P), grp(2 * P, L * J), grp(V7X_SUBLANES, 2 * P)],
        out_specs=[out(width, width + nstate), out(nstate, width), out(V7X_SUBLANES, nstate)],
        out_shape=[jax.ShapeDtypeStruct((nb, width, width + nstate), BF16),
                   jax.ShapeDtypeStruct((nb, nstate, width), BF16),
                   jax.ShapeDtypeStruct((nb, V7X_SUBLANES, nstate), F32)],
        compiler_params=_params("parallel"),
        name="s5_assemble",
    )(w1, fmat, coef)


def _s5_main_kernel(u_ref, w_ref, f_ref, coef_ref, o_ref, ys_ref, sp_ref, st_ref, *, nchunk, nb):
    width = u_ref.shape[2]
    half = st_ref.shape[1] // 2

    @pl.when(pl.program_id(1) == 0)
    def _():
        st_ref[...] = jnp.zeros_like(st_ref)

    ys_ref[...] = jnp.dot(u_ref[0], w_ref[0], preferred_element_type=F32)
    a1, a2 = coef_ref[0, 0:1, :], coef_ref[0, 1:2, :]

    def body(c, s):
        rows = pl.ds(pl.multiple_of(c * nb, nb), nb)
        sp_ref[rows, :] = s
        swapped = jnp.concatenate([s[:, half:], s[:, :half]], axis=1)
        return a1 * s + a2 * swapped + ys_ref[rows, width:]

    st_ref[...] = lax.fori_loop(0, nchunk, body, st_ref[...], unroll=4)
    y = ys_ref[:, 0:width] + jnp.dot(sp_ref[...].astype(BF16), f_ref[0], preferred_element_type=F32)
    o_ref[0] = y.astype(o_ref.dtype)


def _s5_main(u_c, wbig, fbig, cbig, *, nb, rows_per_step):
    nblk, rows, width = u_c.shape
    nstate = fbig.shape[1]
    nsteps = rows // rows_per_step
    fixed = lambda *s: pl.BlockSpec((1,) + s, lambda i, r: (i, 0, 0))
    moving = pl.BlockSpec((1, rows_per_step, width), lambda i, r: (i, r, 0))
    return pl.pallas_call(
        functools.partial(_s5_main_kernel, nchunk=rows_per_step // nb, nb=nb),
        grid=(nblk, nsteps),
        in_specs=[moving, fixed(width, width + nstate), fixed(nstate, width),
                  fixed(V7X_SUBLANES, nstate)],
        out_specs=moving,
        out_shape=jax.ShapeDtypeStruct((nblk, rows, width), BF16),
        scratch_shapes=[pltpu.VMEM((rows_per_step, width + nstate), F32),
                        pltpu.VMEM((rows_per_step, nstate), F32),
                        pltpu.VMEM((nb, nstate), F32)],
        compiler_params=_params("parallel", "arbitrary"),
        name="s5_main",
    )(u_c, wbig, fbig, cbig)


def _s5_glu_kernel(ys_ref, u_ref, d_ref, w_ref, o_ref):
    y = jax.nn.gelu(ys_ref[...].astype(F32) + d_ref[...] * u_ref[...], approximate=True)
    z = jnp.dot(y.astype(BF16), w_ref[...], preferred_element_type=F32)
    o_ref[...] = (y * jax.nn.sigmoid(z)).astype(o_ref.dtype)


def _s5_glu(y_ssm, h, d_skip, w_glu, *, tm):
    m, n = y_ssm.shape
    return pl.pallas_call(
        _s5_glu_kernel,
        grid=(m // tm,),
        in_specs=[pl.BlockSpec((tm, n), lambda i: (i, 0)),
                  pl.BlockSpec((tm, n), lambda i: (i, 0)),
                  pl.BlockSpec((1, n), lambda i: (0, 0)),
                  pl.BlockSpec((n, n), lambda i: (0, 0))],
        out_specs=pl.BlockSpec((tm, n), lambda i: (i, 0)),
        out_shape=jax.ShapeDtypeStruct((m, n), BF16),
        compiler_params=_params("parallel"),
        name="s5_glu",
    )(y_ssm, h, d_skip.reshape(1, n), w_glu)


def _pad_cols(w, n):
    return jnp.pad(w, ((0, 0), (0, n - w.shape[1])))


def _ffn_block(x_f32, x_bf, w_gate, w_up, w_down, layer, ln_g, ln_b, *, emit_bf16):
    hidden = w_gate.shape[2]
    hpad = -(-hidden // FFN_DOWN_TK) * FFN_DOWN_TK
    a, wd = _ffn_up(x_bf, w_gate, w_up, w_down, layer, tm=1024, tn=512, n_out=hpad)
    z = _matmul_resid([a], wd, x_f32, tm=1024, tn=1024, tk=FFN_DOWN_TK, name="ffn_down")
    return _layernorm(z, ln_g, ln_b, tm=LN_TM, emit_bf16=emit_bf16)


def _even_layer(x_f32, x_bf, bsz, seq, e, w_in, w_out, lam_re, lam_im, log_dt, b_re, b_im, c_re, c_im,
                d_skip, w_glu, w_alpha, b_alpha, norm_g, ln_g, ln_b):
    n = bsz * seq
    s5w = d_skip.shape[0]
    groups = s5w // S5_GROUP
    kw = GLA_HEADS * GLA_DK
    vw = GLA_HEADS * GLA_DV
    main = s5w + 2 * kw + 2 * vw
    h = _matmul(x_bf, w_in, e, n=main, tm=1024, tn=512, out_dtype=F32, name="even_in")
    alr = _matmul(x_bf, _pad_cols(w_in[e, :, main:], V7X_LANES)[None], 0, n=V7X_LANES, tm=1024,
                  tn=V7X_LANES, out_dtype=F32, name="even_in_gate")

    w1, fmat, coef = _s5_prep(lam_re, lam_im, log_dt, b_re, b_im, c_re, c_im)
    wbig, fbig, cbig = _s5_assemble(w1, fmat, coef)
    nchunk = seq // S5_CHUNK
    nblk = s5w // V7X_LANES
    u_c = (h[:, :s5w].astype(BF16).reshape(bsz, nchunk, S5_CHUNK, nblk, V7X_LANES)
           .transpose(3, 1, 0, 2, 4).reshape(nblk, nchunk * bsz, S5_CHUNK * V7X_LANES))
    y_c = _s5_main(u_c, wbig, fbig, cbig, nb=bsz, rows_per_step=256)
    y_ssm = (y_c.reshape(nblk, nchunk, bsz, S5_CHUNK, V7X_LANES)
             .transpose(2, 1, 3, 0, 4).reshape(n, s5w))
    y_s5 = _s5_glu(y_ssm, h, d_skip, w_glu.astype(BF16), tm=512)

    wa = jnp.pad(w_alpha, ((0, V7X_LANES - w_alpha.shape[0]), (0, 0))).astype(BF16)
    y_gla = _gla(h, alr, wa, b_alpha.reshape(1, kw), norm_g.reshape(1, GLA_DV), bsz=bsz, seq=seq,
                 col_q=s5w, col_k=s5w + kw, col_v=s5w + 2 * kw, col_g=s5w + 2 * kw + vw,
                 tt=512, hb=4, unroll=2)

    z = _matmul_resid([y_s5, y_gla], w_out.astype(BF16), x_f32, tm=1024, tn=1024, tk=w_out.shape[0],
                      name="even_out")
    return _layernorm(z, ln_g, ln_b, tm=LN_TM, emit_bf16=True)


def _odd_layer(x_f32, x_bf, bsz, seq, layer, o, w_in, w_out, lb_table, norm_g, ln_g, ln_b):
    heads = w_out.shape[0] // HG_DV
    h = _matmul(x_bf, w_in, o, n=w_in.shape[2], tm=1024, tn=512, out_dtype=F32, name="odd_in")
    y = _hgrn(h, lb_table, norm_g.reshape(1, HG_DV), layer=layer, bsz=bsz, seq=seq, heads=heads,
              tt=512, hb=8, unroll=2)
    z = _matmul_resid([y], w_out.astype(BF16), x_f32, tm=1024, tn=1024, tk=w_out.shape[0], name="odd_out")
    return _layernorm(z, ln_g, ln_b, tm=LN_TM, emit_bf16=True)


def kernel(x, ev_w_in, ev_w_out, s5_lam_re, s5_lam_im, s5_log_dt, s5_b_re, s5_b_im, s5_c_re, s5_c_im,
           s5_d, s5_w_glu, gla_w_alpha, gla_b_alpha, gla_norm_g, od_w_in, od_w_out, hg_lb_table,
           hg_norm_g, ln_mix_g, ln_mix_b, ln_ffn_g, ln_ffn_b, ffn_w_gate, ffn_w_up, ffn_w_down):
    bsz, seq, d = x.shape
    depth = ln_mix_g.shape[0]
    assert depth == DEPTH
    xf = x.reshape(bsz * seq, d)
    xb = xf.astype(BF16)
    for layer in range(depth):
        if layer % 2 == 0:
            e = layer // 2
            xf, xb = _even_layer(xf, xb, bsz, seq, e, ev_w_in, ev_w_out[e], s5_lam_re[e], s5_lam_im[e],
                                 s5_log_dt[e], s5_b_re[e], s5_b_im[e], s5_c_re[e], s5_c_im[e], s5_d[e],
                                 s5_w_glu[e], gla_w_alpha[e], gla_b_alpha[e], gla_norm_g[e],
                                 ln_mix_g[layer], ln_mix_b[layer])
        else:
            o = layer // 2
            xf, xb = _odd_layer(xf, xb, bsz, seq, layer, o, od_w_in, od_w_out[o], hg_lb_table,
                                hg_norm_g[o], ln_mix_g[layer], ln_mix_b[layer])
        last = layer == depth - 1
        res = _ffn_block(xf, xb, ffn_w_gate, ffn_w_up, ffn_w_down, layer,
                         ln_ffn_g[layer], ln_ffn_b[layer], emit_bf16=not last)
        xf, xb = (res[0], None) if last else res
    return xf.reshape(bsz, seq, d).astype(x.dtype)
```

```python
import functools
import math

import jax
import jax.numpy as jnp
from jax import lax
from jax.experimental import pallas as pl
from jax.experimental.pallas import tpu as pltpu

F32 = jnp.float32
BF16 = jnp.bfloat16

DEPTH = 2
S5_GROUP = 16
S5_STATE = 64
GLA_HEADS = 8
GLA_DK = 128
GLA_DV = 256
GLA_GATE_RANK = 16
GLA_TAU = 16.0
HG_DK = 128
HG_DV = 128
CHUNK = 64
DEEPNORM_ALPHA = (2.0 * DEPTH) ** 0.25
NORM_EPS = 1e-5

V7X_LANES = 128
V7X_SUBLANES = 8
V7X_VMEM_LIMIT_BYTES = 60 * 1024 * 1024
V7X_VMEM_COMPILER_TEMP_BYTES = 12 * 1024 * 1024

MM_ROW_CHUNK = 256
IN_PROJ_TN = 1024
FFN_DOWN_TK = 2816
LN_TM = 256
LN_COL_BLOCK = 1024
LN_ROW_BLOCK = 16

S5_CHUNK = 16
S5_ROW = S5_CHUNK * S5_GROUP
S5_LANE_GROUPS = V7X_LANES // S5_GROUP


def _nbytes(shape, dtype):
    return math.prod(shape) * jnp.dtype(dtype).itemsize


def _params(*sem, buffers=None):
    limit = V7X_VMEM_LIMIT_BYTES
    if buffers is not None:
        need = sum(_nbytes(shape, dtype) * count for shape, dtype, count in buffers)
        limit = min(limit, need + V7X_VMEM_COMPILER_TEMP_BYTES)
    return pltpu.CompilerParams(dimension_semantics=sem, vmem_limit_bytes=limit)


def _stage_weight_tile(w_hbm, stage_ref, wb_ref, sem, *, layer, n_valid, nj):
    j = pl.program_id(0)
    tn = stage_ref.shape[1]
    last = nj - 1
    last_w = n_valid - last * tn

    def full_copy(jj):
        cols = pl.ds(pl.multiple_of(jj * tn, tn), tn)
        return pltpu.make_async_copy(w_hbm.at[layer, :, cols], stage_ref, sem)

    def last_copy():
        return pltpu.make_async_copy(w_hbm.at[layer, :, pl.ds(last * tn, last_w)],
                                     stage_ref.at[:, pl.ds(0, last_w)], sem)

    def start(jj):
        if last_w == tn:
            full_copy(jj).start()
        else:
            pl.when(jj < last)(lambda: full_copy(jj).start())
            pl.when(jj == last)(lambda: last_copy().start())

    @pl.when(pl.program_id(1) == 0)
    def _():
        pl.when(j == 0)(lambda: start(j))
        if last_w == tn:
            full_copy(j).wait()
            wb_ref[...] = stage_ref[...].astype(BF16)
        else:
            @pl.when(j < last)
            def _():
                full_copy(j).wait()
                wb_ref[...] = stage_ref[...].astype(BF16)

            @pl.when(j == last)
            def _():
                last_copy().wait()
                wb_ref[:, 0:last_w] = stage_ref[:, 0:last_w].astype(BF16)
                wb_ref[:, last_w:] = jnp.zeros((wb_ref.shape[0], tn - last_w), BF16)

        pl.when(j < last)(lambda: start(j + 1))


def _mm_kernel(a_ref, w_hbm, o_ref, stage_ref, wb_ref, sem, *, layer, n_valid, nj):
    _stage_weight_tile(w_hbm, stage_ref, wb_ref, sem, layer=layer, n_valid=n_valid, nj=nj)
    w = wb_ref[...]
    for r0 in range(0, a_ref.shape[0], MM_ROW_CHUNK):
        rows = slice(r0, r0 + MM_ROW_CHUNK)
        o_ref[rows, :] = jnp.dot(a_ref[rows, :], w, preferred_element_type=F32).astype(o_ref.dtype)


def _matmul(a, w, layer, *, n, tm, tn, out_dtype, name):
    m, k = a.shape
    nj = n // tn
    return pl.pallas_call(
        functools.partial(_mm_kernel, layer=layer, n_valid=n, nj=nj),
        grid=(nj, m // tm),
        in_specs=[pl.BlockSpec((tm, k), lambda j, i: (i, 0)),
                  pl.BlockSpec(memory_space=pl.ANY)],
        out_specs=pl.BlockSpec((tm, tn), lambda j, i: (i, j)),
        out_shape=jax.ShapeDtypeStruct((m, n), out_dtype),
        scratch_shapes=[pltpu.VMEM((k, tn), F32), pltpu.VMEM((k, tn), BF16), pltpu.SemaphoreType.DMA],
        compiler_params=_params("arbitrary", "arbitrary", buffers=[
            ((tm, k), BF16, 2), ((k, tn), F32, 1), ((k, tn), BF16, 1), ((tm, tn), out_dtype, 2)]),
        name=name,
    )(a, w)


def _ffn_up_kernel(x_ref, wg_hbm, wu_hbm, wd_ref, o_ref, wdb_ref, sg_ref, su_ref, wgb_ref, wub_ref, sems,
                   *, layer, n_valid, nj, wd_blocks):
    _stage_weight_tile(wg_hbm, sg_ref, wgb_ref, sems.at[0], layer=layer, n_valid=n_valid, nj=nj)
    _stage_weight_tile(wu_hbm, su_ref, wub_ref, sems.at[1], layer=layer, n_valid=n_valid, nj=nj)
    tn = o_ref.shape[1]
    last = nj - 1
    last_w = n_valid - last * tn

    def tile(width):
        wg, wu = wgb_ref[:, 0:width], wub_ref[:, 0:width]
        for r0 in range(0, x_ref.shape[0], MM_ROW_CHUNK):
            rows = slice(r0, r0 + MM_ROW_CHUNK)
            x = x_ref[rows, :]
            g = jnp.dot(x, wg, preferred_element_type=F32)
            u = jnp.dot(x, wu, preferred_element_type=F32)
            o_ref[rows, 0:width] = (g * jax.nn.sigmoid(g) * u).astype(o_ref.dtype)
            if width < tn:
                o_ref[rows, width:] = jnp.zeros((MM_ROW_CHUNK, tn - width), o_ref.dtype)

    if last_w == tn:
        tile(tn)
    else:
        pl.when(pl.program_id(0) < last)(lambda: tile(tn))
        pl.when(pl.program_id(0) == last)(lambda: tile(last_w))
    step = pl.program_id(0) * pl.num_programs(1) + pl.program_id(1)
    wdb_ref[...] = jnp.where(step < wd_blocks, wd_ref[...], 0.0).astype(wdb_ref.dtype)


def _ffn_up(x_bf, wg, wu, wd, layer, *, tm, tn, n_out):
    m, k = x_bf.shape
    hidden, dout = wd.shape[1:]
    nj, ni = n_out // tn, m // tm
    slab = n_out // (nj * ni)
    assert slab * nj * ni == n_out and hidden % slab == 0 and slab % (2 * V7X_SUBLANES) == 0
    wd_blocks = hidden // slab
    hbm = pl.BlockSpec(memory_space=pl.ANY)
    return pl.pallas_call(
        functools.partial(_ffn_up_kernel, layer=layer, n_valid=wg.shape[2], nj=nj, wd_blocks=wd_blocks),
        grid=(nj, ni),
        in_specs=[pl.BlockSpec((tm, k), lambda j, i: (i, 0)), hbm, hbm,
                  pl.BlockSpec((None, slab, dout),
                               lambda j, i: (layer, jnp.minimum(j * ni + i, wd_blocks - 1), 0))],
        out_specs=[pl.BlockSpec((tm, tn), lambda j, i: (i, j)),
                   pl.BlockSpec((slab, dout), lambda j, i: (j * ni + i, 0))],
        out_shape=[jax.ShapeDtypeStruct((m, n_out), BF16), jax.ShapeDtypeStruct((n_out, dout), BF16)],
        scratch_shapes=[pltpu.VMEM((k, tn), F32), pltpu.VMEM((k, tn), F32),
                        pltpu.VMEM((k, tn), BF16), pltpu.VMEM((k, tn), BF16),
                        pltpu.SemaphoreType.DMA((2,))],
        compiler_params=_params("arbitrary", "arbitrary", buffers=[
            ((tm, k), BF16, 2), ((k, tn), F32, 2), ((k, tn), BF16, 2), ((tm, tn), BF16, 2),
            ((slab, dout), F32, 2), ((slab, dout), BF16, 2)]),
        name="ffn_up",
    )(x_bf, wg, wu, wd)


def _mm_resid_kernel(*refs, k_parts, nk):
    n_a = len(k_parts)
    a_refs = refs[:n_a]
    w_ref, r_ref, o_ref = refs[n_a:]
    k = pl.program_id(2)

    def product():
        acc, k0 = None, 0
        for a_ref, kp in zip(a_refs, k_parts):
            d = jnp.dot(a_ref[...], w_ref[k0:k0 + kp, :], preferred_element_type=F32)
            acc = d if acc is None else acc + d
            k0 += kp
        return acc

    if nk == 1:
        o_ref[...] = product() + DEEPNORM_ALPHA * r_ref[...]
    else:
        @pl.when(k == 0)
        def _():
            o_ref[...] = product() + DEEPNORM_ALPHA * r_ref[...]

        @pl.when(k > 0)
        def _():
            o_ref[...] += product()


def _matmul_resid(a_parts, w, resid, *, tm, tn, tk, name):
    m, n = resid.shape
    kdim = w.shape[0]
    if len(a_parts) > 1:
        assert tk == kdim
        k_parts = tuple(a.shape[1] for a in a_parts)
        a_specs = [pl.BlockSpec((tm, kp), lambda i, j, k: (i, 0)) for kp in k_parts]
    else:
        k_parts = (tk,)
        a_specs = [pl.BlockSpec((tm, tk), lambda i, j, k: (i, k))]
    nk = kdim // tk
    tile = pl.BlockSpec((tm, tn), lambda i, j, k: (i, j))
    return pl.pallas_call(
        functools.partial(_mm_resid_kernel, k_parts=k_parts, nk=nk),
        grid=(m // tm, n // tn, nk),
        in_specs=a_specs + [pl.BlockSpec((tk, tn), lambda i, j, k: (k, j)), tile],
        out_specs=tile,
        out_shape=jax.ShapeDtypeStruct((m, n), F32),
        compiler_params=_params("parallel", "parallel", "arbitrary"),
        name=name,
    )(*a_parts, w, resid)


def _ln_kernel(z_ref, g_ref, b_ref, of_ref, *rest):
    maybe_ob_ref, (mu_ref, rs_ref) = rest[:-2], rest[-2:]
    inv_n = 1.0 / z_ref.shape[1]
    nblk = z_ref.shape[0] // LN_ROW_BLOCK
    lanes = mu_ref.shape[1]
    nrep = z_ref.shape[1] // lanes

    def block(r):
        return pl.ds(pl.multiple_of(r * LN_ROW_BLOCK, LN_ROW_BLOCK), LN_ROW_BLOCK)

    def mean_rows(r, carry):
        rows = block(r)
        mu = jnp.sum(z_ref[rows, :], axis=-1, keepdims=True) * inv_n
        mu_ref[rows, :] = jnp.broadcast_to(mu, (LN_ROW_BLOCK, lanes))
        return carry

    def rstd_rows(r, carry):
        rows = block(r)
        zc = z_ref[rows, :] - jnp.tile(mu_ref[rows, :], (1, nrep))
        var = jnp.sum(zc * zc, axis=-1, keepdims=True) * inv_n
        rs_ref[rows, :] = jnp.broadcast_to(lax.rsqrt(var + NORM_EPS), (LN_ROW_BLOCK, lanes))
        return carry

    def norm_rows(r, carry):
        rows = block(r)
        mu = jnp.tile(mu_ref[rows, :], (1, LN_COL_BLOCK // lanes))
        rs = jnp.tile(rs_ref[rows, :], (1, LN_COL_BLOCK // lanes))
        for c0 in range(0, z_ref.shape[1], LN_COL_BLOCK):
            cols = slice(c0, c0 + LN_COL_BLOCK)
            y = (z_ref[rows, cols] - mu) * rs * g_ref[:, cols] + b_ref[:, cols]
            of_ref[rows, cols] = y
            for ob_ref in maybe_ob_ref:
                ob_ref[rows, cols] = y.astype(ob_ref.dtype)
        return carry

    lax.fori_loop(0, nblk, mean_rows, 0, unroll=4)
    lax.fori_loop(0, nblk, rstd_rows, 0, unroll=4)
    lax.fori_loop(0, nblk, norm_rows, 0, unroll=2)


def _layernorm(z, gain, bias, *, tm, emit_bf16):
    m, n = z.shape
    row_spec = pl.BlockSpec((tm, n), lambda i: (i, 0))
    vec_spec = pl.BlockSpec((1, n), lambda i: (0, 0))
    out_shape = [jax.ShapeDtypeStruct((m, n), F32)]
    if emit_bf16:
        out_shape.append(jax.ShapeDtypeStruct((m, n), BF16))
    return pl.pallas_call(
        _ln_kernel,
        grid=(m // tm,),
        in_specs=[row_spec, vec_spec, vec_spec],
        out_specs=[row_spec] * len(out_shape),
        out_shape=out_shape,
        scratch_shapes=[pltpu.VMEM((tm, V7X_LANES), F32), pltpu.VMEM((tm, V7X_LANES), F32)],
        compiler_params=_params("parallel"),
        name="layernorm",
    )(z, gain.reshape(1, n), bias.reshape(1, n))


def _chunk_steps(qs, ks, vs, las, st_ref, tril_bf, causal):
    c = qs[0].shape[0]
    nt = (((1,), (1,)), ((), ()))
    tn = (((0,), (0,)), ((), ()))
    heads = range(len(qs))
    bs = []
    for la in las:
        la_hi = la.astype(BF16)
        la_lo = (la - la_hi.astype(F32)).astype(BF16)
        bs.append(jnp.dot(tril_bf, la_hi, preferred_element_type=F32)
                  + jnp.dot(tril_bf, la_lo, preferred_element_type=F32))
    q_decs, scores, kvs, inters, decays = [], [], [], [], []
    for hd in heads:
        b = bs[hd]
        b_end = b[c - 1:c, :]
        decay = jnp.exp(b_end)
        q_dec = (qs[hd] * jnp.exp(b)).astype(BF16)
        k_inv = ks[hd] * jnp.exp(-b)
        k_end = (k_inv * decay).astype(BF16)
        k_inv = k_inv.astype(BF16)
        v_bf = vs[hd].astype(BF16)
        vs[hd] = v_bf
        scores.append(lax.dot_general(q_dec, k_inv, nt, preferred_element_type=F32))
        kvs.append(lax.dot_general(v_bf, k_end, tn, preferred_element_type=F32))
        inters.append(lax.dot_general(q_dec, st_ref[hd].astype(BF16), nt, preferred_element_type=F32))
        decays.append(decay)
    outs = []
    for hd in heads:
        sc = jnp.where(causal, scores[hd], 0.0).astype(BF16)
        outs.append(jnp.dot(sc, vs[hd], preferred_element_type=F32) + inters[hd])
        st_ref[hd] = decays[hd] * st_ref[hd] + kvs[hd]
    return outs


def _gated_rmsnorm(o, gain, gate):
    ms = jnp.mean(o * o, axis=-1, keepdims=True)
    return o * lax.rsqrt(ms + NORM_EPS) * gain * (gate * jax.nn.sigmoid(gate))


def _chunk_masks():
    row = lax.broadcasted_iota(jnp.int32, (CHUNK, CHUNK), 0)
    col = lax.broadcasted_iota(jnp.int32, (CHUNK, CHUNK), 1)
    causal = row >= col
    return jnp.where(causal, 1.0, 0.0).astype(BF16), causal


def _log_sigmoid(z):
    return jnp.minimum(z, 0.0) - jnp.log1p(jnp.exp(-jnp.abs(z)))


def _gla_kernel(q_ref, k_ref, v_ref, g_ref, alr_ref, wa_ref, ba_ref, gain_ref, o_ref, st_ref,
                *, hb, nchunks, unroll):
    @pl.when(pl.program_id(2) == 0)
    def _():
        st_ref[...] = jnp.zeros_like(st_ref)

    tril_bf, causal = _chunk_masks()
    gain = gain_ref[...]
    ba = ba_ref[...]
    wa = wa_ref[...]

    def body(c, carry):
        rows = pl.ds(pl.multiple_of(c * CHUNK, CHUNK), CHUNK)
        z = jnp.dot(alr_ref[rows, :].astype(BF16), wa, preferred_element_type=F32) + ba
        la_all = _log_sigmoid(z) / GLA_TAU
        kcs = [slice(hd * GLA_DK, (hd + 1) * GLA_DK) for hd in range(hb)]
        vcs = [slice(hd * GLA_DV, (hd + 1) * GLA_DV) for hd in range(hb)]
        outs = _chunk_steps([q_ref[rows, kc] * (GLA_DK ** -0.5) for kc in kcs],
                            [k_ref[rows, kc] for kc in kcs], [v_ref[rows, vc] for vc in vcs],
                            [la_all[:, kc] for kc in kcs], st_ref, tril_bf, causal)
        for o, vc in zip(outs, vcs):
            o_ref[rows, vc] = _gated_rmsnorm(o, gain, g_ref[rows, vc]).astype(o_ref.dtype)
        return carry

    lax.fori_loop(0, nchunks, body, 0, unroll=unroll)


def _gla(h, alr, w_alpha, b_alpha, norm_g, *, bsz, seq, col_q, col_k, col_v, col_g, tt, hb, unroll):
    n = bsz * seq
    nt = seq // tt
    kw, vw = hb * GLA_DK, hb * GLA_DV
    hg = GLA_HEADS // hb

    def rows(b, g, t):
        return b * nt + t

    return pl.pallas_call(
        functools.partial(_gla_kernel, hb=hb, nchunks=tt // CHUNK, unroll=unroll),
        grid=(bsz, hg, nt),
        in_specs=[pl.BlockSpec((tt, kw), lambda b, g, t: (rows(b, g, t), col_q // kw + g)),
                  pl.BlockSpec((tt, kw), lambda b, g, t: (rows(b, g, t), col_k // kw + g)),
                  pl.BlockSpec((tt, vw), lambda b, g, t: (rows(b, g, t), col_v // vw + g)),
                  pl.BlockSpec((tt, vw), lambda b, g, t: (rows(b, g, t), col_g // vw + g)),
                  pl.BlockSpec((tt, V7X_LANES), lambda b, g, t: (rows(b, g, t), 0)),
                  pl.BlockSpec((V7X_LANES, kw), lambda b, g, t: (0, g)),
                  pl.BlockSpec((1, kw), lambda b, g, t: (0, g)),
                  pl.BlockSpec((1, GLA_DV), lambda b, g, t: (0, 0))],
        out_specs=pl.BlockSpec((tt, vw), lambda b, g, t: (rows(b, g, t), g)),
        out_shape=jax.ShapeDtypeStruct((n, GLA_HEADS * GLA_DV), BF16),
        scratch_shapes=[pltpu.VMEM((hb, GLA_DV, GLA_DK), F32)],
        compiler_params=_params("parallel", "parallel", "arbitrary"),
        name="gla",
    )(h, h, h, h, alr, w_alpha, b_alpha, norm_g)


def _hgrn_kernel(q_ref, f_ref, i_ref, g_ref, tab_ref, gain_ref, o_ref, st_ref, *, layer, hb, nchunks,
                 unroll):
    @pl.when(pl.program_id(2) == 0)
    def _():
        st_ref[...] = jnp.zeros_like(st_ref)

    tab = tab_ref[...]
    e = jnp.exp(tab - jnp.max(tab, axis=0, keepdims=True))
    soft = e / jnp.sum(e, axis=0, keepdims=True)
    lb_all = jnp.sum(soft[0:layer + 1], axis=0, keepdims=True) - soft[0:1]
    tril_bf, causal = _chunk_masks()
    gain = gain_ref[...]

    def body(c, carry):
        rows = pl.ds(pl.multiple_of(c * CHUNK, CHUNK), CHUNK)
        kcs = [slice(hd * HG_DK, (hd + 1) * HG_DK) for hd in range(hb)]
        vcs = [slice(hd * HG_DV, (hd + 1) * HG_DV) for hd in range(hb)]
        fs = [lb_all[:, kc] + (1.0 - lb_all[:, kc]) * jax.nn.sigmoid(f_ref[rows, kc]) for kc in kcs]
        ivs = [i_ref[rows, vc] for vc in vcs]
        outs = _chunk_steps([q_ref[rows, kc] for kc in kcs], [1.0 - f for f in fs],
                            [iv * jax.nn.sigmoid(iv) for iv in ivs], [jnp.log(f) for f in fs],
                            st_ref, tril_bf, causal)
        for o, vc in zip(outs, vcs):
            o_ref[rows, vc] = _gated_rmsnorm(o, gain, g_ref[rows, vc]).astype(o_ref.dtype)
        return carry

    lax.fori_loop(0, nchunks, body, 0, unroll=unroll)


def _hgrn(h, lb_table, norm_g, *, layer, bsz, seq, heads, tt, hb, unroll):
    n = bsz * seq
    nt = seq // tt
    w = hb * HG_DK
    hg = heads // hb

    def spec(seg):
        return pl.BlockSpec((tt, w), lambda b, g, t: (b * nt + t, seg * hg + g))

    return pl.pallas_call(
        functools.partial(_hgrn_kernel, layer=layer, hb=hb, nchunks=tt // CHUNK, unroll=unroll),
        grid=(bsz, hg, nt),
        in_specs=[spec(0), spec(1), spec(2), spec(3),
                  pl.BlockSpec((DEPTH, w), lambda b, g, t: (0, g)),
                  pl.BlockSpec((1, HG_DV), lambda b, g, t: (0, 0))],
        out_specs=pl.BlockSpec((tt, w), lambda b, g, t: (b * nt + t, g)),
        out_shape=jax.ShapeDtypeStruct((n, heads * HG_DV), BF16),
        scratch_shapes=[pltpu.VMEM((hb, HG_DV, HG_DK), F32)],
        compiler_params=_params("parallel", "parallel", "arbitrary"),
        name="hgrn2",
    )(h, h, h, h, lb_table, norm_g)


def _s5_prep_cast_kernel(lr_row_ref, li_row_ref, lr_col_ref, li_col_ref, ldt_ref,
                         btr_ref, bti_ref, ctr_ref, cti_ref, src_ref, w1_ref, f_ref, coef_ref, dst_ref):
    _s5_prep_kernel(lr_row_ref, li_row_ref, lr_col_ref, li_col_ref, ldt_ref,
                    btr_ref, bti_ref, ctr_ref, cti_ref, w1_ref, f_ref, coef_ref)
    dst_ref[...] = src_ref[...].astype(dst_ref.dtype)


def _s5_prep_kernel(lr_row_ref, li_row_ref, lr_col_ref, li_col_ref, ldt_ref,
                    btr_ref, bti_ref, ctr_ref, cti_ref, w1_ref, f_ref, coef_ref):
    L, J, P = S5_CHUNK, S5_GROUP, S5_STATE
    hi = lax.Precision.HIGHEST
    dt = jnp.exp(ldt_ref[0])
    lr, li = lr_row_ref[0], li_row_ref[0]
    npow = ((L - 1) - lax.broadcasted_iota(jnp.int32, (L, 2 * P), 0)).astype(F32)
    mag = jnp.exp(npow * (lr * dt))
    th = npow * (li * dt)
    ar, ai = mag * jnp.cos(th), mag * jnp.sin(th)
    half = lax.broadcasted_iota(jnp.int32, (L, 2 * P), 1) < P
    a1r, a1i = ar[L - 2:L - 1], ai[L - 2:L - 1]
    nr, ni = a1r - 1.0, a1i
    den = lr * lr + li * li
    fr, fi = (nr * lr + ni * li) / den, (ni * lr - nr * li) / den
    btr, bti = btr_ref[0], bti_ref[0]
    bbr, bbi = fr * btr - fi * bti, fr * bti + fi * btr
    pa, pb = jnp.where(half, ar, ai), jnp.where(half, -ai, ar)
    for s in range(L):
        rows = slice(s * J, (s + 1) * J)
        w1_ref[0, rows, L * J:] = (bbr * pa[s:s + 1] + bbi * pb[s:s + 1]).astype(BF16)
    alr_, ali_ = ar[0:1] * a1r - ai[0:1] * a1i, ar[0:1] * a1i + ai[0:1] * a1r
    half1 = half[0:1]
    coef_ref[0, 0:1, :] = alr_
    coef_ref[0, 1:2, :] = jnp.where(half1, -ali_, ali_)
    coef_ref[0, 2:, :] = jnp.zeros((V7X_SUBLANES - 2, 2 * P), F32)

    lrc, lic = lr_col_ref[0], li_col_ref[0]
    lane = lax.broadcasted_iota(jnp.int32, (P, L * J), 1)
    lag = jnp.right_shift(lane, int(math.log2(J))).astype(F32)
    magq = jnp.exp(lag * (lrc * dt))
    thq = lag * (lic * dt)
    aqr, aqi = magq * jnp.cos(thq), magq * jnp.sin(thq)
    cr, ci = ctr_ref[0], cti_ref[0]
    qr, qi = cr * aqr - ci * aqi, cr * aqi + ci * aqr
    r = (jnp.dot(bbr[:, :P], qr, precision=hi, preferred_element_type=F32)
         - jnp.dot(bbi[:, :P], qi, precision=hi, preferred_element_type=F32))
    lane_r = lax.broadcasted_iota(jnp.int32, (J, L * J), 1)
    for s in range(L):
        blk = r if s == 0 else jnp.where(lane_r >= s * J, pltpu.roll(r, s * J, 1), 0.0)
        w1_ref[0, s * J:(s + 1) * J, 0:L * J] = blk.astype(BF16)
    m1 = jnp.exp(lrc * dt)
    c1r, c1i = m1 * jnp.cos(lic * dt), m1 * jnp.sin(lic * dt)
    f_ref[0, 0:P, :] = (qr * c1r - qi * c1i).astype(BF16)
    f_ref[0, P:, :] = (-(qr * c1i + qi * c1r)).astype(BF16)


def _s5_prep(lam_re, lam_im, log_dt, b_re, b_im, c_re, c_im, cast_src=None):
    g = lam_re.shape[0]
    P, J, L = S5_STATE, S5_GROUP, S5_CHUNK
    dbl = lambda a: jnp.concatenate([a, a], axis=-1)
    lr_row = dbl(lam_re).reshape(g, 1, 2 * P)
    li_row = dbl(lam_im).reshape(g, 1, 2 * P)
    lr_col = lam_re.reshape(g, P, 1)
    li_col = lam_im.reshape(g, P, 1)
    ldt = log_dt.reshape(g, 1, 1)
    btr = dbl(jnp.swapaxes(b_re, 1, 2))
    bti = dbl(jnp.swapaxes(b_im, 1, 2))
    ctr = jnp.tile(jnp.swapaxes(c_re, 1, 2), (1, 1, L))
    cti = jnp.tile(jnp.swapaxes(c_im, 1, 2), (1, 1, L))
    blk = lambda *s: pl.BlockSpec((1,) + s, lambda i: (i, 0, 0))
    operands = [lr_row, li_row, lr_col, li_col, ldt, btr, bti, ctr, cti]
    in_specs = [blk(1, 2 * P), blk(1, 2 * P), blk(P, 1), blk(P, 1), blk(1, 1),
                blk(J, 2 * P), blk(J, 2 * P), blk(P, L * J), blk(P, L * J)]
    out_specs = [blk(L * J, L * J + 2 * P), blk(2 * P, L * J), blk(V7X_SUBLANES, 2 * P)]
    out_shape = [jax.ShapeDtypeStruct((g, L * J, L * J + 2 * P), BF16),
                 jax.ShapeDtypeStruct((g, 2 * P, L * J), BF16),
                 jax.ShapeDtypeStruct((g, V7X_SUBLANES, 2 * P), F32)]
    body = _s5_prep_kernel
    if cast_src is not None:
        m, d = cast_src.shape
        slab = pl.BlockSpec((m // g, d), lambda i: (i, 0))
        operands.append(cast_src)
        in_specs.append(slab)
        out_specs.append(slab)
        out_shape.append(jax.ShapeDtypeStruct((m, d), BF16))
        body = _s5_prep_cast_kernel
    return pl.pallas_call(body, grid=(g,), in_specs=in_specs, out_specs=out_specs, out_shape=out_shape,
                          compiler_params=_params("parallel"), name="s5_prep")(*operands)


def _s5_assemble_kernel(w1_ref, f_ref, coef_ref, wbig_ref, fbig_ref, cbig_ref):
    L, J, P, Q = S5_CHUNK, S5_GROUP, S5_STATE, S5_LANE_GROUPS
    lanes = Q * J
    width = L * lanes
    nstate = 2 * Q * P
    jbits, pbits = int(math.log2(J)), int(math.log2(P))

    def iota(shape, axis):
        return lax.broadcasted_iota(jnp.int32, shape, axis)

    a, b = iota((L * J, width), 0), iota((L * J, width), 1)
    sel_sc = jnp.where(((b >> int(math.log2(lanes))) == (a >> jbits)) & ((b & (J - 1)) == (a & (J - 1))),
                       1.0, 0.0).astype(BF16)
    a, b = iota((2 * P, nstate), 0), iota((2 * P, nstate), 1)
    sel_st = jnp.where(((b >> int(math.log2(Q * P))) == (a >> pbits)) & ((b & (P - 1)) == (a & (P - 1))),
                       1.0, 0.0).astype(BF16)
    col_group_sc = (iota((lanes, width), 1) >> jbits) & (Q - 1)
    col_group_st = (iota((lanes, nstate), 1) >> pbits) & (Q - 1)
    row_group_j = iota((lanes, 1), 0) >> jbits

    r_stack = jnp.concatenate([w1_ref[g, 0:J, 0:L * J] for g in range(Q)], axis=0)
    r_big = jnp.dot(r_stack, sel_sc, preferred_element_type=F32)
    r_big = jnp.where(col_group_sc == row_group_j, r_big, 0.0).astype(BF16)
    for s in range(L):
        rows = slice(s * lanes, (s + 1) * lanes)
        if s:
            wbig_ref[0, rows, 0:s * lanes] = jnp.zeros((lanes, s * lanes), BF16)
        wbig_ref[0, rows, s * lanes:width] = r_big[:, 0:width - s * lanes]
        e_stack = jnp.concatenate([w1_ref[g, s * J:(s + 1) * J, L * J:] for g in range(Q)], axis=0)
        e_big = jnp.dot(e_stack, sel_st, preferred_element_type=F32)
        wbig_ref[0, rows, width:] = jnp.where(col_group_st == row_group_j, e_big, 0.0).astype(BF16)

    row_in_pair = iota((lanes, 1), 0) >> pbits
    for m in range(nstate // lanes):
        ri, g0 = divmod(m * (lanes // P), Q)
        f_stack = jnp.concatenate([f_ref[g0 + t, ri * P:(ri + 1) * P, :] for t in range(lanes // P)], axis=0)
        f_big = jnp.dot(f_stack, sel_sc, preferred_element_type=F32)
        fbig_ref[0, m * lanes:(m + 1) * lanes, :] = jnp.where(
            col_group_sc == g0 + row_in_pair, f_big, 0.0).astype(BF16)

    low = iota((V7X_SUBLANES, lanes), 1) < P
    for k in range(Q // 2):
        c0, c1 = coef_ref[2 * k], coef_ref[2 * k + 1]
        row_is_a2 = iota((V7X_SUBLANES, lanes), 0) == 1
        re_half = jnp.where(low, c0, jnp.where(row_is_a2, -c1, c1))
        im_half = jnp.where(low, jnp.where(row_is_a2, -c0, c0), c1)
        cbig_ref[0, :, k * lanes:(k + 1) * lanes] = re_half
        cbig_ref[0, :, Q * P + k * lanes:Q * P + (k + 1) * lanes] = im_half


def _s5_assemble(w1, fmat, coef):
    g = w1.shape[0]
    L, J, P, Q = S5_CHUNK, S5_GROUP, S5_STATE, S5_LANE_GROUPS
    nb = g // Q
    width, nstate = L * Q * J, 2 * Q * P
    grp = lambda *s: pl.BlockSpec((Q,) + s, lambda i: (i, 0, 0))
    out = lambda *s: pl.BlockSpec((1,) + s, lambda i: (i, 0, 0))
    return pl.pallas_call(
        _s5_assemble_kernel,
        grid=(nb,),
        in_specs=[grp(L * J, L * J + 2 * P), grp(2 * P, L * J), grp(V7X_SUBLANES, 2 * P)],
        out_specs=[out(width, width + nstate), out(nstate, width), out(V7X_SUBLANES, nstate)],
        out_shape=[jax.ShapeDtypeStruct((nb, width, width + nstate), BF16),
                   jax.ShapeDtypeStruct((nb, nstate, width), BF16),
                   jax.ShapeDtypeStruct((nb, V7X_SUBLANES, nstate), F32)],
        compiler_params=_params("parallel"),
        name="s5_assemble",
    )(w1, fmat, coef)


def _s5_main_kernel(u_ref, w_ref, f_ref, coef_ref, o_ref, ys_ref, sp_ref, st_ref, *, nchunk, nb):
    width = u_ref.shape[2]
    half = st_ref.shape[1] // 2

    @pl.when(pl.program_id(1) == 0)
    def _():
        st_ref[...] = jnp.zeros_like(st_ref)

    ys_ref[...] = jnp.dot(u_ref[0], w_ref[0], preferred_element_type=F32)
    a1, a2 = coef_ref[0, 0:1, :], coef_ref[0, 1:2, :]

    def body(c, s):
        rows = pl.ds(pl.multiple_of(c * nb, nb), nb)
        sp_ref[rows, :] = s
        swapped = jnp.concatenate([s[:, half:], s[:, :half]], axis=1)
        return a1 * s + a2 * swapped + ys_ref[rows, width:]

    st_ref[...] = lax.fori_loop(0, nchunk, body, st_ref[...], unroll=4)
    y = ys_ref[:, 0:width] + jnp.dot(sp_ref[...].astype(BF16), f_ref[0], preferred_element_type=F32)
    o_ref[0] = y.astype(o_ref.dtype)


def _s5_main(u_c, wbig, fbig, cbig, *, nb, rows_per_step):
    nblk, rows, width = u_c.shape
    nstate = fbig.shape[1]
    nsteps = rows // rows_per_step
    fixed = lambda *s: pl.BlockSpec((1,) + s, lambda i, r: (i, 0, 0))
    moving = pl.BlockSpec((1, rows_per_step, width), lambda i, r: (i, r, 0))
    return pl.pallas_call(
        functools.partial(_s5_main_kernel, nchunk=rows_per_step // nb, nb=nb),
        grid=(nblk, nsteps),
        in_specs=[moving, fixed(width, width + nstate), fixed(nstate, width),
                  fixed(V7X_SUBLANES, nstate)],
        out_specs=moving,
        out_shape=jax.ShapeDtypeStruct((nblk, rows, width), BF16),
        scratch_shapes=[pltpu.VMEM((rows_per_step, width + nstate), F32),
                        pltpu.VMEM((rows_per_step, nstate), F32),
                        pltpu.VMEM((nb, nstate), F32)],
        compiler_params=_params("parallel", "arbitrary"),
        name="s5_main",
    )(u_c, wbig, fbig, cbig)


def _s5_glu_kernel(ys_ref, u_ref, d_ref, w_ref, o_ref):
    y = jax.nn.gelu(ys_ref[...].astype(F32) + d_ref[...] * u_ref[...], approximate=True)
    z = jnp.dot(y.astype(BF16), w_ref[...], preferred_element_type=F32)
    o_ref[...] = (y * jax.nn.sigmoid(z)).astype(o_ref.dtype)


def _s5_glu(y_ssm, h, d_skip, w_glu, *, tm):
    m, n = y_ssm.shape
    return pl.pallas_call(
        _s5_glu_kernel,
        grid=(m // tm,),
        in_specs=[pl.BlockSpec((tm, n), lambda i: (i, 0)),
                  pl.BlockSpec((tm, n), lambda i: (i, 0)),
                  pl.BlockSpec((1, n), lambda i: (0, 0)),
                  pl.BlockSpec((n, n), lambda i: (0, 0))],
        out_specs=pl.BlockSpec((tm, n), lambda i: (i, 0)),
        out_shape=jax.ShapeDtypeStruct((m, n), BF16),
        compiler_params=_params("parallel"),
        name="s5_glu",
    )(y_ssm, h, d_skip.reshape(1, n), w_glu)


def _pad_cols(w, n):
    return jnp.pad(w, ((0, 0), (0, n - w.shape[1])))


def _ffn_block(x_f32, x_bf, w_gate, w_up, w_down, layer, ln_g, ln_b, *, emit_bf16):
    hidden = w_gate.shape[2]
    hpad = -(-hidden // FFN_DOWN_TK) * FFN_DOWN_TK
    a, wd = _ffn_up(x_bf, w_gate, w_up, w_down, layer, tm=1024, tn=512, n_out=hpad)
    z = _matmul_resid([a], wd, x_f32, tm=1024, tn=1024, tk=FFN_DOWN_TK, name="ffn_down")
    return _layernorm(z, ln_g, ln_b, tm=LN_TM, emit_bf16=emit_bf16)


def _even_layer(x_f32, x_bf, bsz, seq, e, w_in, w_out, lam_re, lam_im, log_dt, b_re, b_im, c_re, c_im,
                d_skip, w_glu, w_alpha, b_alpha, norm_g, ln_g, ln_b):
    n = bsz * seq
    s5w = d_skip.shape[0]
    groups = s5w // S5_GROUP
    kw = GLA_HEADS * GLA_DK
    vw = GLA_HEADS * GLA_DV
    main = s5w + 2 * kw + 2 * vw
    w1, fmat, coef, *maybe_x_bf = _s5_prep(lam_re, lam_im, log_dt, b_re, b_im, c_re, c_im,
                                           cast_src=x_f32 if x_bf is None else None)
    (x_bf,) = maybe_x_bf or (x_bf,)
    h = _matmul(x_bf, w_in, e, n=main, tm=1024, tn=IN_PROJ_TN, out_dtype=F32, name="even_in")
    alr = _matmul(x_bf, _pad_cols(w_in[e, :, main:], V7X_LANES)[None], 0, n=V7X_LANES, tm=1024,
                  tn=V7X_LANES, out_dtype=F32, name="even_in_gate")

    wbig, fbig, cbig = _s5_assemble(w1, fmat, coef)
    nchunk = seq // S5_CHUNK
    nblk = s5w // V7X_LANES
    u_c = (h[:, :s5w].astype(BF16).reshape(bsz, nchunk, S5_CHUNK, nblk, V7X_LANES)
           .transpose(3, 1, 0, 2, 4).reshape(nblk, nchunk * bsz, S5_CHUNK * V7X_LANES))
    y_c = _s5_main(u_c, wbig, fbig, cbig, nb=bsz, rows_per_step=256)
    y_ssm = (y_c.reshape(nblk, nchunk, bsz, S5_CHUNK, V7X_LANES)
             .transpose(2, 1, 3, 0, 4).reshape(n, s5w))
    y_s5 = _s5_glu(y_ssm, h, d_skip, w_glu.astype(BF16), tm=512)

    wa = jnp.pad(w_alpha, ((0, V7X_LANES - w_alpha.shape[0]), (0, 0))).astype(BF16)
    y_gla = _gla(h, alr, wa, b_alpha.reshape(1, kw), norm_g.reshape(1, GLA_DV), bsz=bsz, seq=seq,
                 col_q=s5w, col_k=s5w + kw, col_v=s5w + 2 * kw, col_g=s5w + 2 * kw + vw,
                 tt=512, hb=4, unroll=2)

    z = _matmul_resid([y_s5, y_gla], w_out.astype(BF16), x_f32, tm=1024, tn=1024, tk=w_out.shape[0],
                      name="even_out")
    return _layernorm(z, ln_g, ln_b, tm=LN_TM, emit_bf16=True)


def _odd_layer(x_f32, x_bf, bsz, seq, layer, o, w_in, w_out, lb_table, norm_g, ln_g, ln_b):
    heads = w_out.shape[0] // HG_DV
    if x_bf is None:
        x_bf = x_f32.astype(BF16)
    h = _matmul(x_bf, w_in, o, n=w_in.shape[2], tm=1024, tn=IN_PROJ_TN, out_dtype=F32, name="odd_in")
    y = _hgrn(h, lb_table, norm_g.reshape(1, HG_DV), layer=layer, bsz=bsz, seq=seq, heads=heads,
              tt=512, hb=8, unroll=2)
    z = _matmul_resid([y], w_out.astype(BF16), x_f32, tm=1024, tn=1024, tk=w_out.shape[0], name="odd_out")
    return _layernorm(z, ln_g, ln_b, tm=LN_TM, emit_bf16=True)


def kernel(x, ev_w_in, ev_w_out, s5_lam_re, s5_lam_im, s5_log_dt, s5_b_re, s5_b_im, s5_c_re, s5_c_im,
           s5_d, s5_w_glu, gla_w_alpha, gla_b_alpha, gla_norm_g, od_w_in, od_w_out, hg_lb_table,
           hg_norm_g, ln_mix_g, ln_mix_b, ln_ffn_g, ln_ffn_b, ffn_w_gate, ffn_w_up, ffn_w_down):
    bsz, seq, d = x.shape
    depth = ln_mix_g.shape[0]
    assert depth == DEPTH
    xf = x.reshape(bsz * seq, d)
    xb = None
    for layer in range(depth):
        if layer % 2 == 0:
            e = layer // 2
            xf, xb = _even_layer(xf, xb, bsz, seq, e, ev_w_in, ev_w_out[e], s5_lam_re[e], s5_lam_im[e],
                                 s5_log_dt[e], s5_b_re[e], s5_b_im[e], s5_c_re[e], s5_c_im[e], s5_d[e],
                                 s5_w_glu[e], gla_w_alpha[e], gla_b_alpha[e], gla_norm_g[e],
                                 ln_mix_g[layer], ln_mix_b[layer])
        else:
            o = layer // 2
            xf, xb = _odd_layer(xf, xb, bsz, seq, layer, o, od_w_in, od_w_out[o], hg_lb_table,
                                hg_norm_g[o], ln_mix_g[layer], ln_mix_b[layer])
        last = layer == depth - 1
        res = _ffn_block(xf, xb, ffn_w_gate, ffn_w_up, ffn_w_down, layer,
                         ln_ffn_g[layer], ln_ffn_b[layer], emit_bf16=not last)
        xf, xb = (res[0], None) if last else res
    return xf.reshape(bsz, seq, d).astype(x.dtype)
```

```python
import functools
import math
from typing import NamedTuple

import jax
import jax.numpy as jnp
from jax import lax
from jax.experimental import pallas as pl
from jax.experimental.pallas import tpu as pltpu

F32 = jnp.float32
BF16 = jnp.bfloat16

DEPTH = 2
S5_GROUP = 16
S5_STATE = 64
GLA_HEADS = 8
GLA_DK = 128
GLA_DV = 256
GLA_GATE_RANK = 16
GLA_TAU = 16.0
HG_DK = 128
HG_DV = 128
CHUNK = 64
DEEPNORM_ALPHA = (2.0 * DEPTH) ** 0.25
NORM_EPS = 1e-5

V7X_LANES = 128
V7X_SUBLANES = 8
V7X_VMEM_LIMIT_BYTES = 60 * 1024 * 1024
V7X_VMEM_COMPILER_TEMP_BYTES = 12 * 1024 * 1024

MM_ROW_CHUNK = 256
IN_PROJ_TN = 1024
FFN_DOWN_TK = 2816
LN_TM = 256
LN_COL_BLOCK = 1024
LN_ROW_BLOCK = 16

S5_CHUNK = 16
S5_ROW = S5_CHUNK * S5_GROUP
S5_LANE_GROUPS = V7X_LANES // S5_GROUP


def _nbytes(shape, dtype):
    return math.prod(shape) * jnp.dtype(dtype).itemsize


def _params(*sem, buffers=None):
    limit = V7X_VMEM_LIMIT_BYTES
    if buffers is not None:
        need = sum(_nbytes(shape, dtype) * count for shape, dtype, count in buffers)
        limit = min(limit, need + V7X_VMEM_COMPILER_TEMP_BYTES)
    return pltpu.CompilerParams(dimension_semantics=sem, vmem_limit_bytes=limit)


def _stage_weight_tile(w_hbm, stage_ref, wb_ref, sem, *, layer, n_valid, nj):
    j = pl.program_id(0)
    tn = stage_ref.shape[1]
    last = nj - 1
    last_w = n_valid - last * tn

    def full_copy(jj):
        cols = pl.ds(pl.multiple_of(jj * tn, tn), tn)
        return pltpu.make_async_copy(w_hbm.at[layer, :, cols], stage_ref, sem)

    def last_copy():
        return pltpu.make_async_copy(w_hbm.at[layer, :, pl.ds(last * tn, last_w)],
                                     stage_ref.at[:, pl.ds(0, last_w)], sem)

    def start(jj):
        if last_w == tn:
            full_copy(jj).start()
        else:
            pl.when(jj < last)(lambda: full_copy(jj).start())
            pl.when(jj == last)(lambda: last_copy().start())

    @pl.when(pl.program_id(1) == 0)
    def _():
        pl.when(j == 0)(lambda: start(j))
        if last_w == tn:
            full_copy(j).wait()
            wb_ref[...] = stage_ref[...].astype(BF16)
        else:
            @pl.when(j < last)
            def _():
                full_copy(j).wait()
                wb_ref[...] = stage_ref[...].astype(BF16)

            @pl.when(j == last)
            def _():
                last_copy().wait()
                wb_ref[:, 0:last_w] = stage_ref[:, 0:last_w].astype(BF16)
                wb_ref[:, last_w:] = jnp.zeros((wb_ref.shape[0], tn - last_w), BF16)

        pl.when(j < last)(lambda: start(j + 1))


def _mm_kernel(a_ref, w_hbm, o_ref, stage_ref, wb_ref, sem, *, layer, n_valid, nj):
    _stage_weight_tile(w_hbm, stage_ref, wb_ref, sem, layer=layer, n_valid=n_valid, nj=nj)
    w = wb_ref[...]
    for r0 in range(0, a_ref.shape[0], MM_ROW_CHUNK):
        rows = slice(r0, r0 + MM_ROW_CHUNK)
        o_ref[rows, :] = jnp.dot(a_ref[rows, :], w, preferred_element_type=F32).astype(o_ref.dtype)


def _matmul(a, w, layer, *, n, tm, tn, out_dtype, name):
    m, k = a.shape
    nj = n // tn
    return pl.pallas_call(
        functools.partial(_mm_kernel, layer=layer, n_valid=n, nj=nj),
        grid=(nj, m // tm),
        in_specs=[pl.BlockSpec((tm, k), lambda j, i: (i, 0)),
                  pl.BlockSpec(memory_space=pl.ANY)],
        out_specs=pl.BlockSpec((tm, tn), lambda j, i: (i, j)),
        out_shape=jax.ShapeDtypeStruct((m, n), out_dtype),
        scratch_shapes=[pltpu.VMEM((k, tn), F32), pltpu.VMEM((k, tn), BF16), pltpu.SemaphoreType.DMA],
        compiler_params=_params("arbitrary", "arbitrary", buffers=[
            ((tm, k), BF16, 2), ((k, tn), F32, 1), ((k, tn), BF16, 1), ((tm, tn), out_dtype, 2)]),
        name=name,
    )(a, w)


def _ffn_up_kernel(x_ref, wg_hbm, wu_hbm, wd_ref, o_ref, wdb_ref, sg_ref, su_ref, wgb_ref, wub_ref, sems,
                   *, layer, n_valid, nj, wd_blocks):
    _stage_weight_tile(wg_hbm, sg_ref, wgb_ref, sems.at[0], layer=layer, n_valid=n_valid, nj=nj)
    _stage_weight_tile(wu_hbm, su_ref, wub_ref, sems.at[1], layer=layer, n_valid=n_valid, nj=nj)
    tn = o_ref.shape[1]
    last = nj - 1
    last_w = n_valid - last * tn

    chunk = min(MM_ROW_CHUNK, x_ref.shape[0])

    def tile(width):
        wg, wu = wgb_ref[:, 0:width], wub_ref[:, 0:width]
        for r0 in range(0, x_ref.shape[0], chunk):
            rows = slice(r0, r0 + chunk)
            x = x_ref[rows, :]
            g = jnp.dot(x, wg, preferred_element_type=F32)
            u = jnp.dot(x, wu, preferred_element_type=F32)
            o_ref[rows, 0:width] = (g * jax.nn.sigmoid(g) * u).astype(o_ref.dtype)
            if width < tn:
                o_ref[rows, width:] = jnp.zeros((chunk, tn - width), o_ref.dtype)

    if last_w == tn:
        tile(tn)
    else:
        pl.when(pl.program_id(0) < last)(lambda: tile(tn))
        pl.when(pl.program_id(0) == last)(lambda: tile(last_w))
    step = pl.program_id(0) * pl.num_programs(1) + pl.program_id(1)
    wdb_ref[...] = jnp.where(step < wd_blocks, wd_ref[...], 0.0).astype(wdb_ref.dtype)


def _ffn_up(x_bf, wg, wu, wd, layer, *, tm, tn, n_out):
    m, k = x_bf.shape
    hidden, dout = wd.shape[1:]
    nj, ni = n_out // tn, m // tm
    slab = n_out // (nj * ni)
    assert slab * nj * ni == n_out and hidden % slab == 0 and slab % (2 * V7X_SUBLANES) == 0
    wd_blocks = hidden // slab
    hbm = pl.BlockSpec(memory_space=pl.ANY)
    return pl.pallas_call(
        functools.partial(_ffn_up_kernel, layer=layer, n_valid=wg.shape[2], nj=nj, wd_blocks=wd_blocks),
        grid=(nj, ni),
        in_specs=[pl.BlockSpec((tm, k), lambda j, i: (i, 0)), hbm, hbm,
                  pl.BlockSpec((None, slab, dout),
                               lambda j, i: (layer, jnp.minimum(j * ni + i, wd_blocks - 1), 0))],
        out_specs=[pl.BlockSpec((tm, tn), lambda j, i: (i, j)),
                   pl.BlockSpec((slab, dout), lambda j, i: (j * ni + i, 0))],
        out_shape=[jax.ShapeDtypeStruct((m, n_out), BF16), jax.ShapeDtypeStruct((n_out, dout), BF16)],
        scratch_shapes=[pltpu.VMEM((k, tn), F32), pltpu.VMEM((k, tn), F32),
                        pltpu.VMEM((k, tn), BF16), pltpu.VMEM((k, tn), BF16),
                        pltpu.SemaphoreType.DMA((2,))],
        compiler_params=_params("arbitrary", "arbitrary", buffers=[
            ((tm, k), BF16, 2), ((k, tn), F32, 2), ((k, tn), BF16, 2), ((tm, tn), BF16, 2),
            ((slab, dout), F32, 2), ((slab, dout), BF16, 2)]),
        name="ffn_up",
    )(x_bf, wg, wu, wd)


def _mm_resid_kernel(*refs, k_parts, nk):
    n_a = len(k_parts)
    a_refs = refs[:n_a]
    w_ref, *r_refs, o_ref = refs[n_a:]
    k = pl.program_id(2)

    def product():
        acc, k0 = None, 0
        for a_ref, kp in zip(a_refs, k_parts):
            d = jnp.dot(a_ref[...], w_ref[k0:k0 + kp, :], preferred_element_type=F32)
            acc = d if acc is None else acc + d
            k0 += kp
        return acc

    def residual():
        if len(r_refs) == 1:
            return r_refs[0][...]
        z_ref, mu_ref, rs_ref, g_ref, b_ref = r_refs
        nrep = z_ref.shape[1] // mu_ref.shape[1]
        mu, rs = jnp.tile(mu_ref[...], (1, nrep)), jnp.tile(rs_ref[...], (1, nrep))
        return (z_ref[...] - mu) * rs * g_ref[...] + b_ref[...]

    if nk == 1:
        o_ref[...] = product() + DEEPNORM_ALPHA * residual()
    else:
        @pl.when(k == 0)
        def _():
            o_ref[...] = product() + DEEPNORM_ALPHA * residual()

        @pl.when(k > 0)
        def _():
            o_ref[...] += product()


class _NormedRows(NamedTuple):
    z: jax.Array
    mean: jax.Array
    rstd: jax.Array
    gain: jax.Array
    bias: jax.Array


def _matmul_resid(a_parts, w, resid, *, tm, tn, tk, name):
    tile = pl.BlockSpec((tm, tn), lambda i, j, k: (i, j))
    if isinstance(resid, _NormedRows):
        m, n = resid.z.shape
        stat = pl.BlockSpec((tm, V7X_LANES), lambda i, j, k: (i, 0))
        vec = pl.BlockSpec((1, tn), lambda i, j, k: (0, j))
        r_ops = [resid.z, resid.mean, resid.rstd, resid.gain.reshape(1, n), resid.bias.reshape(1, n)]
        r_specs = [tile, stat, stat, vec, vec]
    else:
        m, n = resid.shape
        r_ops, r_specs = [resid], [tile]
    kdim = w.shape[0]
    if len(a_parts) > 1:
        assert tk == kdim
        k_parts = tuple(a.shape[1] for a in a_parts)
        a_specs = [pl.BlockSpec((tm, kp), lambda i, j, k: (i, 0)) for kp in k_parts]
    else:
        k_parts = (tk,)
        a_specs = [pl.BlockSpec((tm, tk), lambda i, j, k: (i, k))]
    nk = kdim // tk
    return pl.pallas_call(
        functools.partial(_mm_resid_kernel, k_parts=k_parts, nk=nk),
        grid=(m // tm, n // tn, nk),
        in_specs=a_specs + [pl.BlockSpec((tk, tn), lambda i, j, k: (k, j))] + r_specs,
        out_specs=tile,
        out_shape=jax.ShapeDtypeStruct((m, n), F32),
        compiler_params=_params("parallel", "parallel", "arbitrary"),
        name=name,
    )(*a_parts, w, *r_ops)


def _ln_kernel(z_ref, g_ref, b_ref, y_ref, mu_ref, rs_ref):
    inv_n = 1.0 / z_ref.shape[1]
    nblk = z_ref.shape[0] // LN_ROW_BLOCK
    lanes = mu_ref.shape[1]
    nrep = z_ref.shape[1] // lanes

    def block(r):
        return pl.ds(pl.multiple_of(r * LN_ROW_BLOCK, LN_ROW_BLOCK), LN_ROW_BLOCK)

    def mean_rows(r, carry):
        rows = block(r)
        mu = jnp.sum(z_ref[rows, :], axis=-1, keepdims=True) * inv_n
        mu_ref[rows, :] = jnp.broadcast_to(mu, (LN_ROW_BLOCK, lanes))
        return carry

    def rstd_rows(r, carry):
        rows = block(r)
        zc = z_ref[rows, :] - jnp.tile(mu_ref[rows, :], (1, nrep))
        var = jnp.sum(zc * zc, axis=-1, keepdims=True) * inv_n
        rs_ref[rows, :] = jnp.broadcast_to(lax.rsqrt(var + NORM_EPS), (LN_ROW_BLOCK, lanes))
        return carry

    def norm_rows(r, carry):
        rows = block(r)
        mu = jnp.tile(mu_ref[rows, :], (1, LN_COL_BLOCK // lanes))
        rs = jnp.tile(rs_ref[rows, :], (1, LN_COL_BLOCK // lanes))
        for c0 in range(0, z_ref.shape[1], LN_COL_BLOCK):
            cols = slice(c0, c0 + LN_COL_BLOCK)
            y = (z_ref[rows, cols] - mu) * rs * g_ref[:, cols] + b_ref[:, cols]
            y_ref[rows, cols] = y.astype(y_ref.dtype)
        return carry

    lax.fori_loop(0, nblk, mean_rows, 0, unroll=4)
    lax.fori_loop(0, nblk, rstd_rows, 0, unroll=4)
    lax.fori_loop(0, nblk, norm_rows, 0, unroll=2)


def _layernorm(z, gain, bias, *, tm, out_dtype):
    m, n = z.shape
    row_spec = pl.BlockSpec((tm, n), lambda i: (i, 0))
    vec_spec = pl.BlockSpec((1, n), lambda i: (0, 0))
    stat_spec = pl.BlockSpec((tm, V7X_LANES), lambda i: (i, 0))
    stat_shape = jax.ShapeDtypeStruct((m, V7X_LANES), F32)
    return pl.pallas_call(
        _ln_kernel,
        grid=(m // tm,),
        in_specs=[row_spec, vec_spec, vec_spec],
        out_specs=[row_spec, stat_spec, stat_spec],
        out_shape=[jax.ShapeDtypeStruct((m, n), out_dtype), stat_shape, stat_shape],
        compiler_params=_params("parallel"),
        name="layernorm",
    )(z, gain.reshape(1, n), bias.reshape(1, n))


def _norm_stage(z, gain, bias):
    y_bf, mean, rstd = _layernorm(z, gain, bias, tm=LN_TM, out_dtype=BF16)
    return _NormedRows(z, mean, rstd, gain, bias), y_bf


def _chunk_steps(qs, ks, vs, las, st_ref, tril_bf, causal):
    c = qs[0].shape[0]
    nt = (((1,), (1,)), ((), ()))
    tn = (((0,), (0,)), ((), ()))
    heads = range(len(qs))
    bs = []
    for la in las:
        la_hi = la.astype(BF16)
        la_lo = (la - la_hi.astype(F32)).astype(BF16)
        bs.append(jnp.dot(tril_bf, la_hi, preferred_element_type=F32)
                  + jnp.dot(tril_bf, la_lo, preferred_element_type=F32))
    q_decs, scores, kvs, inters, decays = [], [], [], [], []
    for hd in heads:
        b = bs[hd]
        b_end = b[c - 1:c, :]
        decay = jnp.exp(b_end)
        q_dec = (qs[hd] * jnp.exp(b)).astype(BF16)
        k_inv = ks[hd] * jnp.exp(-b)
        k_end = (k_inv * decay).astype(BF16)
        k_inv = k_inv.astype(BF16)
        v_bf = vs[hd].astype(BF16)
        vs[hd] = v_bf
        scores.append(lax.dot_general(q_dec, k_inv, nt, preferred_element_type=F32))
        kvs.append(lax.dot_general(v_bf, k_end, tn, preferred_element_type=F32))
        inters.append(lax.dot_general(q_dec, st_ref[hd].astype(BF16), nt, preferred_element_type=F32))
        decays.append(decay)
    outs = []
    for hd in heads:
        sc = jnp.where(causal, scores[hd], 0.0).astype(BF16)
        outs.append(jnp.dot(sc, vs[hd], preferred_element_type=F32) + inters[hd])
        st_ref[hd] = decays[hd] * st_ref[hd] + kvs[hd]
    return outs


def _gated_rmsnorm(o, gain, gate):
    ms = jnp.mean(o * o, axis=-1, keepdims=True)
    return o * lax.rsqrt(ms + NORM_EPS) * gain * (gate * jax.nn.sigmoid(gate))


def _chunk_masks():
    row = lax.broadcasted_iota(jnp.int32, (CHUNK, CHUNK), 0)
    col = lax.broadcasted_iota(jnp.int32, (CHUNK, CHUNK), 1)
    causal = row >= col
    return jnp.where(causal, 1.0, 0.0).astype(BF16), causal


def _log_sigmoid(z):
    return jnp.minimum(z, 0.0) - jnp.log1p(jnp.exp(-jnp.abs(z)))


def _gla_kernel(q_ref, k_ref, v_ref, g_ref, alr_ref, wa_ref, ba_ref, gain_ref, o_ref, st_ref,
                *, hb, nchunks, unroll):
    @pl.when(pl.program_id(2) == 0)
    def _():
        st_ref[...] = jnp.zeros_like(st_ref)

    tril_bf, causal = _chunk_masks()
    gain = gain_ref[...]
    ba = ba_ref[...]
    wa = wa_ref[...]

    def body(c, carry):
        rows = pl.ds(pl.multiple_of(c * CHUNK, CHUNK), CHUNK)
        z = jnp.dot(alr_ref[rows, :].astype(BF16), wa, preferred_element_type=F32) + ba
        la_all = _log_sigmoid(z) / GLA_TAU
        kcs = [slice(hd * GLA_DK, (hd + 1) * GLA_DK) for hd in range(hb)]
        vcs = [slice(hd * GLA_DV, (hd + 1) * GLA_DV) for hd in range(hb)]
        outs = _chunk_steps([q_ref[rows, kc] * (GLA_DK ** -0.5) for kc in kcs],
                            [k_ref[rows, kc] for kc in kcs], [v_ref[rows, vc] for vc in vcs],
                            [la_all[:, kc] for kc in kcs], st_ref, tril_bf, causal)
        for o, vc in zip(outs, vcs):
            o_ref[rows, vc] = _gated_rmsnorm(o, gain, g_ref[rows, vc]).astype(o_ref.dtype)
        return carry

    lax.fori_loop(0, nchunks, body, 0, unroll=unroll)


def _gla(h, alr, w_alpha, b_alpha, norm_g, *, bsz, seq, col_q, col_k, col_v, col_g, tt, hb, unroll):
    n = bsz * seq
    nt = seq // tt
    kw, vw = hb * GLA_DK, hb * GLA_DV
    hg = GLA_HEADS // hb

    def rows(b, g, t):
        return b * nt + t

    return pl.pallas_call(
        functools.partial(_gla_kernel, hb=hb, nchunks=tt // CHUNK, unroll=unroll),
        grid=(bsz, hg, nt),
        in_specs=[pl.BlockSpec((tt, kw), lambda b, g, t: (rows(b, g, t), col_q // kw + g)),
                  pl.BlockSpec((tt, kw), lambda b, g, t: (rows(b, g, t), col_k // kw + g)),
                  pl.BlockSpec((tt, vw), lambda b, g, t: (rows(b, g, t), col_v // vw + g)),
                  pl.BlockSpec((tt, vw), lambda b, g, t: (rows(b, g, t), col_g // vw + g)),
                  pl.BlockSpec((tt, V7X_LANES), lambda b, g, t: (rows(b, g, t), 0)),
                  pl.BlockSpec((V7X_LANES, kw), lambda b, g, t: (0, g)),
                  pl.BlockSpec((1, kw), lambda b, g, t: (0, g)),
                  pl.BlockSpec((1, GLA_DV), lambda b, g, t: (0, 0))],
        out_specs=pl.BlockSpec((tt, vw), lambda b, g, t: (rows(b, g, t), g)),
        out_shape=jax.ShapeDtypeStruct((n, GLA_HEADS * GLA_DV), BF16),
        scratch_shapes=[pltpu.VMEM((hb, GLA_DV, GLA_DK), F32)],
        compiler_params=_params("parallel", "parallel", "arbitrary"),
        name="gla",
    )(h, h, h, h, alr, w_alpha, b_alpha, norm_g)


def _hgrn_kernel(q_ref, f_ref, i_ref, g_ref, tab_ref, gain_ref, o_ref, st_ref, *, layer, hb, nchunks,
                 unroll):
    @pl.when(pl.program_id(2) == 0)
    def _():
        st_ref[...] = jnp.zeros_like(st_ref)

    tab = tab_ref[...]
    e = jnp.exp(tab - jnp.max(tab, axis=0, keepdims=True))
    soft = e / jnp.sum(e, axis=0, keepdims=True)
    lb_all = jnp.sum(soft[0:layer + 1], axis=0, keepdims=True) - soft[0:1]
    tril_bf, causal = _chunk_masks()
    gain = gain_ref[...]

    def body(c, carry):
        rows = pl.ds(pl.multiple_of(c * CHUNK, CHUNK), CHUNK)
        kcs = [slice(hd * HG_DK, (hd + 1) * HG_DK) for hd in range(hb)]
        vcs = [slice(hd * HG_DV, (hd + 1) * HG_DV) for hd in range(hb)]
        fs = [lb_all[:, kc] + (1.0 - lb_all[:, kc]) * jax.nn.sigmoid(f_ref[rows, kc]) for kc in kcs]
        ivs = [i_ref[rows, vc] for vc in vcs]
        outs = _chunk_steps([q_ref[rows, kc] for kc in kcs], [1.0 - f for f in fs],
                            [iv * jax.nn.sigmoid(iv) for iv in ivs], [jnp.log(f) for f in fs],
                            st_ref, tril_bf, causal)
        for o, vc in zip(outs, vcs):
            o_ref[rows, vc] = _gated_rmsnorm(o, gain, g_ref[rows, vc]).astype(o_ref.dtype)
        return carry

    lax.fori_loop(0, nchunks, body, 0, unroll=unroll)


def _hgrn(h, lb_table, norm_g, *, layer, bsz, seq, heads, tt, hb, unroll):
    n = bsz * seq
    nt = seq // tt
    w = hb * HG_DK
    hg = heads // hb

    def spec(seg):
        return pl.BlockSpec((tt, w), lambda b, g, t: (b * nt + t, seg * hg + g))

    return pl.pallas_call(
        functools.partial(_hgrn_kernel, layer=layer, hb=hb, nchunks=tt // CHUNK, unroll=unroll),
        grid=(bsz, hg, nt),
        in_specs=[spec(0), spec(1), spec(2), spec(3),
                  pl.BlockSpec((DEPTH, w), lambda b, g, t: (0, g)),
                  pl.BlockSpec((1, HG_DV), lambda b, g, t: (0, 0))],
        out_specs=pl.BlockSpec((tt, w), lambda b, g, t: (b * nt + t, g)),
        out_shape=jax.ShapeDtypeStruct((n, heads * HG_DV), BF16),
        scratch_shapes=[pltpu.VMEM((hb, HG_DV, HG_DK), F32)],
        compiler_params=_params("parallel", "parallel", "arbitrary"),
        name="hgrn2",
    )(h, h, h, h, lb_table, norm_g)


def _s5_prep_cast_kernel(lr_row_ref, li_row_ref, lr_col_ref, li_col_ref, ldt_ref,
                         btr_ref, bti_ref, ctr_ref, cti_ref, src_ref, w1_ref, f_ref, coef_ref, dst_ref):
    _s5_prep_kernel(lr_row_ref, li_row_ref, lr_col_ref, li_col_ref, ldt_ref,
                    btr_ref, bti_ref, ctr_ref, cti_ref, w1_ref, f_ref, coef_ref)
    dst_ref[...] = src_ref[...].astype(dst_ref.dtype)


def _s5_prep_kernel(lr_row_ref, li_row_ref, lr_col_ref, li_col_ref, ldt_ref,
                    btr_ref, bti_ref, ctr_ref, cti_ref, w1_ref, f_ref, coef_ref):
    L, J, P = S5_CHUNK, S5_GROUP, S5_STATE
    hi = lax.Precision.HIGHEST
    dt = jnp.exp(ldt_ref[0])
    lr, li = lr_row_ref[0], li_row_ref[0]
    npow = ((L - 1) - lax.broadcasted_iota(jnp.int32, (L, 2 * P), 0)).astype(F32)
    mag = jnp.exp(npow * (lr * dt))
    th = npow * (li * dt)
    ar, ai = mag * jnp.cos(th), mag * jnp.sin(th)
    half = lax.broadcasted_iota(jnp.int32, (L, 2 * P), 1) < P
    a1r, a1i = ar[L - 2:L - 1], ai[L - 2:L - 1]
    nr, ni = a1r - 1.0, a1i
    den = lr * lr + li * li
    fr, fi = (nr * lr + ni * li) / den, (ni * lr - nr * li) / den
    btr, bti = btr_ref[0], bti_ref[0]
    bbr, bbi = fr * btr - fi * bti, fr * bti + fi * btr
    pa, pb = jnp.where(half, ar, ai), jnp.where(half, -ai, ar)
    for s in range(L):
        rows = slice(s * J, (s + 1) * J)
        w1_ref[0, rows, L * J:] = (bbr * pa[s:s + 1] + bbi * pb[s:s + 1]).astype(BF16)
    alr_, ali_ = ar[0:1] * a1r - ai[0:1] * a1i, ar[0:1] * a1i + ai[0:1] * a1r
    half1 = half[0:1]
    coef_ref[0, 0:1, :] = alr_
    coef_ref[0, 1:2, :] = jnp.where(half1, -ali_, ali_)
    coef_ref[0, 2:, :] = jnp.zeros((V7X_SUBLANES - 2, 2 * P), F32)

    lrc, lic = lr_col_ref[0], li_col_ref[0]
    lane = lax.broadcasted_iota(jnp.int32, (P, L * J), 1)
    lag = jnp.right_shift(lane, int(math.log2(J))).astype(F32)
    magq = jnp.exp(lag * (lrc * dt))
    thq = lag * (lic * dt)
    aqr, aqi = magq * jnp.cos(thq), magq * jnp.sin(thq)
    cr, ci = ctr_ref[0], cti_ref[0]
    qr, qi = cr * aqr - ci * aqi, cr * aqi + ci * aqr
    r = (jnp.dot(bbr[:, :P], qr, precision=hi, preferred_element_type=F32)
         - jnp.dot(bbi[:, :P], qi, precision=hi, preferred_element_type=F32))
    lane_r = lax.broadcasted_iota(jnp.int32, (J, L * J), 1)
    for s in range(L):
        blk = r if s == 0 else jnp.where(lane_r >= s * J, pltpu.roll(r, s * J, 1), 0.0)
        w1_ref[0, s * J:(s + 1) * J, 0:L * J] = blk.astype(BF16)
    m1 = jnp.exp(lrc * dt)
    c1r, c1i = m1 * jnp.cos(lic * dt), m1 * jnp.sin(lic * dt)
    f_ref[0, 0:P, :] = (qr * c1r - qi * c1i).astype(BF16)
    f_ref[0, P:, :] = (-(qr * c1i + qi * c1r)).astype(BF16)


def _s5_prep(lam_re, lam_im, log_dt, b_re, b_im, c_re, c_im, cast_src=None):
    g = lam_re.shape[0]
    P, J, L = S5_STATE, S5_GROUP, S5_CHUNK
    dbl = lambda a: jnp.concatenate([a, a], axis=-1)
    lr_row = dbl(lam_re).reshape(g, 1, 2 * P)
    li_row = dbl(lam_im).reshape(g, 1, 2 * P)
    lr_col = lam_re.reshape(g, P, 1)
    li_col = lam_im.reshape(g, P, 1)
    ldt = log_dt.reshape(g, 1, 1)
    btr = dbl(jnp.swapaxes(b_re, 1, 2))
    bti = dbl(jnp.swapaxes(b_im, 1, 2))
    ctr = jnp.tile(jnp.swapaxes(c_re, 1, 2), (1, 1, L))
    cti = jnp.tile(jnp.swapaxes(c_im, 1, 2), (1, 1, L))
    blk = lambda *s: pl.BlockSpec((1,) + s, lambda i: (i, 0, 0))
    operands = [lr_row, li_row, lr_col, li_col, ldt, btr, bti, ctr, cti]
    in_specs = [blk(1, 2 * P), blk(1, 2 * P), blk(P, 1), blk(P, 1), blk(1, 1),
                blk(J, 2 * P), blk(J, 2 * P), blk(P, L * J), blk(P, L * J)]
    out_specs = [blk(L * J, L * J + 2 * P), blk(2 * P, L * J), blk(V7X_SUBLANES, 2 * P)]
    out_shape = [jax.ShapeDtypeStruct((g, L * J, L * J + 2 * P), BF16),
                 jax.ShapeDtypeStruct((g, 2 * P, L * J), BF16),
                 jax.ShapeDtypeStruct((g, V7X_SUBLANES, 2 * P), F32)]
    body = _s5_prep_kernel
    if cast_src is not None:
        m, d = cast_src.shape
        slab = pl.BlockSpec((m // g, d), lambda i: (i, 0))
        operands.append(cast_src)
        in_specs.append(slab)
        out_specs.append(slab)
        out_shape.append(jax.ShapeDtypeStruct((m, d), BF16))
        body = _s5_prep_cast_kernel
    return pl.pallas_call(body, grid=(g,), in_specs=in_specs, out_specs=out_specs, out_shape=out_shape,
                          compiler_params=_params("parallel"), name="s5_prep")(*operands)


def _s5_assemble_kernel(w1_ref, f_ref, coef_ref, wbig_ref, fbig_ref, cbig_ref):
    L, J, P, Q = S5_CHUNK, S5_GROUP, S5_STATE, S5_LANE_GROUPS
    lanes = Q * J
    width = L * lanes
    nstate = 2 * Q * P
    jbits, pbits = int(math.log2(J)), int(math.log2(P))

    def iota(shape, axis):
        return lax.broadcasted_iota(jnp.int32, shape, axis)

    a, b = iota((L * J, width), 0), iota((L * J, width), 1)
    sel_sc = jnp.where(((b >> int(math.log2(lanes))) == (a >> jbits)) & ((b & (J - 1)) == (a & (J - 1))),
                       1.0, 0.0).astype(BF16)
    a, b = iota((2 * P, nstate), 0), iota((2 * P, nstate), 1)
    sel_st = jnp.where(((b >> int(math.log2(Q * P))) == (a >> pbits)) & ((b & (P - 1)) == (a & (P - 1))),
                       1.0, 0.0).astype(BF16)
    col_group_sc = (iota((lanes, width), 1) >> jbits) & (Q - 1)
    col_group_st = (iota((lanes, nstate), 1) >> pbits) & (Q - 1)
    row_group_j = iota((lanes, 1), 0) >> jbits

    r_stack = jnp.concatenate([w1_ref[g, 0:J, 0:L * J] for g in range(Q)], axis=0)
    r_big = jnp.dot(r_stack, sel_sc, preferred_element_type=F32)
    r_big = jnp.where(col_group_sc == row_group_j, r_big, 0.0).astype(BF16)
    for s in range(L):
        rows = slice(s * lanes, (s + 1) * lanes)
        if s:
            wbig_ref[0, rows, 0:s * lanes] = jnp.zeros((lanes, s * lanes), BF16)
        wbig_ref[0, rows, s * lanes:width] = r_big[:, 0:width - s * lanes]
        e_stack = jnp.concatenate([w1_ref[g, s * J:(s + 1) * J, L * J:] for g in range(Q)], axis=0)
        e_big = jnp.dot(e_stack, sel_st, preferred_element_type=F32)
        wbig_ref[0, rows, width:] = jnp.where(col_group_st == row_group_j, e_big, 0.0).astype(BF16)

    row_in_pair = iota((lanes, 1), 0) >> pbits
    for m in range(nstate // lanes):
        ri, g0 = divmod(m * (lanes // P), Q)
        f_stack = jnp.concatenate([f_ref[g0 + t, ri * P:(ri + 1) * P, :] for t in range(lanes // P)], axis=0)
        f_big = jnp.dot(f_stack, sel_sc, preferred_element_type=F32)
        fbig_ref[0, m * lanes:(m + 1) * lanes, :] = jnp.where(
            col_group_sc == g0 + row_in_pair, f_big, 0.0).astype(BF16)

    low = iota((V7X_SUBLANES, lanes), 1) < P
    for k in range(Q // 2):
        c0, c1 = coef_ref[2 * k], coef_ref[2 * k + 1]
        row_is_a2 = iota((V7X_SUBLANES, lanes), 0) == 1
        re_half = jnp.where(low, c0, jnp.where(row_is_a2, -c1, c1))
        im_half = jnp.where(low, jnp.where(row_is_a2, -c0, c0), c1)
        cbig_ref[0, :, k * lanes:(k + 1) * lanes] = re_half
        cbig_ref[0, :, Q * P + k * lanes:Q * P + (k + 1) * lanes] = im_half


def _s5_assemble(w1, fmat, coef):
    g = w1.shape[0]
    L, J, P, Q = S5_CHUNK, S5_GROUP, S5_STATE, S5_LANE_GROUPS
    nb = g // Q
    width, nstate = L * Q * J, 2 * Q * P
    grp = lambda *s: pl.BlockSpec((Q,) + s, lambda i: (i, 0, 0))
    out = lambda *s: pl.BlockSpec((1,) + s, lambda i: (i, 0, 0))
    return pl.pallas_call(
        _s5_assemble_kernel,
        grid=(nb,),
        in_specs=[grp(L * J, L * J + 2 * P), grp(2 * P, L * J), grp(V7X_SUBLANES, 2 * P)],
        out_specs=[out(width, width + nstate), out(nstate, width), out(V7X_SUBLANES, nstate)],
        out_shape=[jax.ShapeDtypeStruct((nb, width, width + nstate), BF16),
                   jax.ShapeDtypeStruct((nb, nstate, width), BF16),
                   jax.ShapeDtypeStruct((nb, V7X_SUBLANES, nstate), F32)],
        compiler_params=_params("parallel"),
        name="s5_assemble",
    )(w1, fmat, coef)


def _s5_main_kernel(u_ref, w_ref, f_ref, coef_ref, o_ref, ys_ref, sp_ref, st_ref, *, nchunk, nb):
    width = u_ref.shape[2]
    half = st_ref.shape[1] // 2

    @pl.when(pl.program_id(1) == 0)
    def _():
        st_ref[...] = jnp.zeros_like(st_ref)

    ys_ref[...] = jnp.dot(u_ref[0], w_ref[0], preferred_element_type=F32)
    a1, a2 = coef_ref[0, 0:1, :], coef_ref[0, 1:2, :]

    def body(c, s):
        rows = pl.ds(pl.multiple_of(c * nb, nb), nb)
        sp_ref[rows, :] = s
        swapped = jnp.concatenate([s[:, half:], s[:, :half]], axis=1)
        return a1 * s + a2 * swapped + ys_ref[rows, width:]

    st_ref[...] = lax.fori_loop(0, nchunk, body, st_ref[...], unroll=4)
    y = ys_ref[:, 0:width] + jnp.dot(sp_ref[...].astype(BF16), f_ref[0], preferred_element_type=F32)
    o_ref[0] = y.astype(o_ref.dtype)


def _s5_main(u_c, wbig, fbig, cbig, *, nb, rows_per_step):
    nblk, rows, width = u_c.shape
    nstate = fbig.shape[1]
    nsteps = rows // rows_per_step
    fixed = lambda *s: pl.BlockSpec((1,) + s, lambda i, r: (i, 0, 0))
    moving = pl.BlockSpec((1, rows_per_step, width), lambda i, r: (i, r, 0))
    return pl.pallas_call(
        functools.partial(_s5_main_kernel, nchunk=rows_per_step // nb, nb=nb),
        grid=(nblk, nsteps),
        in_specs=[moving, fixed(width, width + nstate), fixed(nstate, width),
                  fixed(V7X_SUBLANES, nstate)],
        out_specs=moving,
        out_shape=jax.ShapeDtypeStruct((nblk, rows, width), BF16),
        scratch_shapes=[pltpu.VMEM((rows_per_step, width + nstate), F32),
                        pltpu.VMEM((rows_per_step, nstate), F32),
                        pltpu.VMEM((nb, nstate), F32)],
        compiler_params=_params("parallel", "arbitrary"),
        name="s5_main",
    )(u_c, wbig, fbig, cbig)


def _s5_glu_kernel(ys_ref, u_ref, d_ref, w_ref, o_ref):
    y = jax.nn.gelu(ys_ref[...].astype(F32) + d_ref[...] * u_ref[...], approximate=True)
    z = jnp.dot(y.astype(BF16), w_ref[...], preferred_element_type=F32)
    o_ref[...] = (y * jax.nn.sigmoid(z)).astype(o_ref.dtype)


def _s5_glu(y_ssm, h, d_skip, w_glu, *, tm):
    m, n = y_ssm.shape
    return pl.pallas_call(
        _s5_glu_kernel,
        grid=(m // tm,),
        in_specs=[pl.BlockSpec((tm, n), lambda i: (i, 0)),
                  pl.BlockSpec((tm, n), lambda i: (i, 0)),
                  pl.BlockSpec((1, n), lambda i: (0, 0)),
                  pl.BlockSpec((n, n), lambda i: (0, 0))],
        out_specs=pl.BlockSpec((tm, n), lambda i: (i, 0)),
        out_shape=jax.ShapeDtypeStruct((m, n), BF16),
        compiler_params=_params("parallel"),
        name="s5_glu",
    )(y_ssm, h, d_skip.reshape(1, n), w_glu)


def _pad_cols(w, n):
    return jnp.pad(w, ((0, 0), (0, n - w.shape[1])))


def _ffn_block(x_res, x_bf, w_gate, w_up, w_down, layer, ln_g, ln_b, *, final):
    hidden = w_gate.shape[2]
    hpad = -(-hidden // FFN_DOWN_TK) * FFN_DOWN_TK
    a, wd = _ffn_up(x_bf, w_gate, w_up, w_down, layer, tm=1024, tn=512, n_out=hpad)
    z = _matmul_resid([a], wd, x_res, tm=1024, tn=1024, tk=FFN_DOWN_TK, name="ffn_down")
    if final:
        return _layernorm(z, ln_g, ln_b, tm=LN_TM, out_dtype=F32)[0], None
    return _norm_stage(z, ln_g, ln_b)


def _even_layer(x_res, x_bf, bsz, seq, e, w_in, w_out, lam_re, lam_im, log_dt, b_re, b_im, c_re, c_im,
                d_skip, w_glu, w_alpha, b_alpha, norm_g, ln_g, ln_b):
    n = bsz * seq
    s5w = d_skip.shape[0]
    groups = s5w // S5_GROUP
    kw = GLA_HEADS * GLA_DK
    vw = GLA_HEADS * GLA_DV
    main = s5w + 2 * kw + 2 * vw
    w1, fmat, coef, *maybe_x_bf = _s5_prep(lam_re, lam_im, log_dt, b_re, b_im, c_re, c_im,
                                           cast_src=x_res if x_bf is None else None)
    (x_bf,) = maybe_x_bf or (x_bf,)
    h = _matmul(x_bf, w_in, e, n=main, tm=1024, tn=IN_PROJ_TN, out_dtype=F32, name="even_in")
    alr = _matmul(x_bf, _pad_cols(w_in[e, :, main:], V7X_LANES)[None], 0, n=V7X_LANES, tm=1024,
                  tn=V7X_LANES, out_dtype=F32, name="even_in_gate")

    wbig, fbig, cbig = _s5_assemble(w1, fmat, coef)
    nchunk = seq // S5_CHUNK
    nblk = s5w // V7X_LANES
    u_c = (h[:, :s5w].astype(BF16).reshape(bsz, nchunk, S5_CHUNK, nblk, V7X_LANES)
           .transpose(3, 1, 0, 2, 4).reshape(nblk, nchunk * bsz, S5_CHUNK * V7X_LANES))
    y_c = _s5_main(u_c, wbig, fbig, cbig, nb=bsz, rows_per_step=256)
    y_ssm = (y_c.reshape(nblk, nchunk, bsz, S5_CHUNK, V7X_LANES)
             .transpose(2, 1, 3, 0, 4).reshape(n, s5w))
    y_s5 = _s5_glu(y_ssm, h, d_skip, w_glu.astype(BF16), tm=512)

    wa = jnp.pad(w_alpha, ((0, V7X_LANES - w_alpha.shape[0]), (0, 0))).astype(BF16)
    y_gla = _gla(h, alr, wa, b_alpha.reshape(1, kw), norm_g.reshape(1, GLA_DV), bsz=bsz, seq=seq,
                 col_q=s5w, col_k=s5w + kw, col_v=s5w + 2 * kw, col_g=s5w + 2 * kw + vw,
                 tt=512, hb=4, unroll=2)

    z = _matmul_resid([y_s5, y_gla], w_out.astype(BF16), x_res, tm=1024, tn=1024, tk=w_out.shape[0],
                      name="even_out")
    return _norm_stage(z, ln_g, ln_b)


def _odd_layer(x_res, x_bf, bsz, seq, layer, o, w_in, w_out, lb_table, norm_g, ln_g, ln_b):
    heads = w_out.shape[0] // HG_DV
    h = _matmul(x_bf, w_in, o, n=w_in.shape[2], tm=1024, tn=IN_PROJ_TN, out_dtype=F32, name="odd_in")
    y = _hgrn(h, lb_table, norm_g.reshape(1, HG_DV), layer=layer, bsz=bsz, seq=seq, heads=heads,
              tt=512, hb=8, unroll=2)
    z = _matmul_resid([y], w_out.astype(BF16), x_res, tm=1024, tn=1024, tk=w_out.shape[0], name="odd_out")
    return _norm_stage(z, ln_g, ln_b)


def kernel(x, ev_w_in, ev_w_out, s5_lam_re, s5_lam_im, s5_log_dt, s5_b_re, s5_b_im, s5_c_re, s5_c_im,
           s5_d, s5_w_glu, gla_w_alpha, gla_b_alpha, gla_norm_g, od_w_in, od_w_out, hg_lb_table,
           hg_norm_g, ln_mix_g, ln_mix_b, ln_ffn_g, ln_ffn_b, ffn_w_gate, ffn_w_up, ffn_w_down):
    bsz, seq, d = x.shape
    depth = ln_mix_g.shape[0]
    assert depth == DEPTH
    xf = x.reshape(bsz * seq, d)
    xb = None
    for layer in range(depth):
        if layer % 2 == 0:
            e = layer // 2
            xf, xb = _even_layer(xf, xb, bsz, seq, e, ev_w_in, ev_w_out[e], s5_lam_re[e], s5_lam_im[e],
                                 s5_log_dt[e], s5_b_re[e], s5_b_im[e], s5_c_re[e], s5_c_im[e], s5_d[e],
                                 s5_w_glu[e], gla_w_alpha[e], gla_b_alpha[e], gla_norm_g[e],
                                 ln_mix_g[layer], ln_mix_b[layer])
        else:
            o = layer // 2
            xf, xb = _odd_layer(xf, xb, bsz, seq, layer, o, od_w_in, od_w_out[o], hg_lb_table,
                                hg_norm_g[o], ln_mix_g[layer], ln_mix_b[layer])
        xf, xb = _ffn_block(xf, xb, ffn_w_gate, ffn_w_up, ffn_w_down, layer,
                            ln_ffn_g[layer], ln_ffn_b[layer], final=layer == depth - 1)
    return xf.reshape(bsz, seq, d).astype(x.dtype)
```

```python
import functools
import math
from typing import NamedTuple

import jax
import jax.numpy as jnp
from jax import lax
from jax.experimental import pallas as pl
from jax.experimental.pallas import tpu as pltpu

F32 = jnp.float32
BF16 = jnp.bfloat16

DEPTH = 2
S5_GROUP = 16
S5_STATE = 64
GLA_HEADS = 8
GLA_DK = 128
GLA_DV = 256
GLA_GATE_RANK = 16
GLA_TAU = 16.0
HG_DK = 128
HG_DV = 128
CHUNK = 64
DEEPNORM_ALPHA = (2.0 * DEPTH) ** 0.25
NORM_EPS = 1e-5

V7X_LANES = 128
V7X_SUBLANES = 8
V7X_VMEM_LIMIT_BYTES = 60 * 1024 * 1024
V7X_VMEM_COMPILER_TEMP_BYTES = 12 * 1024 * 1024

MM_ROW_CHUNK = 256
IN_PROJ_TN = 1024
FFN_DOWN_TK = 2816
LN_TM = 256
LN_COL_BLOCK = 1024
LN_ROW_BLOCK = 16

S5_CHUNK = 16
S5_ROW = S5_CHUNK * S5_GROUP
S5_LANE_GROUPS = V7X_LANES // S5_GROUP


def _nbytes(shape, dtype):
    return math.prod(shape) * jnp.dtype(dtype).itemsize


def _params(*sem, buffers=None):
    limit = V7X_VMEM_LIMIT_BYTES
    if buffers is not None:
        need = sum(_nbytes(shape, dtype) * count for shape, dtype, count in buffers)
        limit = min(limit, need + V7X_VMEM_COMPILER_TEMP_BYTES)
    return pltpu.CompilerParams(dimension_semantics=sem, vmem_limit_bytes=limit)


def _stage_weight_tile(w_hbm, stage_ref, wb_ref, sem, *, layer, n_valid, nj):
    j = pl.program_id(0)
    tn = stage_ref.shape[1]
    last = nj - 1
    last_w = n_valid - last * tn

    def full_copy(jj):
        cols = pl.ds(pl.multiple_of(jj * tn, tn), tn)
        return pltpu.make_async_copy(w_hbm.at[layer, :, cols], stage_ref, sem)

    def last_copy():
        return pltpu.make_async_copy(w_hbm.at[layer, :, pl.ds(last * tn, last_w)],
                                     stage_ref.at[:, pl.ds(0, last_w)], sem)

    def start(jj):
        if last_w == tn:
            full_copy(jj).start()
        else:
            pl.when(jj < last)(lambda: full_copy(jj).start())
            pl.when(jj == last)(lambda: last_copy().start())

    @pl.when(pl.program_id(1) == 0)
    def _():
        pl.when(j == 0)(lambda: start(j))
        if last_w == tn:
            full_copy(j).wait()
            wb_ref[...] = stage_ref[...].astype(BF16)
        else:
            @pl.when(j < last)
            def _():
                full_copy(j).wait()
                wb_ref[...] = stage_ref[...].astype(BF16)

            @pl.when(j == last)
            def _():
                last_copy().wait()
                wb_ref[:, 0:last_w] = stage_ref[:, 0:last_w].astype(BF16)
                wb_ref[:, last_w:] = jnp.zeros((wb_ref.shape[0], tn - last_w), BF16)

        pl.when(j < last)(lambda: start(j + 1))


def _mm_kernel(a_ref, w_hbm, o_ref, stage_ref, wb_ref, sem, *, layer, n_valid, nj):
    _stage_weight_tile(w_hbm, stage_ref, wb_ref, sem, layer=layer, n_valid=n_valid, nj=nj)
    w = wb_ref[...]
    for r0 in range(0, a_ref.shape[0], MM_ROW_CHUNK):
        rows = slice(r0, r0 + MM_ROW_CHUNK)
        o_ref[rows, :] = jnp.dot(a_ref[rows, :], w, preferred_element_type=F32).astype(o_ref.dtype)


def _matmul(a, w, layer, *, n, tm, tn, out_dtype, name):
    m, k = a.shape
    nj = n // tn
    return pl.pallas_call(
        functools.partial(_mm_kernel, layer=layer, n_valid=n, nj=nj),
        grid=(nj, m // tm),
        in_specs=[pl.BlockSpec((tm, k), lambda j, i: (i, 0)),
                  pl.BlockSpec(memory_space=pl.ANY)],
        out_specs=pl.BlockSpec((tm, tn), lambda j, i: (i, j)),
        out_shape=jax.ShapeDtypeStruct((m, n), out_dtype),
        scratch_shapes=[pltpu.VMEM((k, tn), F32), pltpu.VMEM((k, tn), BF16), pltpu.SemaphoreType.DMA],
        compiler_params=_params("arbitrary", "arbitrary", buffers=[
            ((tm, k), BF16, 2), ((k, tn), F32, 1), ((k, tn), BF16, 1), ((tm, tn), out_dtype, 2)]),
        name=name,
    )(a, w)


def _ffn_up_kernel(x_ref, wg_hbm, wu_hbm, wd_ref, o_ref, wdb_ref, sg_ref, su_ref, wgb_ref, wub_ref, sems,
                   *, layer, n_valid, nj, wd_blocks):
    _stage_weight_tile(wg_hbm, sg_ref, wgb_ref, sems.at[0], layer=layer, n_valid=n_valid, nj=nj)
    _stage_weight_tile(wu_hbm, su_ref, wub_ref, sems.at[1], layer=layer, n_valid=n_valid, nj=nj)
    tn = o_ref.shape[1]
    last = nj - 1
    last_w = n_valid - last * tn

    chunk = min(MM_ROW_CHUNK, x_ref.shape[0])

    def tile(width):
        wg, wu = wgb_ref[:, 0:width], wub_ref[:, 0:width]
        for r0 in range(0, x_ref.shape[0], chunk):
            rows = slice(r0, r0 + chunk)
            x = x_ref[rows, :]
            g = jnp.dot(x, wg, preferred_element_type=F32)
            u = jnp.dot(x, wu, preferred_element_type=F32)
            o_ref[rows, 0:width] = (g * jax.nn.sigmoid(g) * u).astype(o_ref.dtype)
            if width < tn:
                o_ref[rows, width:] = jnp.zeros((chunk, tn - width), o_ref.dtype)

    if last_w == tn:
        tile(tn)
    else:
        pl.when(pl.program_id(0) < last)(lambda: tile(tn))
        pl.when(pl.program_id(0) == last)(lambda: tile(last_w))
    step = pl.program_id(0) * pl.num_programs(1) + pl.program_id(1)
    wdb_ref[...] = jnp.where(step < wd_blocks, wd_ref[...], 0.0).astype(wdb_ref.dtype)


def _ffn_up(x_bf, wg, wu, wd, layer, *, tm, tn, n_out):
    m, k = x_bf.shape
    hidden, dout = wd.shape[1:]
    nj, ni = n_out // tn, m // tm
    slab = n_out // (nj * ni)
    assert slab * nj * ni == n_out and hidden % slab == 0 and slab % (2 * V7X_SUBLANES) == 0
    wd_blocks = hidden // slab
    hbm = pl.BlockSpec(memory_space=pl.ANY)
    return pl.pallas_call(
        functools.partial(_ffn_up_kernel, layer=layer, n_valid=wg.shape[2], nj=nj, wd_blocks=wd_blocks),
        grid=(nj, ni),
        in_specs=[pl.BlockSpec((tm, k), lambda j, i: (i, 0)), hbm, hbm,
                  pl.BlockSpec((None, slab, dout),
                               lambda j, i: (layer, jnp.minimum(j * ni + i, wd_blocks - 1), 0))],
        out_specs=[pl.BlockSpec((tm, tn), lambda j, i: (i, j)),
                   pl.BlockSpec((slab, dout), lambda j, i: (j * ni + i, 0))],
        out_shape=[jax.ShapeDtypeStruct((m, n_out), BF16), jax.ShapeDtypeStruct((n_out, dout), BF16)],
        scratch_shapes=[pltpu.VMEM((k, tn), F32), pltpu.VMEM((k, tn), F32),
                        pltpu.VMEM((k, tn), BF16), pltpu.VMEM((k, tn), BF16),
                        pltpu.SemaphoreType.DMA((2,))],
        compiler_params=_params("arbitrary", "arbitrary", buffers=[
            ((tm, k), BF16, 2), ((k, tn), F32, 2), ((k, tn), BF16, 2), ((tm, tn), BF16, 2),
            ((slab, dout), F32, 2), ((slab, dout), BF16, 2)]),
        name="ffn_up",
    )(x_bf, wg, wu, wd)


def _mm_resid_kernel(*refs, k_parts, nk):
    n_a = len(k_parts)
    a_refs = refs[:n_a]
    w_ref, *r_refs, o_ref = refs[n_a:]
    k = pl.program_id(2)

    def product():
        acc, k0 = None, 0
        for a_ref, kp in zip(a_refs, k_parts):
            d = jnp.dot(a_ref[...], w_ref[k0:k0 + kp, :], preferred_element_type=F32)
            acc = d if acc is None else acc + d
            k0 += kp
        return acc

    def residual():
        if len(r_refs) == 1:
            return r_refs[0][...]
        z_ref, mu_ref, rs_ref, g_ref, b_ref = r_refs
        nrep = z_ref.shape[1] // mu_ref.shape[1]
        mu, rs = jnp.tile(mu_ref[...], (1, nrep)), jnp.tile(rs_ref[...], (1, nrep))
        return (z_ref[...] - mu) * rs * g_ref[...] + b_ref[...]

    if nk == 1:
        o_ref[...] = product() + DEEPNORM_ALPHA * residual()
    else:
        @pl.when(k == 0)
        def _():
            o_ref[...] = product() + DEEPNORM_ALPHA * residual()

        @pl.when(k > 0)
        def _():
            o_ref[...] += product()


class _NormedRows(NamedTuple):
    z: jax.Array
    mean: jax.Array
    rstd: jax.Array
    gain: jax.Array
    bias: jax.Array


def _matmul_resid(a_parts, w, resid, *, tm, tn, tk, name):
    tile = pl.BlockSpec((tm, tn), lambda i, j, k: (i, j))
    if isinstance(resid, _NormedRows):
        m, n = resid.z.shape
        stat = pl.BlockSpec((tm, V7X_LANES), lambda i, j, k: (i, 0))
        vec = pl.BlockSpec((1, tn), lambda i, j, k: (0, j))
        r_ops = [resid.z, resid.mean, resid.rstd, resid.gain.reshape(1, n), resid.bias.reshape(1, n)]
        r_specs = [tile, stat, stat, vec, vec]
    else:
        m, n = resid.shape
        r_ops, r_specs = [resid], [tile]
    kdim = w.shape[0]
    if len(a_parts) > 1:
        assert tk == kdim
        k_parts = tuple(a.shape[1] for a in a_parts)
        a_specs = [pl.BlockSpec((tm, kp), lambda i, j, k: (i, 0)) for kp in k_parts]
    else:
        k_parts = (tk,)
        a_specs = [pl.BlockSpec((tm, tk), lambda i, j, k: (i, k))]
    nk = kdim // tk
    return pl.pallas_call(
        functools.partial(_mm_resid_kernel, k_parts=k_parts, nk=nk),
        grid=(m // tm, n // tn, nk),
        in_specs=a_specs + [pl.BlockSpec((tk, tn), lambda i, j, k: (k, j))] + r_specs,
        out_specs=tile,
        out_shape=jax.ShapeDtypeStruct((m, n), F32),
        compiler_params=_params("parallel", "parallel", "arbitrary"),
        name=name,
    )(*a_parts, w, *r_ops)


def _ln_kernel(z_ref, g_ref, b_ref, y_ref, mu_ref, rs_ref):
    inv_n = 1.0 / z_ref.shape[1]
    nblk = z_ref.shape[0] // LN_ROW_BLOCK
    lanes = mu_ref.shape[1]
    nrep = z_ref.shape[1] // lanes

    def block(r):
        return pl.ds(pl.multiple_of(r * LN_ROW_BLOCK, LN_ROW_BLOCK), LN_ROW_BLOCK)

    def mean_rows(r, carry):
        rows = block(r)
        mu = jnp.sum(z_ref[rows, :], axis=-1, keepdims=True) * inv_n
        mu_ref[rows, :] = jnp.broadcast_to(mu, (LN_ROW_BLOCK, lanes))
        return carry

    def rstd_rows(r, carry):
        rows = block(r)
        zc = z_ref[rows, :] - jnp.tile(mu_ref[rows, :], (1, nrep))
        var = jnp.sum(zc * zc, axis=-1, keepdims=True) * inv_n
        rs_ref[rows, :] = jnp.broadcast_to(lax.rsqrt(var + NORM_EPS), (LN_ROW_BLOCK, lanes))
        return carry

    def norm_rows(r, carry):
        rows = block(r)
        mu = jnp.tile(mu_ref[rows, :], (1, LN_COL_BLOCK // lanes))
        rs = jnp.tile(rs_ref[rows, :], (1, LN_COL_BLOCK // lanes))
        for c0 in range(0, z_ref.shape[1], LN_COL_BLOCK):
            cols = slice(c0, c0 + LN_COL_BLOCK)
            y = (z_ref[rows, cols] - mu) * rs * g_ref[:, cols] + b_ref[:, cols]
            y_ref[rows, cols] = y.astype(y_ref.dtype)
        return carry

    lax.fori_loop(0, nblk, mean_rows, 0, unroll=4)
    lax.fori_loop(0, nblk, rstd_rows, 0, unroll=4)
    lax.fori_loop(0, nblk, norm_rows, 0, unroll=2)


def _layernorm(z, gain, bias, *, tm, out_dtype):
    m, n = z.shape
    row_spec = pl.BlockSpec((tm, n), lambda i: (i, 0))
    vec_spec = pl.BlockSpec((1, n), lambda i: (0, 0))
    stat_spec = pl.BlockSpec((tm, V7X_LANES), lambda i: (i, 0))
    stat_shape = jax.ShapeDtypeStruct((m, V7X_LANES), F32)
    return pl.pallas_call(
        _ln_kernel,
        grid=(m // tm,),
        in_specs=[row_spec, vec_spec, vec_spec],
        out_specs=[row_spec, stat_spec, stat_spec],
        out_shape=[jax.ShapeDtypeStruct((m, n), out_dtype), stat_shape, stat_shape],
        compiler_params=_params("parallel"),
        name="layernorm",
    )(z, gain.reshape(1, n), bias.reshape(1, n))


def _norm_stage(z, gain, bias):
    y_bf, mean, rstd = _layernorm(z, gain, bias, tm=LN_TM, out_dtype=BF16)
    return _NormedRows(z, mean, rstd, gain, bias), y_bf


def _chunk_steps(qs, ks, vs, las, st_ref, tril_bf, causal):
    c = qs[0].shape[0]
    nt = (((1,), (1,)), ((), ()))
    tn = (((0,), (0,)), ((), ()))
    heads = range(len(qs))
    bs = []
    for la in las:
        la_hi = la.astype(BF16)
        la_lo = (la - la_hi.astype(F32)).astype(BF16)
        bs.append(jnp.dot(tril_bf, la_hi, preferred_element_type=F32)
                  + jnp.dot(tril_bf, la_lo, preferred_element_type=F32))
    q_decs, scores, kvs, inters, decays = [], [], [], [], []
    for hd in heads:
        b = bs[hd]
        b_end = b[c - 1:c, :]
        decay = jnp.exp(b_end)
        q_dec = (qs[hd] * jnp.exp(b)).astype(BF16)
        k_inv = ks[hd] * jnp.exp(-b)
        k_end = (k_inv * decay).astype(BF16)
        k_inv = k_inv.astype(BF16)
        v_bf = vs[hd].astype(BF16)
        vs[hd] = v_bf
        scores.append(lax.dot_general(q_dec, k_inv, nt, preferred_element_type=F32))
        kvs.append(lax.dot_general(v_bf, k_end, tn, preferred_element_type=F32))
        inters.append(lax.dot_general(q_dec, st_ref[hd].astype(BF16), nt, preferred_element_type=F32))
        decays.append(decay)
    outs = []
    for hd in heads:
        sc = jnp.where(causal, scores[hd], 0.0).astype(BF16)
        outs.append(jnp.dot(sc, vs[hd], preferred_element_type=F32) + inters[hd])
        st_ref[hd] = decays[hd] * st_ref[hd] + kvs[hd]
    return outs


def _gated_rmsnorm(o, gain, gate):
    ms = jnp.mean(o * o, axis=-1, keepdims=True)
    return o * lax.rsqrt(ms + NORM_EPS) * gain * (gate * jax.nn.sigmoid(gate))


def _chunk_masks():
    row = lax.broadcasted_iota(jnp.int32, (CHUNK, CHUNK), 0)
    col = lax.broadcasted_iota(jnp.int32, (CHUNK, CHUNK), 1)
    causal = row >= col
    return jnp.where(causal, 1.0, 0.0).astype(BF16), causal


def _log_sigmoid(z):
    return jnp.minimum(z, 0.0) - jnp.log1p(jnp.exp(-jnp.abs(z)))


def _gla_kernel(q_ref, k_ref, v_ref, g_ref, alr_ref, wa_ref, ba_ref, gain_ref, o_ref, st_ref,
                *, hb, nchunks, unroll):
    @pl.when(pl.program_id(2) == 0)
    def _():
        st_ref[...] = jnp.zeros_like(st_ref)

    tril_bf, causal = _chunk_masks()
    gain = gain_ref[...]
    ba = ba_ref[...]
    wa = wa_ref[...]

    def body(c, carry):
        rows = pl.ds(pl.multiple_of(c * CHUNK, CHUNK), CHUNK)
        z = jnp.dot(alr_ref[rows, :].astype(BF16), wa, preferred_element_type=F32) + ba
        la_all = _log_sigmoid(z) / GLA_TAU
        kcs = [slice(hd * GLA_DK, (hd + 1) * GLA_DK) for hd in range(hb)]
        vcs = [slice(hd * GLA_DV, (hd + 1) * GLA_DV) for hd in range(hb)]
        outs = _chunk_steps([q_ref[rows, kc] * (GLA_DK ** -0.5) for kc in kcs],
                            [k_ref[rows, kc] for kc in kcs], [v_ref[rows, vc] for vc in vcs],
                            [la_all[:, kc] for kc in kcs], st_ref, tril_bf, causal)
        for o, vc in zip(outs, vcs):
            o_ref[rows, vc] = _gated_rmsnorm(o, gain, g_ref[rows, vc]).astype(o_ref.dtype)
        return carry

    lax.fori_loop(0, nchunks, body, 0, unroll=unroll)


def _gla(h, alr, w_alpha, b_alpha, norm_g, *, bsz, seq, col_q, col_k, col_v, col_g, tt, hb, unroll):
    n = bsz * seq
    nt = seq // tt
    kw, vw = hb * GLA_DK, hb * GLA_DV
    hg = GLA_HEADS // hb

    def rows(b, g, t):
        return b * nt + t

    return pl.pallas_call(
        functools.partial(_gla_kernel, hb=hb, nchunks=tt // CHUNK, unroll=unroll),
        grid=(bsz, hg, nt),
        in_specs=[pl.BlockSpec((tt, kw), lambda b, g, t: (rows(b, g, t), col_q // kw + g)),
                  pl.BlockSpec((tt, kw), lambda b, g, t: (rows(b, g, t), col_k // kw + g)),
                  pl.BlockSpec((tt, vw), lambda b, g, t: (rows(b, g, t), col_v // vw + g)),
                  pl.BlockSpec((tt, vw), lambda b, g, t: (rows(b, g, t), col_g // vw + g)),
                  pl.BlockSpec((tt, V7X_LANES), lambda b, g, t: (rows(b, g, t), 0)),
                  pl.BlockSpec((V7X_LANES, kw), lambda b, g, t: (0, g)),
                  pl.BlockSpec((1, kw), lambda b, g, t: (0, g)),
                  pl.BlockSpec((1, GLA_DV), lambda b, g, t: (0, 0))],
        out_specs=pl.BlockSpec((tt, vw), lambda b, g, t: (rows(b, g, t), g)),
        out_shape=jax.ShapeDtypeStruct((n, GLA_HEADS * GLA_DV), BF16),
        scratch_shapes=[pltpu.VMEM((hb, GLA_DV, GLA_DK), F32)],
        compiler_params=_params("parallel", "parallel", "arbitrary"),
        name="gla",
    )(h, h, h, h, alr, w_alpha, b_alpha, norm_g)


def _hgrn_kernel(q_ref, f_ref, i_ref, g_ref, tab_ref, gain_ref, o_ref, st_ref, *, layer, hb, nchunks,
                 unroll):
    @pl.when(pl.program_id(2) == 0)
    def _():
        st_ref[...] = jnp.zeros_like(st_ref)

    tab = tab_ref[...]
    e = jnp.exp(tab - jnp.max(tab, axis=0, keepdims=True))
    soft = e / jnp.sum(e, axis=0, keepdims=True)
    lb_all = jnp.sum(soft[0:layer + 1], axis=0, keepdims=True) - soft[0:1]
    tril_bf, causal = _chunk_masks()
    gain = gain_ref[...]

    def body(c, carry):
        rows = pl.ds(pl.multiple_of(c * CHUNK, CHUNK), CHUNK)
        kcs = [slice(hd * HG_DK, (hd + 1) * HG_DK) for hd in range(hb)]
        vcs = [slice(hd * HG_DV, (hd + 1) * HG_DV) for hd in range(hb)]
        fs = [lb_all[:, kc] + (1.0 - lb_all[:, kc]) * jax.nn.sigmoid(f_ref[rows, kc]) for kc in kcs]
        ivs = [i_ref[rows, vc] for vc in vcs]
        outs = _chunk_steps([q_ref[rows, kc] for kc in kcs], [1.0 - f for f in fs],
                            [iv * jax.nn.sigmoid(iv) for iv in ivs], [jnp.log(f) for f in fs],
                            st_ref, tril_bf, causal)
        for o, vc in zip(outs, vcs):
            o_ref[rows, vc] = _gated_rmsnorm(o, gain, g_ref[rows, vc]).astype(o_ref.dtype)
        return carry

    lax.fori_loop(0, nchunks, body, 0, unroll=unroll)


def _hgrn(h, lb_table, norm_g, *, layer, bsz, seq, heads, tt, hb, unroll):
    n = bsz * seq
    nt = seq // tt
    w = hb * HG_DK
    hg = heads // hb

    def spec(seg):
        return pl.BlockSpec((tt, w), lambda b, g, t: (b * nt + t, seg * hg + g))

    return pl.pallas_call(
        functools.partial(_hgrn_kernel, layer=layer, hb=hb, nchunks=tt // CHUNK, unroll=unroll),
        grid=(bsz, hg, nt),
        in_specs=[spec(0), spec(1), spec(2), spec(3),
                  pl.BlockSpec((DEPTH, w), lambda b, g, t: (0, g)),
                  pl.BlockSpec((1, HG_DV), lambda b, g, t: (0, 0))],
        out_specs=pl.BlockSpec((tt, w), lambda b, g, t: (b * nt + t, g)),
        out_shape=jax.ShapeDtypeStruct((n, heads * HG_DV), BF16),
        scratch_shapes=[pltpu.VMEM((hb, HG_DV, HG_DK), F32)],
        compiler_params=_params("parallel", "parallel", "arbitrary"),
        name="hgrn2",
    )(h, h, h, h, lb_table, norm_g)


def _s5_prep_cast_kernel(lr_row_ref, li_row_ref, lr_col_ref, li_col_ref, ldt_ref,
                         btr_ref, bti_ref, ctr_ref, cti_ref, src_ref, wgate_ref,
                         w1_ref, f_ref, coef_ref, dst_ref, gate_ref):
    _s5_prep_kernel(lr_row_ref, li_row_ref, lr_col_ref, li_col_ref, ldt_ref,
                    btr_ref, bti_ref, ctr_ref, cti_ref, w1_ref, f_ref, coef_ref)
    x = src_ref[...].astype(dst_ref.dtype)
    dst_ref[...] = x
    gate_ref[...] = jnp.dot(x, wgate_ref[...], preferred_element_type=F32)


def _s5_prep_kernel(lr_row_ref, li_row_ref, lr_col_ref, li_col_ref, ldt_ref,
                    btr_ref, bti_ref, ctr_ref, cti_ref, w1_ref, f_ref, coef_ref):
    L, J, P = S5_CHUNK, S5_GROUP, S5_STATE
    hi = lax.Precision.HIGHEST
    dt = jnp.exp(ldt_ref[0])
    lr, li = lr_row_ref[0], li_row_ref[0]
    npow = ((L - 1) - lax.broadcasted_iota(jnp.int32, (L, 2 * P), 0)).astype(F32)
    mag = jnp.exp(npow * (lr * dt))
    th = npow * (li * dt)
    ar, ai = mag * jnp.cos(th), mag * jnp.sin(th)
    half = lax.broadcasted_iota(jnp.int32, (L, 2 * P), 1) < P
    a1r, a1i = ar[L - 2:L - 1], ai[L - 2:L - 1]
    nr, ni = a1r - 1.0, a1i
    den = lr * lr + li * li
    fr, fi = (nr * lr + ni * li) / den, (ni * lr - nr * li) / den
    btr, bti = btr_ref[0], bti_ref[0]
    bbr, bbi = fr * btr - fi * bti, fr * bti + fi * btr
    pa, pb = jnp.where(half, ar, ai), jnp.where(half, -ai, ar)
    for s in range(L):
        rows = slice(s * J, (s + 1) * J)
        w1_ref[0, rows, L * J:] = (bbr * pa[s:s + 1] + bbi * pb[s:s + 1]).astype(BF16)
    alr_, ali_ = ar[0:1] * a1r - ai[0:1] * a1i, ar[0:1] * a1i + ai[0:1] * a1r
    half1 = half[0:1]
    coef_ref[0, 0:1, :] = alr_
    coef_ref[0, 1:2, :] = jnp.where(half1, -ali_, ali_)
    coef_ref[0, 2:, :] = jnp.zeros((V7X_SUBLANES - 2, 2 * P), F32)

    lrc, lic = lr_col_ref[0], li_col_ref[0]
    lane = lax.broadcasted_iota(jnp.int32, (P, L * J), 1)
    lag = jnp.right_shift(lane, int(math.log2(J))).astype(F32)
    magq = jnp.exp(lag * (lrc * dt))
    thq = lag * (lic * dt)
    aqr, aqi = magq * jnp.cos(thq), magq * jnp.sin(thq)
    cr, ci = ctr_ref[0], cti_ref[0]
    qr, qi = cr * aqr - ci * aqi, cr * aqi + ci * aqr
    r = (jnp.dot(bbr[:, :P], qr, precision=hi, preferred_element_type=F32)
         - jnp.dot(bbi[:, :P], qi, precision=hi, preferred_element_type=F32))
    lane_r = lax.broadcasted_iota(jnp.int32, (J, L * J), 1)
    for s in range(L):
        blk = r if s == 0 else jnp.where(lane_r >= s * J, pltpu.roll(r, s * J, 1), 0.0)
        w1_ref[0, s * J:(s + 1) * J, 0:L * J] = blk.astype(BF16)
    m1 = jnp.exp(lrc * dt)
    c1r, c1i = m1 * jnp.cos(lic * dt), m1 * jnp.sin(lic * dt)
    f_ref[0, 0:P, :] = (qr * c1r - qi * c1i).astype(BF16)
    f_ref[0, P:, :] = (-(qr * c1i + qi * c1r)).astype(BF16)


def _s5_prep(lam_re, lam_im, log_dt, b_re, b_im, c_re, c_im, cast_src=None, gate_w=None):
    g = lam_re.shape[0]
    P, J, L = S5_STATE, S5_GROUP, S5_CHUNK
    dbl = lambda a: jnp.concatenate([a, a], axis=-1)
    lr_row = dbl(lam_re).reshape(g, 1, 2 * P)
    li_row = dbl(lam_im).reshape(g, 1, 2 * P)
    lr_col = lam_re.reshape(g, P, 1)
    li_col = lam_im.reshape(g, P, 1)
    ldt = log_dt.reshape(g, 1, 1)
    btr = dbl(jnp.swapaxes(b_re, 1, 2))
    bti = dbl(jnp.swapaxes(b_im, 1, 2))
    ctr = jnp.tile(jnp.swapaxes(c_re, 1, 2), (1, 1, L))
    cti = jnp.tile(jnp.swapaxes(c_im, 1, 2), (1, 1, L))
    blk = lambda *s: pl.BlockSpec((1,) + s, lambda i: (i, 0, 0))
    operands = [lr_row, li_row, lr_col, li_col, ldt, btr, bti, ctr, cti]
    in_specs = [blk(1, 2 * P), blk(1, 2 * P), blk(P, 1), blk(P, 1), blk(1, 1),
                blk(J, 2 * P), blk(J, 2 * P), blk(P, L * J), blk(P, L * J)]
    out_specs = [blk(L * J, L * J + 2 * P), blk(2 * P, L * J), blk(V7X_SUBLANES, 2 * P)]
    out_shape = [jax.ShapeDtypeStruct((g, L * J, L * J + 2 * P), BF16),
                 jax.ShapeDtypeStruct((g, 2 * P, L * J), BF16),
                 jax.ShapeDtypeStruct((g, V7X_SUBLANES, 2 * P), F32)]
    body = _s5_prep_kernel
    if cast_src is not None:
        m, d = cast_src.shape
        slab = pl.BlockSpec((m // g, d), lambda i: (i, 0))
        gate_slab = pl.BlockSpec((m // g, gate_w.shape[1]), lambda i: (i, 0))
        operands += [cast_src, gate_w]
        in_specs += [slab, pl.BlockSpec(gate_w.shape, lambda i: (0, 0))]
        out_specs += [slab, gate_slab]
        out_shape += [jax.ShapeDtypeStruct((m, d), BF16), jax.ShapeDtypeStruct((m, gate_w.shape[1]), F32)]
        body = _s5_prep_cast_kernel
    return pl.pallas_call(body, grid=(g,), in_specs=in_specs, out_specs=out_specs, out_shape=out_shape,
                          compiler_params=_params("parallel"), name="s5_prep")(*operands)


def _s5_assemble_kernel(w1_ref, f_ref, coef_ref, wbig_ref, fbig_ref, cbig_ref):
    L, J, P, Q = S5_CHUNK, S5_GROUP, S5_STATE, S5_LANE_GROUPS
    lanes = Q * J
    width = L * lanes
    nstate = 2 * Q * P
    jbits, pbits = int(math.log2(J)), int(math.log2(P))

    def iota(shape, axis):
        return lax.broadcasted_iota(jnp.int32, shape, axis)

    a, b = iota((L * J, width), 0), iota((L * J, width), 1)
    sel_sc = jnp.where(((b >> int(math.log2(lanes))) == (a >> jbits)) & ((b & (J - 1)) == (a & (J - 1))),
                       1.0, 0.0).astype(BF16)
    a, b = iota((2 * P, nstate), 0), iota((2 * P, nstate), 1)
    sel_st = jnp.where(((b >> int(math.log2(Q * P))) == (a >> pbits)) & ((b & (P - 1)) == (a & (P - 1))),
                       1.0, 0.0).astype(BF16)
    col_group_sc = (iota((lanes, width), 1) >> jbits) & (Q - 1)
    col_group_st = (iota((lanes, nstate), 1) >> pbits) & (Q - 1)
    row_group_j = iota((lanes, 1), 0) >> jbits

    r_stack = jnp.concatenate([w1_ref[g, 0:J, 0:L * J] for g in range(Q)], axis=0)
    r_big = jnp.dot(r_stack, sel_sc, preferred_element_type=F32)
    r_big = jnp.where(col_group_sc == row_group_j, r_big, 0.0).astype(BF16)
    for s in range(L):
        rows = slice(s * lanes, (s + 1) * lanes)
        if s:
            wbig_ref[0, rows, 0:s * lanes] = jnp.zeros((lanes, s * lanes), BF16)
        wbig_ref[0, rows, s * lanes:width] = r_big[:, 0:width - s * lanes]
        e_stack = jnp.concatenate([w1_ref[g, s * J:(s + 1) * J, L * J:] for g in range(Q)], axis=0)
        e_big = jnp.dot(e_stack, sel_st, preferred_element_type=F32)
        wbig_ref[0, rows, width:] = jnp.where(col_group_st == row_group_j, e_big, 0.0).astype(BF16)

    row_in_pair = iota((lanes, 1), 0) >> pbits
    for m in range(nstate // lanes):
        ri, g0 = divmod(m * (lanes // P), Q)
        f_stack = jnp.concatenate([f_ref[g0 + t, ri * P:(ri + 1) * P, :] for t in range(lanes // P)], axis=0)
        f_big = jnp.dot(f_stack, sel_sc, preferred_element_type=F32)
        fbig_ref[0, m * lanes:(m + 1) * lanes, :] = jnp.where(
            col_group_sc == g0 + row_in_pair, f_big, 0.0).astype(BF16)

    low = iota((V7X_SUBLANES, lanes), 1) < P
    for k in range(Q // 2):
        c0, c1 = coef_ref[2 * k], coef_ref[2 * k + 1]
        row_is_a2 = iota((V7X_SUBLANES, lanes), 0) == 1
        re_half = jnp.where(low, c0, jnp.where(row_is_a2, -c1, c1))
        im_half = jnp.where(low, jnp.where(row_is_a2, -c0, c0), c1)
        cbig_ref[0, :, k * lanes:(k + 1) * lanes] = re_half
        cbig_ref[0, :, Q * P + k * lanes:Q * P + (k + 1) * lanes] = im_half


def _s5_assemble(w1, fmat, coef):
    g = w1.shape[0]
    L, J, P, Q = S5_CHUNK, S5_GROUP, S5_STATE, S5_LANE_GROUPS
    nb = g // Q
    width, nstate = L * Q * J, 2 * Q * P
    grp = lambda *s: pl.BlockSpec((Q,) + s, lambda i: (i, 0, 0))
    out = lambda *s: pl.BlockSpec((1,) + s, lambda i: (i, 0, 0))
    return pl.pallas_call(
        _s5_assemble_kernel,
        grid=(nb,),
        in_specs=[grp(L * J, L * J + 2 * P), grp(2 * P, L * J), grp(V7X_SUBLANES, 2 * P)],
        out_specs=[out(width, width + nstate), out(nstate, width), out(V7X_SUBLANES, nstate)],
        out_shape=[jax.ShapeDtypeStruct((nb, width, width + nstate), BF16),
                   jax.ShapeDtypeStruct((nb, nstate, width), BF16),
                   jax.ShapeDtypeStruct((nb, V7X_SUBLANES, nstate), F32)],
        compiler_params=_params("parallel"),
        name="s5_assemble",
    )(w1, fmat, coef)


def _s5_main_kernel(u_ref, w_ref, f_ref, coef_ref, o_ref, ys_ref, sp_ref, st_ref, *, nchunk, nb):
    width = u_ref.shape[2]
    half = st_ref.shape[1] // 2

    @pl.when(pl.program_id(1) == 0)
    def _():
        st_ref[...] = jnp.zeros_like(st_ref)

    ys_ref[...] = jnp.dot(u_ref[0], w_ref[0], preferred_element_type=F32)
    a1, a2 = coef_ref[0, 0:1, :], coef_ref[0, 1:2, :]

    def body(c, s):
        rows = pl.ds(pl.multiple_of(c * nb, nb), nb)
        sp_ref[rows, :] = s
        swapped = jnp.concatenate([s[:, half:], s[:, :half]], axis=1)
        return a1 * s + a2 * swapped + ys_ref[rows, width:]

    st_ref[...] = lax.fori_loop(0, nchunk, body, st_ref[...], unroll=4)
    y = ys_ref[:, 0:width] + jnp.dot(sp_ref[...].astype(BF16), f_ref[0], preferred_element_type=F32)
    o_ref[0] = y.astype(o_ref.dtype)


def _s5_main(u_c, wbig, fbig, cbig, *, nb, rows_per_step):
    nblk, rows, width = u_c.shape
    nstate = fbig.shape[1]
    nsteps = rows // rows_per_step
    fixed = lambda *s: pl.BlockSpec((1,) + s, lambda i, r: (i, 0, 0))
    moving = pl.BlockSpec((1, rows_per_step, width), lambda i, r: (i, r, 0))
    return pl.pallas_call(
        functools.partial(_s5_main_kernel, nchunk=rows_per_step // nb, nb=nb),
        grid=(nblk, nsteps),
        in_specs=[moving, fixed(width, width + nstate), fixed(nstate, width),
                  fixed(V7X_SUBLANES, nstate)],
        out_specs=moving,
        out_shape=jax.ShapeDtypeStruct((nblk, rows, width), BF16),
        scratch_shapes=[pltpu.VMEM((rows_per_step, width + nstate), F32),
                        pltpu.VMEM((rows_per_step, nstate), F32),
                        pltpu.VMEM((nb, nstate), F32)],
        compiler_params=_params("parallel", "arbitrary"),
        name="s5_main",
    )(u_c, wbig, fbig, cbig)


def _s5_glu_kernel(ys_ref, u_ref, d_ref, w_ref, o_ref):
    y = jax.nn.gelu(ys_ref[...].astype(F32) + d_ref[...] * u_ref[...], approximate=True)
    z = jnp.dot(y.astype(BF16), w_ref[...], preferred_element_type=F32)
    o_ref[...] = (y * jax.nn.sigmoid(z)).astype(o_ref.dtype)


def _s5_glu(y_ssm, h, d_skip, w_glu, *, tm):
    m, n = y_ssm.shape
    return pl.pallas_call(
        _s5_glu_kernel,
        grid=(m // tm,),
        in_specs=[pl.BlockSpec((tm, n), lambda i: (i, 0)),
                  pl.BlockSpec((tm, n), lambda i: (i, 0)),
                  pl.BlockSpec((1, n), lambda i: (0, 0)),
                  pl.BlockSpec((n, n), lambda i: (0, 0))],
        out_specs=pl.BlockSpec((tm, n), lambda i: (i, 0)),
        out_shape=jax.ShapeDtypeStruct((m, n), BF16),
        compiler_params=_params("parallel"),
        name="s5_glu",
    )(y_ssm, h, d_skip.reshape(1, n), w_glu)


def _pad_cols(w, n):
    return jnp.pad(w, ((0, 0), (0, n - w.shape[1])))


def _ffn_block(x_res, x_bf, w_gate, w_up, w_down, layer, ln_g, ln_b, *, final):
    hidden = w_gate.shape[2]
    hpad = -(-hidden // FFN_DOWN_TK) * FFN_DOWN_TK
    a, wd = _ffn_up(x_bf, w_gate, w_up, w_down, layer, tm=1024, tn=512, n_out=hpad)
    z = _matmul_resid([a], wd, x_res, tm=1024, tn=1024, tk=FFN_DOWN_TK, name="ffn_down")
    if final:
        return _layernorm(z, ln_g, ln_b, tm=LN_TM, out_dtype=F32)[0], None
    return _norm_stage(z, ln_g, ln_b)


def _even_layer(x_res, x_bf, bsz, seq, e, w_in, w_out, lam_re, lam_im, log_dt, b_re, b_im, c_re, c_im,
                d_skip, w_glu, w_alpha, b_alpha, norm_g, ln_g, ln_b):
    n = bsz * seq
    s5w = d_skip.shape[0]
    groups = s5w // S5_GROUP
    kw = GLA_HEADS * GLA_DK
    vw = GLA_HEADS * GLA_DV
    main = s5w + 2 * kw + 2 * vw
    w_gate = _pad_cols(w_in[e, :, main:], V7X_LANES)
    if x_bf is None:
        w1, fmat, coef, x_bf, alr = _s5_prep(lam_re, lam_im, log_dt, b_re, b_im, c_re, c_im,
                                             cast_src=x_res, gate_w=w_gate.astype(BF16))
    else:
        w1, fmat, coef = _s5_prep(lam_re, lam_im, log_dt, b_re, b_im, c_re, c_im)
        alr = _matmul(x_bf, w_gate[None], 0, n=V7X_LANES, tm=1024, tn=V7X_LANES, out_dtype=F32,
                      name="even_in_gate")
    h = _matmul(x_bf, w_in, e, n=main, tm=1024, tn=IN_PROJ_TN, out_dtype=F32, name="even_in")

    wbig, fbig, cbig = _s5_assemble(w1, fmat, coef)
    nchunk = seq // S5_CHUNK
    nblk = s5w // V7X_LANES
    u_c = (h[:, :s5w].astype(BF16).reshape(bsz, nchunk, S5_CHUNK, nblk, V7X_LANES)
           .transpose(3, 1, 0, 2, 4).reshape(nblk, nchunk * bsz, S5_CHUNK * V7X_LANES))
    y_c = _s5_main(u_c, wbig, fbig, cbig, nb=bsz, rows_per_step=256)
    y_ssm = (y_c.reshape(nblk, nchunk, bsz, S5_CHUNK, V7X_LANES)
             .transpose(2, 1, 3, 0, 4).reshape(n, s5w))
    y_s5 = _s5_glu(y_ssm, h, d_skip, w_glu.astype(BF16), tm=512)

    wa = jnp.pad(w_alpha, ((0, V7X_LANES - w_alpha.shape[0]), (0, 0))).astype(BF16)
    y_gla = _gla(h, alr, wa, b_alpha.reshape(1, kw), norm_g.reshape(1, GLA_DV), bsz=bsz, seq=seq,
                 col_q=s5w, col_k=s5w + kw, col_v=s5w + 2 * kw, col_g=s5w + 2 * kw + vw,
                 tt=512, hb=8, unroll=2)

    z = _matmul_resid([y_s5, y_gla], w_out.astype(BF16), x_res, tm=1024, tn=1024, tk=w_out.shape[0],
                      name="even_out")
    return _norm_stage(z, ln_g, ln_b)


def _odd_layer(x_res, x_bf, bsz, seq, layer, o, w_in, w_out, lb_table, norm_g, ln_g, ln_b):
    heads = w_out.shape[0] // HG_DV
    h = _matmul(x_bf, w_in, o, n=w_in.shape[2], tm=1024, tn=IN_PROJ_TN, out_dtype=F32, name="odd_in")
    y = _hgrn(h, lb_table, norm_g.reshape(1, HG_DV), layer=layer, bsz=bsz, seq=seq, heads=heads,
              tt=512, hb=16, unroll=2)
    z = _matmul_resid([y], w_out.astype(BF16), x_res, tm=1024, tn=1024, tk=w_out.shape[0], name="odd_out")
    return _norm_stage(z, ln_g, ln_b)


def kernel(x, ev_w_in, ev_w_out, s5_lam_re, s5_lam_im, s5_log_dt, s5_b_re, s5_b_im, s5_c_re, s5_c_im,
           s5_d, s5_w_glu, gla_w_alpha, gla_b_alpha, gla_norm_g, od_w_in, od_w_out, hg_lb_table,
           hg_norm_g, ln_mix_g, ln_mix_b, ln_ffn_g, ln_ffn_b, ffn_w_gate, ffn_w_up, ffn_w_down):
    bsz, seq, d = x.shape
    depth = ln_mix_g.shape[0]
    assert depth == DEPTH
    xf = x.reshape(bsz * seq, d)
    xb = None
    for layer in range(depth):
        if layer % 2 == 0:
            e = layer // 2
            xf, xb = _even_layer(xf, xb, bsz, seq, e, ev_w_in, ev_w_out[e], s5_lam_re[e], s5_lam_im[e],
                                 s5_log_dt[e], s5_b_re[e], s5_b_im[e], s5_c_re[e], s5_c_im[e], s5_d[e],
                                 s5_w_glu[e], gla_w_alpha[e], gla_b_alpha[e], gla_norm_g[e],
                                 ln_mix_g[layer], ln_mix_b[layer])
        else:
            o = layer // 2
            xf, xb = _odd_layer(xf, xb, bsz, seq, layer, o, od_w_in, od_w_out[o], hg_lb_table,
                                hg_norm_g[o], ln_mix_g[layer], ln_mix_b[layer])
        xf, xb = _ffn_block(xf, xb, ffn_w_gate, ffn_w_up, ffn_w_down, layer,
                            ln_ffn_g[layer], ln_ffn_b[layer], final=layer == depth - 1)
    return xf.reshape(bsz, seq, d).astype(x.dtype)
```

```python
import functools
import math
from typing import NamedTuple

import jax
import jax.numpy as jnp
from jax import lax
from jax.experimental import pallas as pl
from jax.experimental.pallas import tpu as pltpu

F32 = jnp.float32
BF16 = jnp.bfloat16

DEPTH = 2
S5_GROUP = 16
S5_STATE = 64
GLA_HEADS = 8
GLA_DK = 128
GLA_DV = 256
GLA_GATE_RANK = 16
GLA_TAU = 16.0
HG_DK = 128
HG_DV = 128
CHUNK = 64
DEEPNORM_ALPHA = (2.0 * DEPTH) ** 0.25
NORM_EPS = 1e-5

V7X_LANES = 128
V7X_SUBLANES = 8
V7X_VMEM_LIMIT_BYTES = 60 * 1024 * 1024
V7X_VMEM_COMPILER_TEMP_BYTES = 12 * 1024 * 1024

MM_ROW_CHUNK = 256
IN_PROJ_TN = 1024
FFN_DOWN_TK = 2816
LN_TM = 256
LN_COL_BLOCK = 1024
LN_ROW_BLOCK = 16
LN_STAGE_LAG = 2

S5_CHUNK = 16
S5_ROW = S5_CHUNK * S5_GROUP
S5_LANE_GROUPS = V7X_LANES // S5_GROUP
S5_TRI_COLS = 512


def _nbytes(shape, dtype):
    return math.prod(shape) * jnp.dtype(dtype).itemsize


def _params(*sem, buffers=None):
    limit = V7X_VMEM_LIMIT_BYTES
    if buffers is not None:
        need = sum(_nbytes(shape, dtype) * count for shape, dtype, count in buffers)
        limit = min(limit, need + V7X_VMEM_COMPILER_TEMP_BYTES)
    return pltpu.CompilerParams(dimension_semantics=sem, vmem_limit_bytes=limit)


def _stage_weight_tile(w_hbm, stage_ref, wb_ref, sem, *, layer, n_valid, nj):
    j = pl.program_id(0)
    tn = stage_ref.shape[1]
    last = nj - 1
    last_w = n_valid - last * tn

    def full_copy(jj):
        cols = pl.ds(pl.multiple_of(jj * tn, tn), tn)
        return pltpu.make_async_copy(w_hbm.at[layer, :, cols], stage_ref, sem)

    def last_copy():
        return pltpu.make_async_copy(w_hbm.at[layer, :, pl.ds(last * tn, last_w)],
                                     stage_ref.at[:, pl.ds(0, last_w)], sem)

    def start(jj):
        if last_w == tn:
            full_copy(jj).start()
        else:
            pl.when(jj < last)(lambda: full_copy(jj).start())
            pl.when(jj == last)(lambda: last_copy().start())

    @pl.when(pl.program_id(1) == 0)
    def _():
        pl.when(j == 0)(lambda: start(j))
        if last_w == tn:
            full_copy(j).wait()
            wb_ref[...] = stage_ref[...].astype(BF16)
        else:
            @pl.when(j < last)
            def _():
                full_copy(j).wait()
                wb_ref[...] = stage_ref[...].astype(BF16)

            @pl.when(j == last)
            def _():
                last_copy().wait()
                wb_ref[:, 0:last_w] = stage_ref[:, 0:last_w].astype(BF16)
                wb_ref[:, last_w:] = jnp.zeros((wb_ref.shape[0], tn - last_w), BF16)

        pl.when(j < last)(lambda: start(j + 1))


def _mm_kernel(a_ref, w_hbm, o_ref, stage_ref, wb_ref, sem, *, layer, n_valid, nj):
    _stage_weight_tile(w_hbm, stage_ref, wb_ref, sem, layer=layer, n_valid=n_valid, nj=nj)
    w = wb_ref[...]
    for r0 in range(0, a_ref.shape[0], MM_ROW_CHUNK):
        rows = slice(r0, r0 + MM_ROW_CHUNK)
        o_ref[rows, :] = jnp.dot(a_ref[rows, :], w, preferred_element_type=F32).astype(o_ref.dtype)


def _matmul(a, w, layer, *, n, tm, tn, out_dtype, name):
    m, k = a.shape
    nj = n // tn
    return pl.pallas_call(
        functools.partial(_mm_kernel, layer=layer, n_valid=n, nj=nj),
        grid=(nj, m // tm),
        in_specs=[pl.BlockSpec((tm, k), lambda j, i: (i, 0)),
                  pl.BlockSpec(memory_space=pl.ANY)],
        out_specs=pl.BlockSpec((tm, tn), lambda j, i: (i, j)),
        out_shape=jax.ShapeDtypeStruct((m, n), out_dtype),
        scratch_shapes=[pltpu.VMEM((k, tn), F32), pltpu.VMEM((k, tn), BF16), pltpu.SemaphoreType.DMA],
        compiler_params=_params("arbitrary", "arbitrary", buffers=[
            ((tm, k), BF16, 2), ((k, tn), F32, 1), ((k, tn), BF16, 1), ((tm, tn), out_dtype, 2)]),
        name=name,
    )(a, w)


def _ffn_up_kernel(x_ref, wg_hbm, wu_hbm, wd_ref, o_ref, wdb_ref, sg_ref, su_ref, wgb_ref, wub_ref, sems,
                   *, layer, n_valid, nj, wd_blocks):
    _stage_weight_tile(wg_hbm, sg_ref, wgb_ref, sems.at[0], layer=layer, n_valid=n_valid, nj=nj)
    _stage_weight_tile(wu_hbm, su_ref, wub_ref, sems.at[1], layer=layer, n_valid=n_valid, nj=nj)
    tn = o_ref.shape[1]
    last = nj - 1
    last_w = n_valid - last * tn

    chunk = min(MM_ROW_CHUNK, x_ref.shape[0])

    def tile(width):
        wg, wu = wgb_ref[:, 0:width], wub_ref[:, 0:width]
        for r0 in range(0, x_ref.shape[0], chunk):
            rows = slice(r0, r0 + chunk)
            x = x_ref[rows, :]
            g = jnp.dot(x, wg, preferred_element_type=F32)
            u = jnp.dot(x, wu, preferred_element_type=F32)
            o_ref[rows, 0:width] = (g * jax.nn.sigmoid(g) * u).astype(o_ref.dtype)
            if width < tn:
                o_ref[rows, width:] = jnp.zeros((chunk, tn - width), o_ref.dtype)

    if last_w == tn:
        tile(tn)
    else:
        pl.when(pl.program_id(0) < last)(lambda: tile(tn))
        pl.when(pl.program_id(0) == last)(lambda: tile(last_w))
    step = pl.program_id(0) * pl.num_programs(1) + pl.program_id(1)
    wdb_ref[...] = jnp.where(step < wd_blocks, wd_ref[...], 0.0).astype(wdb_ref.dtype)


def _ffn_up(x_bf, wg, wu, wd, layer, *, tm, tn, n_out):
    m, k = x_bf.shape
    hidden, dout = wd.shape[1:]
    nj, ni = n_out // tn, m // tm
    slab = n_out // (nj * ni)
    assert slab * nj * ni == n_out and hidden % slab == 0 and slab % (2 * V7X_SUBLANES) == 0
    wd_blocks = hidden // slab
    hbm = pl.BlockSpec(memory_space=pl.ANY)
    return pl.pallas_call(
        functools.partial(_ffn_up_kernel, layer=layer, n_valid=wg.shape[2], nj=nj, wd_blocks=wd_blocks),
        grid=(nj, ni),
        in_specs=[pl.BlockSpec((tm, k), lambda j, i: (i, 0)), hbm, hbm,
                  pl.BlockSpec((None, slab, dout),
                               lambda j, i: (layer, jnp.minimum(j * ni + i, wd_blocks - 1), 0))],
        out_specs=[pl.BlockSpec((tm, tn), lambda j, i: (i, j)),
                   pl.BlockSpec((slab, dout), lambda j, i: (j * ni + i, 0))],
        out_shape=[jax.ShapeDtypeStruct((m, n_out), BF16), jax.ShapeDtypeStruct((n_out, dout), BF16)],
        scratch_shapes=[pltpu.VMEM((k, tn), F32), pltpu.VMEM((k, tn), F32),
                        pltpu.VMEM((k, tn), BF16), pltpu.VMEM((k, tn), BF16),
                        pltpu.SemaphoreType.DMA((2,))],
        compiler_params=_params("arbitrary", "arbitrary", buffers=[
            ((tm, k), BF16, 2), ((k, tn), F32, 2), ((k, tn), BF16, 2), ((tm, tn), BF16, 2),
            ((slab, dout), F32, 2), ((slab, dout), BF16, 2)]),
        name="ffn_up",
    )(x_bf, wg, wu, wd)


def _mm_resid_kernel(*refs, k_parts, nk):
    n_a = len(k_parts)
    a_refs = refs[:n_a]
    w_ref, *r_refs, o_ref = refs[n_a:]
    k = pl.program_id(2)

    def product():
        acc, k0 = None, 0
        for a_ref, kp in zip(a_refs, k_parts):
            d = jnp.dot(a_ref[...], w_ref[k0:k0 + kp, :], preferred_element_type=F32)
            acc = d if acc is None else acc + d
            k0 += kp
        return acc

    def residual():
        if len(r_refs) == 1:
            return r_refs[0][...]
        z_ref, mu_ref, rs_ref, g_ref, b_ref = r_refs
        nrep = z_ref.shape[1] // mu_ref.shape[1]
        mu, rs = jnp.tile(mu_ref[...], (1, nrep)), jnp.tile(rs_ref[...], (1, nrep))
        return (z_ref[...] - mu) * rs * g_ref[...] + b_ref[...]

    if nk == 1:
        o_ref[...] = product() + DEEPNORM_ALPHA * residual()
    else:
        @pl.when(k == 0)
        def _():
            o_ref[...] = product() + DEEPNORM_ALPHA * residual()

        @pl.when(k > 0)
        def _():
            o_ref[...] += product()


class _NormedRows(NamedTuple):
    z: jax.Array
    mean: jax.Array
    rstd: jax.Array
    gain: jax.Array
    bias: jax.Array


def _matmul_resid(a_parts, w, resid, *, tm, tn, tk, name):
    tile = pl.BlockSpec((tm, tn), lambda i, j, k: (i, j))
    if isinstance(resid, _NormedRows):
        m, n = resid.z.shape
        stat = pl.BlockSpec((tm, V7X_LANES), lambda i, j, k: (i, 0))
        vec = pl.BlockSpec((1, tn), lambda i, j, k: (0, j))
        r_ops = [resid.z, resid.mean, resid.rstd, resid.gain.reshape(1, n), resid.bias.reshape(1, n)]
        r_specs = [tile, stat, stat, vec, vec]
    else:
        m, n = resid.shape
        r_ops, r_specs = [resid], [tile]
    kdim = w.shape[0]
    if len(a_parts) > 1:
        assert tk == kdim
        k_parts = tuple(a.shape[1] for a in a_parts)
        a_specs = [pl.BlockSpec((tm, kp), lambda i, j, k: (i, 0)) for kp in k_parts]
    else:
        k_parts = (tk,)
        a_specs = [pl.BlockSpec((tm, tk), lambda i, j, k: (i, k))]
    nk = kdim // tk
    return pl.pallas_call(
        functools.partial(_mm_resid_kernel, k_parts=k_parts, nk=nk),
        grid=(m // tm, n // tn, nk),
        in_specs=a_specs + [pl.BlockSpec((tk, tn), lambda i, j, k: (k, j))] + r_specs,
        out_specs=tile,
        out_shape=jax.ShapeDtypeStruct((m, n), F32),
        compiler_params=_params("parallel", "parallel", "arbitrary"),
        name=name,
    )(*a_parts, w, *r_ops)


def _ln_kernel(z_ref, g_ref, b_ref, y_ref, mu_ref, rs_ref):
    inv_n = 1.0 / z_ref.shape[1]
    nblk = z_ref.shape[0] // LN_ROW_BLOCK
    lanes = mu_ref.shape[1]
    nrep = z_ref.shape[1] // lanes

    def block(r):
        start = r * LN_ROW_BLOCK
        return pl.ds(start if isinstance(r, int) else pl.multiple_of(start, LN_ROW_BLOCK), LN_ROW_BLOCK)

    def mean_rows(r):
        rows = block(r)
        mu = jnp.sum(z_ref[rows, :], axis=-1, keepdims=True) * inv_n
        mu_ref[rows, :] = jnp.broadcast_to(mu, (LN_ROW_BLOCK, lanes))

    def rstd_rows(r):
        rows = block(r)
        zc = z_ref[rows, :] - jnp.tile(mu_ref[rows, :], (1, nrep))
        var = jnp.sum(zc * zc, axis=-1, keepdims=True) * inv_n
        rs_ref[rows, :] = jnp.broadcast_to(lax.rsqrt(var + NORM_EPS), (LN_ROW_BLOCK, lanes))

    def norm_rows(r):
        rows = block(r)
        mu = jnp.tile(mu_ref[rows, :], (1, LN_COL_BLOCK // lanes))
        rs = jnp.tile(rs_ref[rows, :], (1, LN_COL_BLOCK // lanes))
        for c0 in range(0, z_ref.shape[1], LN_COL_BLOCK):
            cols = slice(c0, c0 + LN_COL_BLOCK)
            y = (z_ref[rows, cols] - mu) * rs * g_ref[:, cols] + b_ref[:, cols]
            y_ref[rows, cols] = y.astype(y_ref.dtype)

    lag = LN_STAGE_LAG

    def trip(r, with_mean=True, with_rstd=True, with_norm=True):
        if with_norm:
            norm_rows(r - 2 * lag)
        if with_rstd:
            rstd_rows(r - lag)
        if with_mean:
            mean_rows(r)

    for r in range(2 * lag):
        trip(r, with_rstd=r >= lag, with_norm=False)
    lax.fori_loop(2 * lag, nblk, lambda r, c: (trip(r), c)[1], 0, unroll=2)
    for r in range(nblk, nblk + 2 * lag):
        trip(r, with_mean=False, with_rstd=r - lag < nblk)


def _layernorm(z, gain, bias, *, tm, out_dtype):
    m, n = z.shape
    row_spec = pl.BlockSpec((tm, n), lambda i: (i, 0))
    vec_spec = pl.BlockSpec((1, n), lambda i: (0, 0))
    stat_spec = pl.BlockSpec((tm, V7X_LANES), lambda i: (i, 0))
    stat_shape = jax.ShapeDtypeStruct((m, V7X_LANES), F32)
    return pl.pallas_call(
        _ln_kernel,
        grid=(m // tm,),
        in_specs=[row_spec, vec_spec, vec_spec],
        out_specs=[row_spec, stat_spec, stat_spec],
        out_shape=[jax.ShapeDtypeStruct((m, n), out_dtype), stat_shape, stat_shape],
        compiler_params=_params("parallel"),
        name="layernorm",
    )(z, gain.reshape(1, n), bias.reshape(1, n))


def _norm_stage(z, gain, bias):
    y_bf, mean, rstd = _layernorm(z, gain, bias, tm=LN_TM, out_dtype=BF16)
    return _NormedRows(z, mean, rstd, gain, bias), y_bf


def _chunk_steps(qs, ks, vs, las, st_ref, tril_bf, causal):
    c = qs[0].shape[0]
    nt = (((1,), (1,)), ((), ()))
    tn = (((0,), (0,)), ((), ()))
    heads = range(len(qs))
    bs = []
    for la in las:
        la_hi = la.astype(BF16)
        la_lo = (la - la_hi.astype(F32)).astype(BF16)
        bs.append(jnp.dot(tril_bf, la_hi, preferred_element_type=F32)
                  + jnp.dot(tril_bf, la_lo, preferred_element_type=F32))
    q_decs, scores, kvs, inters, decays = [], [], [], [], []
    for hd in heads:
        b = bs[hd]
        b_end = b[c - 1:c, :]
        decay = jnp.exp(b_end)
        q_dec = (qs[hd] * jnp.exp(b)).astype(BF16)
        k_inv = ks[hd] * jnp.exp(-b)
        k_end = (k_inv * decay).astype(BF16)
        k_inv = k_inv.astype(BF16)
        v_bf = vs[hd].astype(BF16)
        vs[hd] = v_bf
        scores.append(lax.dot_general(q_dec, k_inv, nt, preferred_element_type=F32))
        kvs.append(lax.dot_general(v_bf, k_end, tn, preferred_element_type=F32))
        inters.append(lax.dot_general(q_dec, st_ref[hd].astype(BF16), nt, preferred_element_type=F32))
        decays.append(decay)
    outs = []
    for hd in heads:
        sc = jnp.where(causal, scores[hd], 0.0).astype(BF16)
        outs.append(jnp.dot(sc, vs[hd], preferred_element_type=F32) + inters[hd])
        st_ref[hd] = decays[hd] * st_ref[hd] + kvs[hd]
    return outs


def _gated_rmsnorm(o, gain, gate):
    ms = jnp.mean(o * o, axis=-1, keepdims=True)
    return o * lax.rsqrt(ms + NORM_EPS) * gain * (gate * jax.nn.sigmoid(gate))


def _chunk_masks():
    row = lax.broadcasted_iota(jnp.int32, (CHUNK, CHUNK), 0)
    col = lax.broadcasted_iota(jnp.int32, (CHUNK, CHUNK), 1)
    causal = row >= col
    return jnp.where(causal, 1.0, 0.0).astype(BF16), causal


def _log_sigmoid(z):
    return jnp.minimum(z, 0.0) - jnp.log1p(jnp.exp(-jnp.abs(z)))


def _gla_kernel(q_ref, k_ref, v_ref, g_ref, alr_ref, wa_ref, ba_ref, gain_ref, o_ref, st_ref,
                *, hb, nchunks, unroll):
    @pl.when(pl.program_id(2) == 0)
    def _():
        st_ref[...] = jnp.zeros_like(st_ref)

    tril_bf, causal = _chunk_masks()
    gain = gain_ref[...]
    ba = ba_ref[...]
    wa = wa_ref[...]

    def body(c, carry):
        rows = pl.ds(pl.multiple_of(c * CHUNK, CHUNK), CHUNK)
        z = jnp.dot(alr_ref[rows, :].astype(BF16), wa, preferred_element_type=F32) + ba
        la_all = _log_sigmoid(z) / GLA_TAU
        kcs = [slice(hd * GLA_DK, (hd + 1) * GLA_DK) for hd in range(hb)]
        vcs = [slice(hd * GLA_DV, (hd + 1) * GLA_DV) for hd in range(hb)]
        outs = _chunk_steps([q_ref[rows, kc] * (GLA_DK ** -0.5) for kc in kcs],
                            [k_ref[rows, kc] for kc in kcs], [v_ref[rows, vc] for vc in vcs],
                            [la_all[:, kc] for kc in kcs], st_ref, tril_bf, causal)
        for o, vc in zip(outs, vcs):
            o_ref[rows, vc] = _gated_rmsnorm(o, gain, g_ref[rows, vc]).astype(o_ref.dtype)
        return carry

    lax.fori_loop(0, nchunks, body, 0, unroll=unroll)


def _gla(h, alr, w_alpha, b_alpha, norm_g, *, bsz, seq, col_q, col_k, col_v, col_g, tt, hb, unroll):
    n = bsz * seq
    nt = seq // tt
    kw, vw = hb * GLA_DK, hb * GLA_DV
    hg = GLA_HEADS // hb

    def rows(b, g, t):
        return b * nt + t

    return pl.pallas_call(
        functools.partial(_gla_kernel, hb=hb, nchunks=tt // CHUNK, unroll=unroll),
        grid=(bsz, hg, nt),
        in_specs=[pl.BlockSpec((tt, kw), lambda b, g, t: (rows(b, g, t), col_q // kw + g)),
                  pl.BlockSpec((tt, kw), lambda b, g, t: (rows(b, g, t), col_k // kw + g)),
                  pl.BlockSpec((tt, vw), lambda b, g, t: (rows(b, g, t), col_v // vw + g)),
                  pl.BlockSpec((tt, vw), lambda b, g, t: (rows(b, g, t), col_g // vw + g)),
                  pl.BlockSpec((tt, V7X_LANES), lambda b, g, t: (rows(b, g, t), 0)),
                  pl.BlockSpec((V7X_LANES, kw), lambda b, g, t: (0, g)),
                  pl.BlockSpec((1, kw), lambda b, g, t: (0, g)),
                  pl.BlockSpec((1, GLA_DV), lambda b, g, t: (0, 0))],
        out_specs=pl.BlockSpec((tt, vw), lambda b, g, t: (rows(b, g, t), g)),
        out_shape=jax.ShapeDtypeStruct((n, GLA_HEADS * GLA_DV), BF16),
        scratch_shapes=[pltpu.VMEM((hb, GLA_DV, GLA_DK), F32)],
        compiler_params=_params("parallel", "parallel", "arbitrary"),
        name="gla",
    )(h, h, h, h, alr, w_alpha, b_alpha, norm_g)


def _hgrn_kernel(q_ref, f_ref, i_ref, g_ref, tab_ref, gain_ref, o_ref, st_ref, *, layer, hb, nchunks,
                 unroll):
    @pl.when(pl.program_id(2) == 0)
    def _():
        st_ref[...] = jnp.zeros_like(st_ref)

    tab = tab_ref[...]
    e = jnp.exp(tab - jnp.max(tab, axis=0, keepdims=True))
    soft = e / jnp.sum(e, axis=0, keepdims=True)
    lb_all = jnp.sum(soft[0:layer + 1], axis=0, keepdims=True) - soft[0:1]
    tril_bf, causal = _chunk_masks()
    gain = gain_ref[...]

    def body(c, carry):
        rows = pl.ds(pl.multiple_of(c * CHUNK, CHUNK), CHUNK)
        kcs = [slice(hd * HG_DK, (hd + 1) * HG_DK) for hd in range(hb)]
        vcs = [slice(hd * HG_DV, (hd + 1) * HG_DV) for hd in range(hb)]
        fs = [lb_all[:, kc] + (1.0 - lb_all[:, kc]) * jax.nn.sigmoid(f_ref[rows, kc]) for kc in kcs]
        ivs = [i_ref[rows, vc] for vc in vcs]
        outs = _chunk_steps([q_ref[rows, kc] for kc in kcs], [1.0 - f for f in fs],
                            [iv * jax.nn.sigmoid(iv) for iv in ivs], [jnp.log(f) for f in fs],
                            st_ref, tril_bf, causal)
        for o, vc in zip(outs, vcs):
            o_ref[rows, vc] = _gated_rmsnorm(o, gain, g_ref[rows, vc]).astype(o_ref.dtype)
        return carry

    lax.fori_loop(0, nchunks, body, 0, unroll=unroll)


def _hgrn(h, lb_table, norm_g, *, layer, bsz, seq, heads, tt, hb, unroll):
    n = bsz * seq
    nt = seq // tt
    w = hb * HG_DK
    hg = heads // hb

    def spec(seg):
        return pl.BlockSpec((tt, w), lambda b, g, t: (b * nt + t, seg * hg + g))

    return pl.pallas_call(
        functools.partial(_hgrn_kernel, layer=layer, hb=hb, nchunks=tt // CHUNK, unroll=unroll),
        grid=(bsz, hg, nt),
        in_specs=[spec(0), spec(1), spec(2), spec(3),
                  pl.BlockSpec((DEPTH, w), lambda b, g, t: (0, g)),
                  pl.BlockSpec((1, HG_DV), lambda b, g, t: (0, 0))],
        out_specs=pl.BlockSpec((tt, w), lambda b, g, t: (b * nt + t, g)),
        out_shape=jax.ShapeDtypeStruct((n, heads * HG_DV), BF16),
        scratch_shapes=[pltpu.VMEM((hb, HG_DV, HG_DK), F32)],
        compiler_params=_params("parallel", "parallel", "arbitrary"),
        name="hgrn2",
    )(h, h, h, h, lb_table, norm_g)


def _s5_prep_cast_kernel(lr_row_ref, li_row_ref, lr_col_ref, li_col_ref, ldt_ref,
                         btr_ref, bti_ref, ctr_ref, cti_ref, src_ref, wgate_ref,
                         w1_ref, f_ref, coef_ref, dst_ref, gate_ref):
    _s5_prep_kernel(lr_row_ref, li_row_ref, lr_col_ref, li_col_ref, ldt_ref,
                    btr_ref, bti_ref, ctr_ref, cti_ref, w1_ref, f_ref, coef_ref)
    x = src_ref[...].astype(dst_ref.dtype)
    dst_ref[...] = x
    gate_ref[...] = jnp.dot(x, wgate_ref[...], preferred_element_type=F32)


def _s5_prep_kernel(lr_row_ref, li_row_ref, lr_col_ref, li_col_ref, ldt_ref,
                    btr_ref, bti_ref, ctr_ref, cti_ref, w1_ref, f_ref, coef_ref):
    L, J, P = S5_CHUNK, S5_GROUP, S5_STATE
    hi = lax.Precision.HIGHEST
    dt = jnp.exp(ldt_ref[0])
    lr, li = lr_row_ref[0], li_row_ref[0]
    npow = ((L - 1) - lax.broadcasted_iota(jnp.int32, (L, 2 * P), 0)).astype(F32)
    mag = jnp.exp(npow * (lr * dt))
    th = npow * (li * dt)
    ar, ai = mag * jnp.cos(th), mag * jnp.sin(th)
    half = lax.broadcasted_iota(jnp.int32, (L, 2 * P), 1) < P
    a1r, a1i = ar[L - 2:L - 1], ai[L - 2:L - 1]
    nr, ni = a1r - 1.0, a1i
    den = lr * lr + li * li
    fr, fi = (nr * lr + ni * li) / den, (ni * lr - nr * li) / den
    btr, bti = btr_ref[0], bti_ref[0]
    bbr, bbi = fr * btr - fi * bti, fr * bti + fi * btr
    pa, pb = jnp.where(half, ar, ai), jnp.where(half, -ai, ar)
    for s in range(L):
        rows = slice(s * J, (s + 1) * J)
        w1_ref[0, rows, L * J:] = (bbr * pa[s:s + 1] + bbi * pb[s:s + 1]).astype(BF16)
    alr_, ali_ = ar[0:1] * a1r - ai[0:1] * a1i, ar[0:1] * a1i + ai[0:1] * a1r
    half1 = half[0:1]
    coef_ref[0, 0:1, :] = alr_
    coef_ref[0, 1:2, :] = jnp.where(half1, -ali_, ali_)
    coef_ref[0, 2:, :] = jnp.zeros((V7X_SUBLANES - 2, 2 * P), F32)

    lrc, lic = lr_col_ref[0], li_col_ref[0]
    lane = lax.broadcasted_iota(jnp.int32, (P, L * J), 1)
    lag = jnp.right_shift(lane, int(math.log2(J))).astype(F32)
    magq = jnp.exp(lag * (lrc * dt))
    thq = lag * (lic * dt)
    aqr, aqi = magq * jnp.cos(thq), magq * jnp.sin(thq)
    cr, ci = ctr_ref[0], cti_ref[0]
    qr, qi = cr * aqr - ci * aqi, cr * aqi + ci * aqr
    r = (jnp.dot(bbr[:, :P], qr, precision=hi, preferred_element_type=F32)
         - jnp.dot(bbi[:, :P], qi, precision=hi, preferred_element_type=F32))
    lane_r = lax.broadcasted_iota(jnp.int32, (J, L * J), 1)
    for s in range(L):
        blk = r if s == 0 else jnp.where(lane_r >= s * J, pltpu.roll(r, s * J, 1), 0.0)
        w1_ref[0, s * J:(s + 1) * J, 0:L * J] = blk.astype(BF16)
    m1 = jnp.exp(lrc * dt)
    c1r, c1i = m1 * jnp.cos(lic * dt), m1 * jnp.sin(lic * dt)
    f_ref[0, 0:P, :] = (qr * c1r - qi * c1i).astype(BF16)
    f_ref[0, P:, :] = (-(qr * c1i + qi * c1r)).astype(BF16)


def _s5_prep(lam_re, lam_im, log_dt, b_re, b_im, c_re, c_im, cast_src=None, gate_w=None):
    g = lam_re.shape[0]
    P, J, L = S5_STATE, S5_GROUP, S5_CHUNK
    dbl = lambda a: jnp.concatenate([a, a], axis=-1)
    lr_row = dbl(lam_re).reshape(g, 1, 2 * P)
    li_row = dbl(lam_im).reshape(g, 1, 2 * P)
    lr_col = lam_re.reshape(g, P, 1)
    li_col = lam_im.reshape(g, P, 1)
    ldt = log_dt.reshape(g, 1, 1)
    btr = dbl(jnp.swapaxes(b_re, 1, 2))
    bti = dbl(jnp.swapaxes(b_im, 1, 2))
    ctr = jnp.tile(jnp.swapaxes(c_re, 1, 2), (1, 1, L))
    cti = jnp.tile(jnp.swapaxes(c_im, 1, 2), (1, 1, L))
    blk = lambda *s: pl.BlockSpec((1,) + s, lambda i: (i, 0, 0))
    operands = [lr_row, li_row, lr_col, li_col, ldt, btr, bti, ctr, cti]
    in_specs = [blk(1, 2 * P), blk(1, 2 * P), blk(P, 1), blk(P, 1), blk(1, 1),
                blk(J, 2 * P), blk(J, 2 * P), blk(P, L * J), blk(P, L * J)]
    out_specs = [blk(L * J, L * J + 2 * P), blk(2 * P, L * J), blk(V7X_SUBLANES, 2 * P)]
    out_shape = [jax.ShapeDtypeStruct((g, L * J, L * J + 2 * P), BF16),
                 jax.ShapeDtypeStruct((g, 2 * P, L * J), BF16),
                 jax.ShapeDtypeStruct((g, V7X_SUBLANES, 2 * P), F32)]
    body = _s5_prep_kernel
    if cast_src is not None:
        m, d = cast_src.shape
        slab = pl.BlockSpec((m // g, d), lambda i: (i, 0))
        gate_slab = pl.BlockSpec((m // g, gate_w.shape[1]), lambda i: (i, 0))
        operands += [cast_src, gate_w]
        in_specs += [slab, pl.BlockSpec(gate_w.shape, lambda i: (0, 0))]
        out_specs += [slab, gate_slab]
        out_shape += [jax.ShapeDtypeStruct((m, d), BF16), jax.ShapeDtypeStruct((m, gate_w.shape[1]), F32)]
        body = _s5_prep_cast_kernel
    return pl.pallas_call(body, grid=(g,), in_specs=in_specs, out_specs=out_specs, out_shape=out_shape,
                          compiler_params=_params("parallel"), name="s5_prep")(*operands)


def _s5_assemble_kernel(w1_ref, f_ref, coef_ref, wbig_ref, fbig_ref, cbig_ref):
    L, J, P, Q = S5_CHUNK, S5_GROUP, S5_STATE, S5_LANE_GROUPS
    lanes = Q * J
    width = L * lanes
    nstate = 2 * Q * P
    jbits, pbits = int(math.log2(J)), int(math.log2(P))

    def iota(shape, axis):
        return lax.broadcasted_iota(jnp.int32, shape, axis)

    a, b = iota((L * J, width), 0), iota((L * J, width), 1)
    sel_sc = jnp.where(((b >> int(math.log2(lanes))) == (a >> jbits)) & ((b & (J - 1)) == (a & (J - 1))),
                       1.0, 0.0).astype(BF16)
    a, b = iota((2 * P, nstate), 0), iota((2 * P, nstate), 1)
    sel_st = jnp.where(((b >> int(math.log2(Q * P))) == (a >> pbits)) & ((b & (P - 1)) == (a & (P - 1))),
                       1.0, 0.0).astype(BF16)
    col_group_sc = (iota((lanes, width), 1) >> jbits) & (Q - 1)
    col_group_st = (iota((lanes, nstate), 1) >> pbits) & (Q - 1)
    row_group_j = iota((lanes, 1), 0) >> jbits

    r_stack = jnp.concatenate([w1_ref[g, 0:J, 0:L * J] for g in range(Q)], axis=0)
    r_big = jnp.dot(r_stack, sel_sc, preferred_element_type=F32)
    r_big = jnp.where(col_group_sc == row_group_j, r_big, 0.0).astype(BF16)
    for s in range(L):
        rows = slice(s * lanes, (s + 1) * lanes)
        if s:
            wbig_ref[0, rows, 0:s * lanes] = jnp.zeros((lanes, s * lanes), BF16)
        wbig_ref[0, rows, s * lanes:width] = r_big[:, 0:width - s * lanes]
        e_stack = jnp.concatenate([w1_ref[g, s * J:(s + 1) * J, L * J:] for g in range(Q)], axis=0)
        e_big = jnp.dot(e_stack, sel_st, preferred_element_type=F32)
        wbig_ref[0, rows, width:] = jnp.where(col_group_st == row_group_j, e_big, 0.0).astype(BF16)

    row_in_pair = iota((lanes, 1), 0) >> pbits
    for m in range(nstate // lanes):
        ri, g0 = divmod(m * (lanes // P), Q)
        f_stack = jnp.concatenate([f_ref[g0 + t, ri * P:(ri + 1) * P, :] for t in range(lanes // P)], axis=0)
        f_big = jnp.dot(f_stack, sel_sc, preferred_element_type=F32)
        fbig_ref[0, m * lanes:(m + 1) * lanes, :] = jnp.where(
            col_group_sc == g0 + row_in_pair, f_big, 0.0).astype(BF16)

    low = iota((V7X_SUBLANES, lanes), 1) < P
    for k in range(Q // 2):
        c0, c1 = coef_ref[2 * k], coef_ref[2 * k + 1]
        row_is_a2 = iota((V7X_SUBLANES, lanes), 0) == 1
        re_half = jnp.where(low, c0, jnp.where(row_is_a2, -c1, c1))
        im_half = jnp.where(low, jnp.where(row_is_a2, -c0, c0), c1)
        cbig_ref[0, :, k * lanes:(k + 1) * lanes] = re_half
        cbig_ref[0, :, Q * P + k * lanes:Q * P + (k + 1) * lanes] = im_half


def _s5_assemble(w1, fmat, coef):
    g = w1.shape[0]
    L, J, P, Q = S5_CHUNK, S5_GROUP, S5_STATE, S5_LANE_GROUPS
    nb = g // Q
    width, nstate = L * Q * J, 2 * Q * P
    grp = lambda *s: pl.BlockSpec((Q,) + s, lambda i: (i, 0, 0))
    out = lambda *s: pl.BlockSpec((1,) + s, lambda i: (i, 0, 0))
    return pl.pallas_call(
        _s5_assemble_kernel,
        grid=(nb,),
        in_specs=[grp(L * J, L * J + 2 * P), grp(2 * P, L * J), grp(V7X_SUBLANES, 2 * P)],
        out_specs=[out(width, width + nstate), out(nstate, width), out(V7X_SUBLANES, nstate)],
        out_shape=[jax.ShapeDtypeStruct((nb, width, width + nstate), BF16),
                   jax.ShapeDtypeStruct((nb, nstate, width), BF16),
                   jax.ShapeDtypeStruct((nb, V7X_SUBLANES, nstate), F32)],
        compiler_params=_params("parallel"),
        name="s5_assemble",
    )(w1, fmat, coef)


def _s5_main_kernel(u_ref, w_ref, f_ref, coef_ref, o_ref, ys_ref, sp_ref, st_ref, *, nchunk, nb):
    width = u_ref.shape[2]
    half = st_ref.shape[1] // 2

    @pl.when(pl.program_id(1) == 0)
    def _():
        st_ref[...] = jnp.zeros_like(st_ref)

    for c0 in range(0, width, S5_TRI_COLS):
        c1 = c0 + S5_TRI_COLS
        ys_ref[:, c0:c1] = jnp.dot(u_ref[0, :, 0:c1], w_ref[0, 0:c1, c0:c1], preferred_element_type=F32)
    ys_ref[:, width:] = jnp.dot(u_ref[0], w_ref[0, :, width:], preferred_element_type=F32)
    a1, a2 = coef_ref[0, 0:1, :], coef_ref[0, 1:2, :]

    def body(c, s):
        rows = pl.ds(pl.multiple_of(c * nb, nb), nb)
        sp_ref[rows, :] = s
        swapped = jnp.concatenate([s[:, half:], s[:, :half]], axis=1)
        return a1 * s + a2 * swapped + ys_ref[rows, width:]

    st_ref[...] = lax.fori_loop(0, nchunk, body, st_ref[...], unroll=4)
    y = ys_ref[:, 0:width] + jnp.dot(sp_ref[...].astype(BF16), f_ref[0], preferred_element_type=F32)
    o_ref[0] = y.astype(o_ref.dtype)


def _s5_main(u_c, wbig, fbig, cbig, *, nb, rows_per_step):
    nblk, rows, width = u_c.shape
    nstate = fbig.shape[1]
    nsteps = rows // rows_per_step
    fixed = lambda *s: pl.BlockSpec((1,) + s, lambda i, r: (i, 0, 0))
    moving = pl.BlockSpec((1, rows_per_step, width), lambda i, r: (i, r, 0))
    return pl.pallas_call(
        functools.partial(_s5_main_kernel, nchunk=rows_per_step // nb, nb=nb),
        grid=(nblk, nsteps),
        in_specs=[moving, fixed(width, width + nstate), fixed(nstate, width),
                  fixed(V7X_SUBLANES, nstate)],
        out_specs=moving,
        out_shape=jax.ShapeDtypeStruct((nblk, rows, width), BF16),
        scratch_shapes=[pltpu.VMEM((rows_per_step, width + nstate), F32),
                        pltpu.VMEM((rows_per_step, nstate), F32),
                        pltpu.VMEM((nb, nstate), F32)],
        compiler_params=_params("parallel", "arbitrary"),
        name="s5_main",
    )(u_c, wbig, fbig, cbig)


def _s5_glu_kernel(ys_ref, u_ref, d_ref, w_ref, o_ref):
    y = jax.nn.gelu(ys_ref[...].astype(F32) + d_ref[...] * u_ref[...], approximate=True)
    z = jnp.dot(y.astype(BF16), w_ref[...], preferred_element_type=F32)
    o_ref[...] = (y * jax.nn.sigmoid(z)).astype(o_ref.dtype)


def _s5_glu(y_ssm, h, d_skip, w_glu, *, tm):
    m, n = y_ssm.shape
    return pl.pallas_call(
        _s5_glu_kernel,
        grid=(m // tm,),
        in_specs=[pl.BlockSpec((tm, n), lambda i: (i, 0)),
                  pl.BlockSpec((tm, n), lambda i: (i, 0)),
                  pl.BlockSpec((1, n), lambda i: (0, 0)),
                  pl.BlockSpec((n, n), lambda i: (0, 0))],
        out_specs=pl.BlockSpec((tm, n), lambda i: (i, 0)),
        out_shape=jax.ShapeDtypeStruct((m, n), BF16),
        compiler_params=_params("parallel"),
        name="s5_glu",
    )(y_ssm, h, d_skip.reshape(1, n), w_glu)


def _pad_cols(w, n):
    return jnp.pad(w, ((0, 0), (0, n - w.shape[1])))


def _ffn_block(x_res, x_bf, w_gate, w_up, w_down, layer, ln_g, ln_b, *, final):
    hidden = w_gate.shape[2]
    hpad = -(-hidden // FFN_DOWN_TK) * FFN_DOWN_TK
    a, wd = _ffn_up(x_bf, w_gate, w_up, w_down, layer, tm=1024, tn=512, n_out=hpad)
    z = _matmul_resid([a], wd, x_res, tm=1024, tn=1024, tk=FFN_DOWN_TK, name="ffn_down")
    if final:
        return _layernorm(z, ln_g, ln_b, tm=LN_TM, out_dtype=F32)[0], None
    return _norm_stage(z, ln_g, ln_b)


def _even_layer(x_res, x_bf, bsz, seq, e, w_in, w_out, lam_re, lam_im, log_dt, b_re, b_im, c_re, c_im,
                d_skip, w_glu, w_alpha, b_alpha, norm_g, ln_g, ln_b):
    n = bsz * seq
    s5w = d_skip.shape[0]
    groups = s5w // S5_GROUP
    kw = GLA_HEADS * GLA_DK
    vw = GLA_HEADS * GLA_DV
    main = s5w + 2 * kw + 2 * vw
    w_gate = _pad_cols(w_in[e, :, main:], V7X_LANES)
    if x_bf is None:
        w1, fmat, coef, x_bf, alr = _s5_prep(lam_re, lam_im, log_dt, b_re, b_im, c_re, c_im,
                                             cast_src=x_res, gate_w=w_gate.astype(BF16))
    else:
        w1, fmat, coef = _s5_prep(lam_re, lam_im, log_dt, b_re, b_im, c_re, c_im)
        alr = _matmul(x_bf, w_gate[None], 0, n=V7X_LANES, tm=1024, tn=V7X_LANES, out_dtype=F32,
                      name="even_in_gate")
    h = _matmul(x_bf, w_in, e, n=main, tm=1024, tn=IN_PROJ_TN, out_dtype=F32, name="even_in")

    wbig, fbig, cbig = _s5_assemble(w1, fmat, coef)
    nchunk = seq // S5_CHUNK
    nblk = s5w // V7X_LANES
    u_c = (h[:, :s5w].astype(BF16).reshape(bsz, nchunk, S5_CHUNK, nblk, V7X_LANES)
           .transpose(3, 1, 0, 2, 4).reshape(nblk, nchunk * bsz, S5_CHUNK * V7X_LANES))
    y_c = _s5_main(u_c, wbig, fbig, cbig, nb=bsz, rows_per_step=256)
    y_ssm = (y_c.reshape(nblk, nchunk, bsz, S5_CHUNK, V7X_LANES)
             .transpose(2, 1, 3, 0, 4).reshape(n, s5w))
    y_s5 = _s5_glu(y_ssm, h, d_skip, w_glu.astype(BF16), tm=512)

    wa = jnp.pad(w_alpha, ((0, V7X_LANES - w_alpha.shape[0]), (0, 0))).astype(BF16)
    y_gla = _gla(h, alr, wa, b_alpha.reshape(1, kw), norm_g.reshape(1, GLA_DV), bsz=bsz, seq=seq,
                 col_q=s5w, col_k=s5w + kw, col_v=s5w + 2 * kw, col_g=s5w + 2 * kw + vw,
                 tt=512, hb=8, unroll=2)

    z = _matmul_resid([y_s5, y_gla], w_out.astype(BF16), x_res, tm=1024, tn=1024, tk=w_out.shape[0],
                      name="even_out")
    return _norm_stage(z, ln_g, ln_b)


def _odd_layer(x_res, x_bf, bsz, seq, layer, o, w_in, w_out, lb_table, norm_g, ln_g, ln_b):
    heads = w_out.shape[0] // HG_DV
    h = _matmul(x_bf, w_in, o, n=w_in.shape[2], tm=1024, tn=IN_PROJ_TN, out_dtype=F32, name="odd_in")
    y = _hgrn(h, lb_table, norm_g.reshape(1, HG_DV), layer=layer, bsz=bsz, seq=seq, heads=heads,
              tt=512, hb=16, unroll=2)
    z = _matmul_resid([y], w_out.astype(BF16), x_res, tm=1024, tn=1024, tk=w_out.shape[0], name="odd_out")
    return _norm_stage(z, ln_g, ln_b)


def kernel(x, ev_w_in, ev_w_out, s5_lam_re, s5_lam_im, s5_log_dt, s5_b_re, s5_b_im, s5_c_re, s5_c_im,
           s5_d, s5_w_glu, gla_w_alpha, gla_b_alpha, gla_norm_g, od_w_in, od_w_out, hg_lb_table,
           hg_norm_g, ln_mix_g, ln_mix_b, ln_ffn_g, ln_ffn_b, ffn_w_gate, ffn_w_up, ffn_w_down):
    bsz, seq, d = x.shape
    depth = ln_mix_g.shape[0]
    assert depth == DEPTH
    xf = x.reshape(bsz * seq, d)
    xb = None
    for layer in range(depth):
        if layer % 2 == 0:
            e = layer // 2
            xf, xb = _even_layer(xf, xb, bsz, seq, e, ev_w_in, ev_w_out[e], s5_lam_re[e], s5_lam_im[e],
                                 s5_log_dt[e], s5_b_re[e], s5_b_im[e], s5_c_re[e], s5_c_im[e], s5_d[e],
                                 s5_w_glu[e], gla_w_alpha[e], gla_b_alpha[e], gla_norm_g[e],
                                 ln_mix_g[layer], ln_mix_b[layer])
        else:
            o = layer // 2
            xf, xb = _odd_layer(xf, xb, bsz, seq, layer, o, od_w_in, od_w_out[o], hg_lb_table,
                                hg_norm_g[o], ln_mix_g[layer], ln_mix_b[layer])
        xf, xb = _ffn_block(xf, xb, ffn_w_gate, ffn_w_up, ffn_w_down, layer,
                            ln_ffn_g[layer], ln_ffn_b[layer], final=layer == depth - 1)
    return xf.reshape(bsz, seq, d).astype(x.dtype)
```

```python
import functools
import math
from typing import NamedTuple

import jax
import jax.numpy as jnp
from jax import lax
from jax.experimental import pallas as pl
from jax.experimental.pallas import tpu as pltpu

F32 = jnp.float32
BF16 = jnp.bfloat16

DEPTH = 2
S5_GROUP = 16
S5_STATE = 64
GLA_HEADS = 8
GLA_DK = 128
GLA_DV = 256
GLA_TAU = 16.0
HG_DK = 128
HG_DV = 128
CHUNK = 64
DEEPNORM_ALPHA = (2.0 * DEPTH) ** 0.25
NORM_EPS = 1e-5

V7X_LANES = 128
V7X_SUBLANES = 8
V7X_VMEM_LIMIT_BYTES = 60 * 1024 * 1024
V7X_VMEM_COMPILER_TEMP_BYTES = 12 * 1024 * 1024

MM_ROW_CHUNK = 256
IN_PROJ_TN = 1024
FFN_DOWN_TK = 2816
LN_TM = 512
LN_COL_BLOCK = 1024
LN_ROW_BLOCK = 16
LN_STAGE_LAG = 2

S5_CHUNK = 16
S5_LANE_GROUPS = V7X_LANES // S5_GROUP
S5_TRI_COLS = 512


def _nbytes(shape, dtype):
    return math.prod(shape) * jnp.dtype(dtype).itemsize


def _params(*sem, buffers=None):
    limit = V7X_VMEM_LIMIT_BYTES
    if buffers is not None:
        need = sum(_nbytes(shape, dtype) * count for shape, dtype, count in buffers)
        limit = min(limit, need + V7X_VMEM_COMPILER_TEMP_BYTES)
    return pltpu.CompilerParams(dimension_semantics=sem, vmem_limit_bytes=limit)


def _stage_weight_tile(w_hbm, stage_ref, wb_ref, sem, *, layer, n_valid, nj):
    j = pl.program_id(0)
    tn = stage_ref.shape[1]
    last = nj - 1
    last_w = n_valid - last * tn

    def full_copy(jj):
        cols = pl.ds(pl.multiple_of(jj * tn, tn), tn)
        return pltpu.make_async_copy(w_hbm.at[layer, :, cols], stage_ref, sem)

    def last_copy():
        return pltpu.make_async_copy(w_hbm.at[layer, :, pl.ds(last * tn, last_w)],
                                     stage_ref.at[:, pl.ds(0, last_w)], sem)

    def start(jj):
        if last_w == tn:
            full_copy(jj).start()
        else:
            pl.when(jj < last)(lambda: full_copy(jj).start())
            pl.when(jj == last)(lambda: last_copy().start())

    @pl.when(pl.program_id(1) == 0)
    def _():
        pl.when(j == 0)(lambda: start(j))
        if last_w == tn:
            full_copy(j).wait()
            wb_ref[...] = stage_ref[...].astype(BF16)
        else:
            @pl.when(j < last)
            def _():
                full_copy(j).wait()
                wb_ref[...] = stage_ref[...].astype(BF16)

            @pl.when(j == last)
            def _():
                last_copy().wait()
                wb_ref[:, 0:last_w] = stage_ref[:, 0:last_w].astype(BF16)
                wb_ref[:, last_w:] = jnp.zeros((wb_ref.shape[0], tn - last_w), BF16)

        pl.when(j < last)(lambda: start(j + 1))


def _mm_kernel(a_ref, w_hbm, o_ref, stage_ref, wb_ref, sem, *, layer, n_valid, nj):
    _stage_weight_tile(w_hbm, stage_ref, wb_ref, sem, layer=layer, n_valid=n_valid, nj=nj)
    w = wb_ref[...]
    chunk = min(MM_ROW_CHUNK, a_ref.shape[0])
    for r0 in range(0, a_ref.shape[0], chunk):
        rows = slice(r0, r0 + chunk)
        o_ref[rows, :] = jnp.dot(a_ref[rows, :], w, preferred_element_type=F32).astype(o_ref.dtype)


def _matmul(a, w, layer, *, n, tm, tn, out_dtype, name):
    m, k = a.shape
    nj = n // tn
    return pl.pallas_call(
        functools.partial(_mm_kernel, layer=layer, n_valid=n, nj=nj),
        grid=(nj, m // tm),
        in_specs=[pl.BlockSpec((tm, k), lambda j, i: (i, 0)),
                  pl.BlockSpec(memory_space=pl.ANY)],
        out_specs=pl.BlockSpec((tm, tn), lambda j, i: (i, j)),
        out_shape=jax.ShapeDtypeStruct((m, n), out_dtype),
        scratch_shapes=[pltpu.VMEM((k, tn), F32), pltpu.VMEM((k, tn), BF16), pltpu.SemaphoreType.DMA],
        compiler_params=_params("arbitrary", "arbitrary", buffers=[
            ((tm, k), BF16, 2), ((k, tn), F32, 1), ((k, tn), BF16, 1), ((tm, tn), out_dtype, 2)]),
        name=name,
    )(a, w)


def _ffn_up_kernel(x_ref, wg_hbm, wu_hbm, wd_ref, o_ref, wdb_ref, sg_ref, su_ref, wgb_ref, wub_ref, sems,
                   *, layer, n_valid, nj, wd_blocks):
    _stage_weight_tile(wg_hbm, sg_ref, wgb_ref, sems.at[0], layer=layer, n_valid=n_valid, nj=nj)
    _stage_weight_tile(wu_hbm, su_ref, wub_ref, sems.at[1], layer=layer, n_valid=n_valid, nj=nj)
    tn = o_ref.shape[1]
    last = nj - 1
    last_w = n_valid - last * tn

    chunk = min(MM_ROW_CHUNK, x_ref.shape[0])

    def tile(width):
        wg, wu = wgb_ref[:, 0:width], wub_ref[:, 0:width]
        for r0 in range(0, x_ref.shape[0], chunk):
            rows = slice(r0, r0 + chunk)
            x = x_ref[rows, :]
            g = jnp.dot(x, wg, preferred_element_type=F32)
            u = jnp.dot(x, wu, preferred_element_type=F32)
            o_ref[rows, 0:width] = (g * jax.nn.sigmoid(g) * u).astype(o_ref.dtype)
            if width < tn:
                o_ref[rows, width:] = jnp.zeros((chunk, tn - width), o_ref.dtype)

    if last_w == tn:
        tile(tn)
    else:
        pl.when(pl.program_id(0) < last)(lambda: tile(tn))
        pl.when(pl.program_id(0) == last)(lambda: tile(last_w))
    step = pl.program_id(0) * pl.num_programs(1) + pl.program_id(1)
    wdb_ref[...] = jnp.where(step < wd_blocks, wd_ref[...], 0.0).astype(wdb_ref.dtype)


def _ffn_up(x_bf, wg, wu, wd, layer, *, tm, tn, n_out):
    m, k = x_bf.shape
    hidden, dout = wd.shape[1:]
    nj, ni = n_out // tn, m // tm
    slab = n_out // (nj * ni)
    assert slab * nj * ni == n_out and hidden % slab == 0 and slab % (2 * V7X_SUBLANES) == 0
    wd_blocks = hidden // slab
    hbm = pl.BlockSpec(memory_space=pl.ANY)
    return pl.pallas_call(
        functools.partial(_ffn_up_kernel, layer=layer, n_valid=wg.shape[2], nj=nj, wd_blocks=wd_blocks),
        grid=(nj, ni),
        in_specs=[pl.BlockSpec((tm, k), lambda j, i: (i, 0)), hbm, hbm,
                  pl.BlockSpec((None, slab, dout),
                               lambda j, i: (layer, jnp.minimum(j * ni + i, wd_blocks - 1), 0))],
        out_specs=[pl.BlockSpec((tm, tn), lambda j, i: (i, j)),
                   pl.BlockSpec((slab, dout), lambda j, i: (j * ni + i, 0))],
        out_shape=[jax.ShapeDtypeStruct((m, n_out), BF16), jax.ShapeDtypeStruct((n_out, dout), BF16)],
        scratch_shapes=[pltpu.VMEM((k, tn), F32), pltpu.VMEM((k, tn), F32),
                        pltpu.VMEM((k, tn), BF16), pltpu.VMEM((k, tn), BF16),
                        pltpu.SemaphoreType.DMA((2,))],
        compiler_params=_params("arbitrary", "arbitrary", buffers=[
            ((tm, k), BF16, 2), ((k, tn), F32, 2), ((k, tn), BF16, 2), ((tm, tn), BF16, 2),
            ((slab, dout), F32, 2), ((slab, dout), BF16, 2)]),
        name="ffn_up",
    )(x_bf, wg, wu, wd)


def _mm_resid_kernel(*refs, k_parts, nk):
    n_a = len(k_parts)
    a_refs = refs[:n_a]
    w_ref, *r_refs, o_ref = refs[n_a:]
    k = pl.program_id(2)

    def product():
        acc, k0 = None, 0
        for a_ref, kp in zip(a_refs, k_parts):
            d = jnp.dot(a_ref[...], w_ref[k0:k0 + kp, :], preferred_element_type=F32)
            acc = d if acc is None else acc + d
            k0 += kp
        return acc

    def residual():
        if len(r_refs) == 1:
            return r_refs[0][...]
        z_ref, mu_ref, rs_ref, g_ref, b_ref = r_refs
        nrep = z_ref.shape[1] // mu_ref.shape[1]
        mu, rs = jnp.tile(mu_ref[...], (1, nrep)), jnp.tile(rs_ref[...], (1, nrep))
        return (z_ref[...] - mu) * rs * g_ref[...] + b_ref[...]

    if nk == 1:
        o_ref[...] = product() + DEEPNORM_ALPHA * residual()
    else:
        @pl.when(k == 0)
        def _():
            o_ref[...] = product() + DEEPNORM_ALPHA * residual()

        @pl.when(k > 0)
        def _():
            o_ref[...] += product()


class _NormedRows(NamedTuple):
    z: jax.Array
    mean: jax.Array
    rstd: jax.Array
    gain: jax.Array
    bias: jax.Array


def _matmul_resid(a_parts, w, resid, *, tm, tn, tk, name):
    tile = pl.BlockSpec((tm, tn), lambda i, j, k: (i, j))
    if isinstance(resid, _NormedRows):
        m, n = resid.z.shape
        stat = pl.BlockSpec((tm, V7X_LANES), lambda i, j, k: (i, 0))
        vec = pl.BlockSpec((1, tn), lambda i, j, k: (0, j))
        r_ops = [resid.z, resid.mean, resid.rstd, resid.gain.reshape(1, n), resid.bias.reshape(1, n)]
        r_specs = [tile, stat, stat, vec, vec]
    else:
        m, n = resid.shape
        r_ops, r_specs = [resid], [tile]
    kdim = w.shape[0]
    if len(a_parts) > 1:
        assert tk == kdim
        k_parts = tuple(a.shape[1] for a in a_parts)
        a_specs = [pl.BlockSpec((tm, kp), lambda i, j, k: (i, 0)) for kp in k_parts]
    else:
        k_parts = (tk,)
        a_specs = [pl.BlockSpec((tm, tk), lambda i, j, k: (i, k))]
    nk = kdim // tk
    return pl.pallas_call(
        functools.partial(_mm_resid_kernel, k_parts=k_parts, nk=nk),
        grid=(m // tm, n // tn, nk),
        in_specs=a_specs + [pl.BlockSpec((tk, tn), lambda i, j, k: (k, j))] + r_specs,
        out_specs=tile,
        out_shape=jax.ShapeDtypeStruct((m, n), F32),
        compiler_params=_params("parallel", "parallel", "arbitrary"),
        name=name,
    )(*a_parts, w, *r_ops)


def _ln_kernel(z_ref, g_ref, b_ref, y_ref, mu_ref, rs_ref):
    inv_n = 1.0 / z_ref.shape[1]
    nblk = z_ref.shape[0] // LN_ROW_BLOCK
    lanes = mu_ref.shape[1]
    nrep = z_ref.shape[1] // lanes

    def block(r):
        start = r * LN_ROW_BLOCK
        return pl.ds(start if isinstance(r, int) else pl.multiple_of(start, LN_ROW_BLOCK), LN_ROW_BLOCK)

    def mean_rows(r):
        rows = block(r)
        mu = jnp.sum(z_ref[rows, :], axis=-1, keepdims=True) * inv_n
        mu_ref[rows, :] = jnp.broadcast_to(mu, (LN_ROW_BLOCK, lanes))

    def rstd_rows(r):
        rows = block(r)
        zc = z_ref[rows, :] - jnp.tile(mu_ref[rows, :], (1, nrep))
        var = jnp.sum(zc * zc, axis=-1, keepdims=True) * inv_n
        rs_ref[rows, :] = jnp.broadcast_to(lax.rsqrt(var + NORM_EPS), (LN_ROW_BLOCK, lanes))

    def norm_rows(r):
        rows = block(r)
        mu = jnp.tile(mu_ref[rows, :], (1, LN_COL_BLOCK // lanes))
        rs = jnp.tile(rs_ref[rows, :], (1, LN_COL_BLOCK // lanes))
        for c0 in range(0, z_ref.shape[1], LN_COL_BLOCK):
            cols = slice(c0, c0 + LN_COL_BLOCK)
            y = (z_ref[rows, cols] - mu) * rs * g_ref[:, cols] + b_ref[:, cols]
            y_ref[rows, cols] = y.astype(y_ref.dtype)

    lag = LN_STAGE_LAG

    def trip(r, with_mean=True, with_rstd=True, with_norm=True):
        if with_norm:
            norm_rows(r - 2 * lag)
        if with_rstd:
            rstd_rows(r - lag)
        if with_mean:
            mean_rows(r)

    for r in range(2 * lag):
        trip(r, with_rstd=r >= lag, with_norm=False)
    lax.fori_loop(2 * lag, nblk, lambda r, c: (trip(r), c)[1], 0, unroll=2)
    for r in range(nblk, nblk + 2 * lag):
        trip(r, with_mean=False, with_rstd=r - lag < nblk)


def _layernorm(z, gain, bias, *, tm, out_dtype):
    m, n = z.shape
    row_spec = pl.BlockSpec((tm, n), lambda i: (i, 0))
    vec_spec = pl.BlockSpec((1, n), lambda i: (0, 0))
    stat_spec = pl.BlockSpec((tm, V7X_LANES), lambda i: (i, 0))
    stat_shape = jax.ShapeDtypeStruct((m, V7X_LANES), F32)
    return pl.pallas_call(
        _ln_kernel,
        grid=(m // tm,),
        in_specs=[row_spec, vec_spec, vec_spec],
        out_specs=[row_spec, stat_spec, stat_spec],
        out_shape=[jax.ShapeDtypeStruct((m, n), out_dtype), stat_shape, stat_shape],
        compiler_params=_params("parallel"),
        name="layernorm",
    )(z, gain.reshape(1, n), bias.reshape(1, n))


def _norm_stage(z, gain, bias):
    y_bf, mean, rstd = _layernorm(z, gain, bias, tm=LN_TM, out_dtype=BF16)
    return _NormedRows(z, mean, rstd, gain, bias), y_bf


def _chunk_steps(qs, ks, vs, las, st_ref, tril_bf, causal):
    c = qs[0].shape[0]
    nt = (((1,), (1,)), ((), ()))
    tn = (((0,), (0,)), ((), ()))
    heads = range(len(qs))
    bs = []
    for la in las:
        la_hi = la.astype(BF16)
        la_lo = (la - la_hi.astype(F32)).astype(BF16)
        bs.append(jnp.dot(tril_bf, la_hi, preferred_element_type=F32)
                  + jnp.dot(tril_bf, la_lo, preferred_element_type=F32))
    q_decs, scores, kvs, inters, decays = [], [], [], [], []
    for hd in heads:
        b = bs[hd]
        b_end = b[c - 1:c, :]
        decay = jnp.exp(b_end)
        q_dec = (qs[hd] * jnp.exp(b)).astype(BF16)
        k_inv = ks[hd] * jnp.exp(-b)
        k_end = (k_inv * decay).astype(BF16)
        k_inv = k_inv.astype(BF16)
        v_bf = vs[hd].astype(BF16)
        vs[hd] = v_bf
        scores.append(lax.dot_general(q_dec, k_inv, nt, preferred_element_type=F32))
        kvs.append(lax.dot_general(v_bf, k_end, tn, preferred_element_type=F32))
        inters.append(lax.dot_general(q_dec, st_ref[hd].astype(BF16), nt, preferred_element_type=F32))
        decays.append(decay)
    outs = []
    for hd in heads:
        sc = jnp.where(causal, scores[hd], 0.0).astype(BF16)
        outs.append(jnp.dot(sc, vs[hd], preferred_element_type=F32) + inters[hd])
        st_ref[hd] = decays[hd] * st_ref[hd] + kvs[hd]
    return outs


def _gated_rmsnorm(o, gain, gate):
    ms = jnp.mean(o * o, axis=-1, keepdims=True)
    return o * lax.rsqrt(ms + NORM_EPS) * gain * (gate * jax.nn.sigmoid(gate))


def _chunk_masks():
    row = lax.broadcasted_iota(jnp.int32, (CHUNK, CHUNK), 0)
    col = lax.broadcasted_iota(jnp.int32, (CHUNK, CHUNK), 1)
    causal = row >= col
    return jnp.where(causal, 1.0, 0.0).astype(BF16), causal


def _log_sigmoid(z):
    return jnp.minimum(z, 0.0) - jnp.log1p(jnp.exp(-jnp.abs(z)))


def _gla_kernel(q_ref, k_ref, v_ref, g_ref, alr_ref, wa_ref, ba_ref, gain_ref, o_ref, st_ref,
                *, hb, nchunks, unroll):
    @pl.when(pl.program_id(2) == 0)
    def _():
        st_ref[...] = jnp.zeros_like(st_ref)

    tril_bf, causal = _chunk_masks()
    gain = gain_ref[...]
    ba = ba_ref[...]
    wa = wa_ref[...]

    def body(c, carry):
        rows = pl.ds(pl.multiple_of(c * CHUNK, CHUNK), CHUNK)
        z = jnp.dot(alr_ref[rows, :].astype(BF16), wa, preferred_element_type=F32) + ba
        la_all = _log_sigmoid(z) / GLA_TAU
        kcs = [slice(hd * GLA_DK, (hd + 1) * GLA_DK) for hd in range(hb)]
        vcs = [slice(hd * GLA_DV, (hd + 1) * GLA_DV) for hd in range(hb)]
        outs = _chunk_steps([q_ref[rows, kc] * (GLA_DK ** -0.5) for kc in kcs],
                            [k_ref[rows, kc] for kc in kcs], [v_ref[rows, vc] for vc in vcs],
                            [la_all[:, kc] for kc in kcs], st_ref, tril_bf, causal)
        for o, vc in zip(outs, vcs):
            o_ref[rows, vc] = _gated_rmsnorm(o, gain, g_ref[rows, vc]).astype(o_ref.dtype)
        return carry

    lax.fori_loop(0, nchunks, body, 0, unroll=unroll)


def _gla(h, alr, w_alpha, b_alpha, norm_g, *, bsz, seq, col_q, col_k, col_v, col_g, tt, hb, unroll):
    n = bsz * seq
    nt = seq // tt
    kw, vw = hb * GLA_DK, hb * GLA_DV
    hg = GLA_HEADS // hb

    def rows(b, g, t):
        return b * nt + t

    return pl.pallas_call(
        functools.partial(_gla_kernel, hb=hb, nchunks=tt // CHUNK, unroll=unroll),
        grid=(bsz, hg, nt),
        in_specs=[pl.BlockSpec((tt, kw), lambda b, g, t: (rows(b, g, t), col_q // kw + g)),
                  pl.BlockSpec((tt, kw), lambda b, g, t: (rows(b, g, t), col_k // kw + g)),
                  pl.BlockSpec((tt, vw), lambda b, g, t: (rows(b, g, t), col_v // vw + g)),
                  pl.BlockSpec((tt, vw), lambda b, g, t: (rows(b, g, t), col_g // vw + g)),
                  pl.BlockSpec((tt, V7X_LANES), lambda b, g, t: (rows(b, g, t), 0)),
                  pl.BlockSpec((V7X_LANES, kw), lambda b, g, t: (0, g)),
                  pl.BlockSpec((1, kw), lambda b, g, t: (0, g)),
                  pl.BlockSpec((1, GLA_DV), lambda b, g, t: (0, 0))],
        out_specs=pl.BlockSpec((tt, vw), lambda b, g, t: (rows(b, g, t), g)),
        out_shape=jax.ShapeDtypeStruct((n, GLA_HEADS * GLA_DV), BF16),
        scratch_shapes=[pltpu.VMEM((hb, GLA_DV, GLA_DK), F32)],
        compiler_params=_params("parallel", "parallel", "arbitrary"),
        name="gla",
    )(h, h, h, h, alr, w_alpha, b_alpha, norm_g)


def _hgrn_kernel(q_ref, f_ref, i_ref, g_ref, tab_ref, gain_ref, o_ref, st_ref, *, layer, hb, nchunks,
                 unroll):
    @pl.when(pl.program_id(2) == 0)
    def _():
        st_ref[...] = jnp.zeros_like(st_ref)

    tab = tab_ref[...]
    e = jnp.exp(tab - jnp.max(tab, axis=0, keepdims=True))
    soft = e / jnp.sum(e, axis=0, keepdims=True)
    lb_all = jnp.sum(soft[0:layer + 1], axis=0, keepdims=True) - soft[0:1]
    tril_bf, causal = _chunk_masks()
    gain = gain_ref[...]

    def body(c, carry):
        rows = pl.ds(pl.multiple_of(c * CHUNK, CHUNK), CHUNK)
        kcs = [slice(hd * HG_DK, (hd + 1) * HG_DK) for hd in range(hb)]
        vcs = [slice(hd * HG_DV, (hd + 1) * HG_DV) for hd in range(hb)]
        fs = [lb_all[:, kc] + (1.0 - lb_all[:, kc]) * jax.nn.sigmoid(f_ref[rows, kc]) for kc in kcs]
        ivs = [i_ref[rows, vc] for vc in vcs]
        outs = _chunk_steps([q_ref[rows, kc] for kc in kcs], [1.0 - f for f in fs],
                            [iv * jax.nn.sigmoid(iv) for iv in ivs], [jnp.log(f) for f in fs],
                            st_ref, tril_bf, causal)
        for o, vc in zip(outs, vcs):
            o_ref[rows, vc] = _gated_rmsnorm(o, gain, g_ref[rows, vc]).astype(o_ref.dtype)
        return carry

    lax.fori_loop(0, nchunks, body, 0, unroll=unroll)


def _hgrn(h, lb_table, norm_g, *, layer, bsz, seq, heads, tt, hb, unroll):
    n = bsz * seq
    nt = seq // tt
    w = hb * HG_DK
    hg = heads // hb

    def spec(seg):
        return pl.BlockSpec((tt, w), lambda b, g, t: (b * nt + t, seg * hg + g))

    return pl.pallas_call(
        functools.partial(_hgrn_kernel, layer=layer, hb=hb, nchunks=tt // CHUNK, unroll=unroll),
        grid=(bsz, hg, nt),
        in_specs=[spec(0), spec(1), spec(2), spec(3),
                  pl.BlockSpec((DEPTH, w), lambda b, g, t: (0, g)),
                  pl.BlockSpec((1, HG_DV), lambda b, g, t: (0, 0))],
        out_specs=pl.BlockSpec((tt, w), lambda b, g, t: (b * nt + t, g)),
        out_shape=jax.ShapeDtypeStruct((n, heads * HG_DV), BF16),
        scratch_shapes=[pltpu.VMEM((hb, HG_DV, HG_DK), F32)],
        compiler_params=_params("parallel", "parallel", "arbitrary"),
        name="hgrn2",
    )(h, h, h, h, lb_table, norm_g)


def _s5_prep_cast_kernel(lr_row_ref, li_row_ref, lr_col_ref, li_col_ref, ldt_ref,
                         btr_ref, bti_ref, ctr_ref, cti_ref, src_ref, wgate_ref,
                         w1_ref, f_ref, coef_ref, dst_ref, gate_ref):
    _s5_prep_kernel(lr_row_ref, li_row_ref, lr_col_ref, li_col_ref, ldt_ref,
                    btr_ref, bti_ref, ctr_ref, cti_ref, w1_ref, f_ref, coef_ref)
    x = src_ref[...].astype(dst_ref.dtype)
    dst_ref[...] = x
    gate_ref[...] = jnp.dot(x, wgate_ref[...], preferred_element_type=F32)


def _s5_prep_kernel(lr_row_ref, li_row_ref, lr_col_ref, li_col_ref, ldt_ref,
                    btr_ref, bti_ref, ctr_ref, cti_ref, w1_ref, f_ref, coef_ref):
    L, J, P = S5_CHUNK, S5_GROUP, S5_STATE
    hi = lax.Precision.HIGHEST
    dt = jnp.exp(ldt_ref[0])
    lr, li = lr_row_ref[0], li_row_ref[0]
    npow = ((L - 1) - lax.broadcasted_iota(jnp.int32, (L, 2 * P), 0)).astype(F32)
    mag = jnp.exp(npow * (lr * dt))
    th = npow * (li * dt)
    ar, ai = mag * jnp.cos(th), mag * jnp.sin(th)
    half = lax.broadcasted_iota(jnp.int32, (L, 2 * P), 1) < P
    a1r, a1i = ar[L - 2:L - 1], ai[L - 2:L - 1]
    nr, ni = a1r - 1.0, a1i
    den = lr * lr + li * li
    fr, fi = (nr * lr + ni * li) / den, (ni * lr - nr * li) / den
    btr, bti = btr_ref[0], bti_ref[0]
    bbr, bbi = fr * btr - fi * bti, fr * bti + fi * btr
    pa, pb = jnp.where(half, ar, ai), jnp.where(half, -ai, ar)
    for s in range(L):
        rows = slice(s * J, (s + 1) * J)
        w1_ref[0, rows, L * J:] = (bbr * pa[s:s + 1] + bbi * pb[s:s + 1]).astype(BF16)
    alr_, ali_ = ar[0:1] * a1r - ai[0:1] * a1i, ar[0:1] * a1i + ai[0:1] * a1r
    half1 = half[0:1]
    coef_ref[0, 0:1, :] = alr_
    coef_ref[0, 1:2, :] = jnp.where(half1, -ali_, ali_)
    coef_ref[0, 2:, :] = jnp.zeros((V7X_SUBLANES - 2, 2 * P), F32)

    lrc, lic = lr_col_ref[0], li_col_ref[0]
    lane = lax.broadcasted_iota(jnp.int32, (P, L * J), 1)
    lag = jnp.right_shift(lane, int(math.log2(J))).astype(F32)
    magq = jnp.exp(lag * (lrc * dt))
    thq = lag * (lic * dt)
    aqr, aqi = magq * jnp.cos(thq), magq * jnp.sin(thq)
    cr, ci = ctr_ref[0], cti_ref[0]
    qr, qi = cr * aqr - ci * aqi, cr * aqi + ci * aqr
    r = (jnp.dot(bbr[:, :P], qr, precision=hi, preferred_element_type=F32)
         - jnp.dot(bbi[:, :P], qi, precision=hi, preferred_element_type=F32))
    lane_r = lax.broadcasted_iota(jnp.int32, (J, L * J), 1)
    for s in range(L):
        blk = r if s == 0 else jnp.where(lane_r >= s * J, pltpu.roll(r, s * J, 1), 0.0)
        w1_ref[0, s * J:(s + 1) * J, 0:L * J] = blk.astype(BF16)
    c1r, c1i = aqr[:, J:J + 1], aqi[:, J:J + 1]
    f_ref[0, 0:P, :] = (qr * c1r - qi * c1i).astype(BF16)
    f_ref[0, P:, :] = (-(qr * c1i + qi * c1r)).astype(BF16)


def _s5_prep(lam_re, lam_im, log_dt, b_re, b_im, c_re, c_im, cast_src=None, gate_w=None):
    g = lam_re.shape[0]
    P, J, L = S5_STATE, S5_GROUP, S5_CHUNK
    dbl = lambda a: jnp.concatenate([a, a], axis=-1)
    lr_row = dbl(lam_re).reshape(g, 1, 2 * P)
    li_row = dbl(lam_im).reshape(g, 1, 2 * P)
    lr_col = lam_re.reshape(g, P, 1)
    li_col = lam_im.reshape(g, P, 1)
    ldt = log_dt.reshape(g, 1, 1)
    btr = dbl(jnp.swapaxes(b_re, 1, 2))
    bti = dbl(jnp.swapaxes(b_im, 1, 2))
    ctr = jnp.tile(jnp.swapaxes(c_re, 1, 2), (1, 1, L))
    cti = jnp.tile(jnp.swapaxes(c_im, 1, 2), (1, 1, L))
    blk = lambda *s: pl.BlockSpec((1,) + s, lambda i: (i, 0, 0))
    operands = [lr_row, li_row, lr_col, li_col, ldt, btr, bti, ctr, cti]
    in_specs = [blk(1, 2 * P), blk(1, 2 * P), blk(P, 1), blk(P, 1), blk(1, 1),
                blk(J, 2 * P), blk(J, 2 * P), blk(P, L * J), blk(P, L * J)]
    out_specs = [blk(L * J, L * J + 2 * P), blk(2 * P, L * J), blk(V7X_SUBLANES, 2 * P)]
    out_shape = [jax.ShapeDtypeStruct((g, L * J, L * J + 2 * P), BF16),
                 jax.ShapeDtypeStruct((g, 2 * P, L * J), BF16),
                 jax.ShapeDtypeStruct((g, V7X_SUBLANES, 2 * P), F32)]
    body = _s5_prep_kernel
    if cast_src is not None:
        m, d = cast_src.shape
        slab = pl.BlockSpec((m // g, d), lambda i: (i, 0))
        gate_slab = pl.BlockSpec((m // g, gate_w.shape[1]), lambda i: (i, 0))
        operands += [cast_src, gate_w]
        in_specs += [slab, pl.BlockSpec(gate_w.shape, lambda i: (0, 0))]
        out_specs += [slab, gate_slab]
        out_shape += [jax.ShapeDtypeStruct((m, d), BF16), jax.ShapeDtypeStruct((m, gate_w.shape[1]), F32)]
        body = _s5_prep_cast_kernel
    return pl.pallas_call(body, grid=(g,), in_specs=in_specs, out_specs=out_specs, out_shape=out_shape,
                          compiler_params=_params("parallel"), name="s5_prep")(*operands)


def _s5_assemble_kernel(w1_ref, f_ref, coef_ref, wbig_ref, fbig_ref, cbig_ref):
    L, J, P, Q = S5_CHUNK, S5_GROUP, S5_STATE, S5_LANE_GROUPS
    lanes = Q * J
    width = L * lanes
    nstate = 2 * Q * P
    jbits, pbits = int(math.log2(J)), int(math.log2(P))

    def iota(shape, axis):
        return lax.broadcasted_iota(jnp.int32, shape, axis)

    a, b = iota((L * J, width), 0), iota((L * J, width), 1)
    sel_sc = jnp.where(((b >> int(math.log2(lanes))) == (a >> jbits)) & ((b & (J - 1)) == (a & (J - 1))),
                       1.0, 0.0).astype(BF16)
    a, b = iota((2 * P, nstate), 0), iota((2 * P, nstate), 1)
    sel_st = jnp.where(((b >> int(math.log2(Q * P))) == (a >> pbits)) & ((b & (P - 1)) == (a & (P - 1))),
                       1.0, 0.0).astype(BF16)
    col_group_sc = (iota((lanes, width), 1) >> jbits) & (Q - 1)
    col_group_st = (iota((lanes, nstate), 1) >> pbits) & (Q - 1)
    row_group_j = iota((lanes, 1), 0) >> jbits

    r_stack = jnp.concatenate([w1_ref[g, 0:J, 0:L * J] for g in range(Q)], axis=0)
    r_big = jnp.dot(r_stack, sel_sc, preferred_element_type=F32)
    r_big = jnp.where(col_group_sc == row_group_j, r_big, 0.0).astype(BF16)
    for s in range(L):
        rows = slice(s * lanes, (s + 1) * lanes)
        if s:
            wbig_ref[0, rows, 0:s * lanes] = jnp.zeros((lanes, s * lanes), BF16)
        wbig_ref[0, rows, s * lanes:width] = r_big[:, 0:width - s * lanes]
        e_stack = jnp.concatenate([w1_ref[g, s * J:(s + 1) * J, L * J:] for g in range(Q)], axis=0)
        e_big = jnp.dot(e_stack, sel_st, preferred_element_type=F32)
        wbig_ref[0, rows, width:] = jnp.where(col_group_st == row_group_j, e_big, 0.0).astype(BF16)

    row_in_pair = iota((lanes, 1), 0) >> pbits
    for m in range(nstate // lanes):
        ri, g0 = divmod(m * (lanes // P), Q)
        f_stack = jnp.concatenate([f_ref[g0 + t, ri * P:(ri + 1) * P, :] for t in range(lanes // P)], axis=0)
        f_big = jnp.dot(f_stack, sel_sc, preferred_element_type=F32)
        fbig_ref[0, m * lanes:(m + 1) * lanes, :] = jnp.where(
            col_group_sc == g0 + row_in_pair, f_big, 0.0).astype(BF16)

    low = iota((V7X_SUBLANES, lanes), 1) < P
    for k in range(Q // 2):
        c0, c1 = coef_ref[2 * k], coef_ref[2 * k + 1]
        row_is_a2 = iota((V7X_SUBLANES, lanes), 0) == 1
        re_half = jnp.where(low, c0, jnp.where(row_is_a2, -c1, c1))
        im_half = jnp.where(low, jnp.where(row_is_a2, -c0, c0), c1)
        cbig_ref[0, :, k * lanes:(k + 1) * lanes] = re_half
        cbig_ref[0, :, Q * P + k * lanes:Q * P + (k + 1) * lanes] = im_half


def _s5_assemble(w1, fmat, coef):
    g = w1.shape[0]
    L, J, P, Q = S5_CHUNK, S5_GROUP, S5_STATE, S5_LANE_GROUPS
    nb = g // Q
    width, nstate = L * Q * J, 2 * Q * P
    grp = lambda *s: pl.BlockSpec((Q,) + s, lambda i: (i, 0, 0))
    out = lambda *s: pl.BlockSpec((1,) + s, lambda i: (i, 0, 0))
    return pl.pallas_call(
        _s5_assemble_kernel,
        grid=(nb,),
        in_specs=[grp(L * J, L * J + 2 * P), grp(2 * P, L * J), grp(V7X_SUBLANES, 2 * P)],
        out_specs=[out(width, width + nstate), out(nstate, width), out(V7X_SUBLANES, nstate)],
        out_shape=[jax.ShapeDtypeStruct((nb, width, width + nstate), BF16),
                   jax.ShapeDtypeStruct((nb, nstate, width), BF16),
                   jax.ShapeDtypeStruct((nb, V7X_SUBLANES, nstate), F32)],
        compiler_params=_params("parallel"),
        name="s5_assemble",
    )(w1, fmat, coef)


def _s5_main_kernel(u_ref, w_ref, f_ref, coef_ref, o_ref, ys_ref, sp_ref, st_ref, *, nchunk, nb):
    width = u_ref.shape[2]
    half = st_ref.shape[1] // 2

    @pl.when(pl.program_id(1) == 0)
    def _():
        st_ref[...] = jnp.zeros_like(st_ref)

    for c0 in range(0, width, S5_TRI_COLS):
        c1 = c0 + S5_TRI_COLS
        ys_ref[:, c0:c1] = jnp.dot(u_ref[0, :, 0:c1], w_ref[0, 0:c1, c0:c1], preferred_element_type=F32)
    ys_ref[:, width:] = jnp.dot(u_ref[0], w_ref[0, :, width:], preferred_element_type=F32)
    a1, a2 = coef_ref[0, 0:1, :], coef_ref[0, 1:2, :]

    def body(c, s):
        rows = pl.ds(pl.multiple_of(c * nb, nb), nb)
        sp_ref[rows, :] = s
        swapped = jnp.concatenate([s[:, half:], s[:, :half]], axis=1)
        return a1 * s + a2 * swapped + ys_ref[rows, width:]

    st_ref[...] = lax.fori_loop(0, nchunk, body, st_ref[...], unroll=4)
    y = ys_ref[:, 0:width] + jnp.dot(sp_ref[...].astype(BF16), f_ref[0], preferred_element_type=F32)
    o_ref[0] = y.astype(o_ref.dtype)


def _s5_main(u_c, wbig, fbig, cbig, *, nb, rows_per_step):
    nblk, rows, width = u_c.shape
    nstate = fbig.shape[1]
    nsteps = rows // rows_per_step
    fixed = lambda *s: pl.BlockSpec((1,) + s, lambda i, r: (i, 0, 0))
    moving = pl.BlockSpec((1, rows_per_step, width), lambda i, r: (i, r, 0))
    return pl.pallas_call(
        functools.partial(_s5_main_kernel, nchunk=rows_per_step // nb, nb=nb),
        grid=(nblk, nsteps),
        in_specs=[moving, fixed(width, width + nstate), fixed(nstate, width),
                  fixed(V7X_SUBLANES, nstate)],
        out_specs=moving,
        out_shape=jax.ShapeDtypeStruct((nblk, rows, width), BF16),
        scratch_shapes=[pltpu.VMEM((rows_per_step, width + nstate), F32),
                        pltpu.VMEM((rows_per_step, nstate), F32),
                        pltpu.VMEM((nb, nstate), F32)],
        compiler_params=_params("parallel", "arbitrary"),
        name="s5_main",
    )(u_c, wbig, fbig, cbig)


def _s5_glu_kernel(ys_ref, u_ref, d_ref, w_ref, o_ref):
    y = jax.nn.gelu(ys_ref[...].astype(F32) + d_ref[...] * u_ref[...], approximate=True)
    z = jnp.dot(y.astype(BF16), w_ref[...], preferred_element_type=F32)
    o_ref[...] = (y * jax.nn.sigmoid(z)).astype(o_ref.dtype)


def _s5_glu(y_ssm, h, d_skip, w_glu, *, tm):
    m, n = y_ssm.shape
    return pl.pallas_call(
        _s5_glu_kernel,
        grid=(m // tm,),
        in_specs=[pl.BlockSpec((tm, n), lambda i: (i, 0)),
                  pl.BlockSpec((tm, n), lambda i: (i, 0)),
                  pl.BlockSpec((1, n), lambda i: (0, 0)),
                  pl.BlockSpec((n, n), lambda i: (0, 0))],
        out_specs=pl.BlockSpec((tm, n), lambda i: (i, 0)),
        out_shape=jax.ShapeDtypeStruct((m, n), BF16),
        compiler_params=_params("parallel"),
        name="s5_glu",
    )(y_ssm, h, d_skip.reshape(1, n), w_glu)


def _pad_cols(w, n):
    return jnp.pad(w, ((0, 0), (0, n - w.shape[1])))


def _ffn_block(x_res, x_bf, w_gate, w_up, w_down, layer, ln_g, ln_b, *, final):
    hidden = w_gate.shape[2]
    hpad = -(-hidden // FFN_DOWN_TK) * FFN_DOWN_TK
    a, wd = _ffn_up(x_bf, w_gate, w_up, w_down, layer, tm=1024, tn=512, n_out=hpad)
    z = _matmul_resid([a], wd, x_res, tm=1024, tn=1024, tk=FFN_DOWN_TK, name="ffn_down")
    if final:
        return _layernorm(z, ln_g, ln_b, tm=LN_TM, out_dtype=F32)[0], None
    return _norm_stage(z, ln_g, ln_b)


def _even_layer(x_res, x_bf, bsz, seq, e, w_in, w_out, lam_re, lam_im, log_dt, b_re, b_im, c_re, c_im,
                d_skip, w_glu, w_alpha, b_alpha, norm_g, ln_g, ln_b):
    n = bsz * seq
    s5w = d_skip.shape[0]
    groups = s5w // S5_GROUP
    kw = GLA_HEADS * GLA_DK
    vw = GLA_HEADS * GLA_DV
    main = s5w + 2 * kw + 2 * vw
    w_gate = _pad_cols(w_in[e, :, main:], V7X_LANES)
    if x_bf is None:
        w1, fmat, coef, x_bf, alr = _s5_prep(lam_re, lam_im, log_dt, b_re, b_im, c_re, c_im,
                                             cast_src=x_res, gate_w=w_gate.astype(BF16))
    else:
        w1, fmat, coef = _s5_prep(lam_re, lam_im, log_dt, b_re, b_im, c_re, c_im)
        alr = _matmul(x_bf, w_gate[None], 0, n=V7X_LANES, tm=1024, tn=V7X_LANES, out_dtype=F32,
                      name="even_in_gate")
    h = _matmul(x_bf, w_in, e, n=main, tm=1024, tn=IN_PROJ_TN, out_dtype=F32, name="even_in")

    wbig, fbig, cbig = _s5_assemble(w1, fmat, coef)
    nchunk = seq // S5_CHUNK
    nblk = s5w // V7X_LANES
    u_c = (h[:, :s5w].astype(BF16).reshape(bsz, nchunk, S5_CHUNK, nblk, V7X_LANES)
           .transpose(3, 1, 0, 2, 4).reshape(nblk, nchunk * bsz, S5_CHUNK * V7X_LANES))
    y_c = _s5_main(u_c, wbig, fbig, cbig, nb=bsz, rows_per_step=256)
    y_ssm = (y_c.reshape(nblk, nchunk, bsz, S5_CHUNK, V7X_LANES)
             .transpose(2, 1, 3, 0, 4).reshape(n, s5w))
    y_s5 = _s5_glu(y_ssm, h, d_skip, w_glu.astype(BF16), tm=512)

    wa = jnp.pad(w_alpha, ((0, V7X_LANES - w_alpha.shape[0]), (0, 0))).astype(BF16)
    y_gla = _gla(h, alr, wa, b_alpha.reshape(1, kw), norm_g.reshape(1, GLA_DV), bsz=bsz, seq=seq,
                 col_q=s5w, col_k=s5w + kw, col_v=s5w + 2 * kw, col_g=s5w + 2 * kw + vw,
                 tt=512, hb=8, unroll=2)

    z = _matmul_resid([y_s5, y_gla], w_out.astype(BF16), x_res, tm=1024, tn=1024, tk=w_out.shape[0],
                      name="even_out")
    return _norm_stage(z, ln_g, ln_b)


def _odd_layer(x_res, x_bf, bsz, seq, layer, o, w_in, w_out, lb_table, norm_g, ln_g, ln_b):
    heads = w_out.shape[0] // HG_DV
    h = _matmul(x_bf, w_in, o, n=w_in.shape[2], tm=1024, tn=IN_PROJ_TN, out_dtype=F32, name="odd_in")
    y = _hgrn(h, lb_table, norm_g.reshape(1, HG_DV), layer=layer, bsz=bsz, seq=seq, heads=heads,
              tt=512, hb=16, unroll=2)
    z = _matmul_resid([y], w_out.astype(BF16), x_res, tm=1024, tn=1024, tk=w_out.shape[0], name="odd_out")
    return _norm_stage(z, ln_g, ln_b)


def kernel(x, ev_w_in, ev_w_out, s5_lam_re, s5_lam_im, s5_log_dt, s5_b_re, s5_b_im, s5_c_re, s5_c_im,
           s5_d, s5_w_glu, gla_w_alpha, gla_b_alpha, gla_norm_g, od_w_in, od_w_out, hg_lb_table,
           hg_norm_g, ln_mix_g, ln_mix_b, ln_ffn_g, ln_ffn_b, ffn_w_gate, ffn_w_up, ffn_w_down):
    bsz, seq, d = x.shape
    depth = ln_mix_g.shape[0]
    assert depth == DEPTH
    xf = x.reshape(bsz * seq, d)
    xb = None
    for layer in range(depth):
        if layer % 2 == 0:
            e = layer // 2
            xf, xb = _even_layer(xf, xb, bsz, seq, e, ev_w_in, ev_w_out[e], s5_lam_re[e], s5_lam_im[e],
                                 s5_log_dt[e], s5_b_re[e], s5_b_im[e], s5_c_re[e], s5_c_im[e], s5_d[e],
                                 s5_w_glu[e], gla_w_alpha[e], gla_b_alpha[e], gla_norm_g[e],
                                 ln_mix_g[layer], ln_mix_b[layer])
        else:
            o = layer // 2
            xf, xb = _odd_layer(xf, xb, bsz, seq, layer, o, od_w_in, od_w_out[o], hg_lb_table,
                                hg_norm_g[o], ln_mix_g[layer], ln_mix_b[layer])
        xf, xb = _ffn_block(xf, xb, ffn_w_gate, ffn_w_up, ffn_w_down, layer,
                            ln_ffn_g[layer], ln_ffn_b[layer], final=layer == depth - 1)
    return xf.reshape(bsz, seq, d).astype(x.dtype)
```

```python
import functools
import math
from typing import NamedTuple

import jax
import jax.numpy as jnp
from jax import lax
from jax.experimental import pallas as pl
from jax.experimental.pallas import tpu as pltpu

F32 = jnp.float32
BF16 = jnp.bfloat16

DEPTH = 2
S5_GROUP = 16
S5_STATE = 64
GLA_HEADS = 8
GLA_DK = 128
GLA_DV = 256
GLA_TAU = 16.0
HG_DK = 128
HG_DV = 128
CHUNK = 64
GLA_STAGE_LAG = 8
HGRN_STAGE_LAG = 6
DEEPNORM_ALPHA = (2.0 * DEPTH) ** 0.25
NORM_EPS = 1e-5

V7X_LANES = 128
V7X_SUBLANES = 8
V7X_VMEM_LIMIT_BYTES = 60 * 1024 * 1024
V7X_VMEM_COMPILER_TEMP_BYTES = 12 * 1024 * 1024

MM_ROW_CHUNK = 256
IN_PROJ_TN = 1024
FFN_DOWN_TK = 2816
LN_TM = 512
LN_COL_BLOCK = 1024
LN_ROW_BLOCK = 16
LN_STAGE_LAG = 2

S5_CHUNK = 16
S5_LANE_GROUPS = V7X_LANES // S5_GROUP
S5_TRI_COLS = 512


def _nbytes(shape, dtype):
    return math.prod(shape) * jnp.dtype(dtype).itemsize


def _params(*sem, buffers=None):
    limit = V7X_VMEM_LIMIT_BYTES
    if buffers is not None:
        need = sum(_nbytes(shape, dtype) * count for shape, dtype, count in buffers)
        limit = min(limit, need + V7X_VMEM_COMPILER_TEMP_BYTES)
    return pltpu.CompilerParams(dimension_semantics=sem, vmem_limit_bytes=limit)


def _stage_weight_tile(w_hbm, stage_ref, wb_ref, sem, *, layer, n_valid, nj):
    j = pl.program_id(0)
    tn = stage_ref.shape[1]
    last = nj - 1
    last_w = n_valid - last * tn

    def full_copy(jj):
        cols = pl.ds(pl.multiple_of(jj * tn, tn), tn)
        return pltpu.make_async_copy(w_hbm.at[layer, :, cols], stage_ref, sem)

    def last_copy():
        return pltpu.make_async_copy(w_hbm.at[layer, :, pl.ds(last * tn, last_w)],
                                     stage_ref.at[:, pl.ds(0, last_w)], sem)

    def start(jj):
        if last_w == tn:
            full_copy(jj).start()
        else:
            pl.when(jj < last)(lambda: full_copy(jj).start())
            pl.when(jj == last)(lambda: last_copy().start())

    @pl.when(pl.program_id(1) == 0)
    def _():
        pl.when(j == 0)(lambda: start(j))
        if last_w == tn:
            full_copy(j).wait()
            wb_ref[...] = stage_ref[...].astype(BF16)
        else:
            @pl.when(j < last)
            def _():
                full_copy(j).wait()
                wb_ref[...] = stage_ref[...].astype(BF16)

            @pl.when(j == last)
            def _():
                last_copy().wait()
                wb_ref[:, 0:last_w] = stage_ref[:, 0:last_w].astype(BF16)
                wb_ref[:, last_w:] = jnp.zeros((wb_ref.shape[0], tn - last_w), BF16)

        pl.when(j < last)(lambda: start(j + 1))


def _mm_kernel(a_ref, w_hbm, o_ref, stage_ref, wb_ref, sem, *, layer, n_valid, nj):
    _stage_weight_tile(w_hbm, stage_ref, wb_ref, sem, layer=layer, n_valid=n_valid, nj=nj)
    w = wb_ref[...]
    chunk = min(MM_ROW_CHUNK, a_ref.shape[0])
    for r0 in range(0, a_ref.shape[0], chunk):
        rows = slice(r0, r0 + chunk)
        o_ref[rows, :] = jnp.dot(a_ref[rows, :], w, preferred_element_type=F32).astype(o_ref.dtype)


def _matmul(a, w, layer, *, n, tm, tn, out_dtype, name):
    m, k = a.shape
    nj = n // tn
    return pl.pallas_call(
        functools.partial(_mm_kernel, layer=layer, n_valid=n, nj=nj),
        grid=(nj, m // tm),
        in_specs=[pl.BlockSpec((tm, k), lambda j, i: (i, 0)),
                  pl.BlockSpec(memory_space=pl.ANY)],
        out_specs=pl.BlockSpec((tm, tn), lambda j, i: (i, j)),
        out_shape=jax.ShapeDtypeStruct((m, n), out_dtype),
        scratch_shapes=[pltpu.VMEM((k, tn), F32), pltpu.VMEM((k, tn), BF16), pltpu.SemaphoreType.DMA],
        compiler_params=_params("arbitrary", "arbitrary", buffers=[
            ((tm, k), BF16, 2), ((k, tn), F32, 1), ((k, tn), BF16, 1), ((tm, tn), out_dtype, 2)]),
        name=name,
    )(a, w)


def _ffn_up_kernel(x_ref, wg_hbm, wu_hbm, wd_ref, o_ref, wdb_ref, sg_ref, su_ref, wgb_ref, wub_ref, sems,
                   *, layer, n_valid, nj, wd_blocks):
    _stage_weight_tile(wg_hbm, sg_ref, wgb_ref, sems.at[0], layer=layer, n_valid=n_valid, nj=nj)
    _stage_weight_tile(wu_hbm, su_ref, wub_ref, sems.at[1], layer=layer, n_valid=n_valid, nj=nj)
    tn = o_ref.shape[1]
    last = nj - 1
    last_w = n_valid - last * tn

    chunk = min(MM_ROW_CHUNK, x_ref.shape[0])

    def tile(width):
        wg, wu = wgb_ref[:, 0:width], wub_ref[:, 0:width]
        for r0 in range(0, x_ref.shape[0], chunk):
            rows = slice(r0, r0 + chunk)
            x = x_ref[rows, :]
            g = jnp.dot(x, wg, preferred_element_type=F32)
            u = jnp.dot(x, wu, preferred_element_type=F32)
            o_ref[rows, 0:width] = (g * jax.nn.sigmoid(g) * u).astype(o_ref.dtype)
            if width < tn:
                o_ref[rows, width:] = jnp.zeros((chunk, tn - width), o_ref.dtype)

    if last_w == tn:
        tile(tn)
    else:
        pl.when(pl.program_id(0) < last)(lambda: tile(tn))
        pl.when(pl.program_id(0) == last)(lambda: tile(last_w))
    step = pl.program_id(0) * pl.num_programs(1) + pl.program_id(1)
    wdb_ref[...] = jnp.where(step < wd_blocks, wd_ref[...], 0.0).astype(wdb_ref.dtype)


def _ffn_up(x_bf, wg, wu, wd, layer, *, tm, tn, n_out):
    m, k = x_bf.shape
    hidden, dout = wd.shape[1:]
    nj, ni = n_out // tn, m // tm
    slab = n_out // (nj * ni)
    assert slab * nj * ni == n_out and hidden % slab == 0 and slab % (2 * V7X_SUBLANES) == 0
    wd_blocks = hidden // slab
    hbm = pl.BlockSpec(memory_space=pl.ANY)
    return pl.pallas_call(
        functools.partial(_ffn_up_kernel, layer=layer, n_valid=wg.shape[2], nj=nj, wd_blocks=wd_blocks),
        grid=(nj, ni),
        in_specs=[pl.BlockSpec((tm, k), lambda j, i: (i, 0)), hbm, hbm,
                  pl.BlockSpec((None, slab, dout),
                               lambda j, i: (layer, jnp.minimum(j * ni + i, wd_blocks - 1), 0))],
        out_specs=[pl.BlockSpec((tm, tn), lambda j, i: (i, j)),
                   pl.BlockSpec((slab, dout), lambda j, i: (j * ni + i, 0))],
        out_shape=[jax.ShapeDtypeStruct((m, n_out), BF16), jax.ShapeDtypeStruct((n_out, dout), BF16)],
        scratch_shapes=[pltpu.VMEM((k, tn), F32), pltpu.VMEM((k, tn), F32),
                        pltpu.VMEM((k, tn), BF16), pltpu.VMEM((k, tn), BF16),
                        pltpu.SemaphoreType.DMA((2,))],
        compiler_params=_params("arbitrary", "arbitrary", buffers=[
            ((tm, k), BF16, 2), ((k, tn), F32, 2), ((k, tn), BF16, 2), ((tm, tn), BF16, 2),
            ((slab, dout), F32, 2), ((slab, dout), BF16, 2)]),
        name="ffn_up",
    )(x_bf, wg, wu, wd)


def _mm_resid_kernel(*refs, k_parts, nk):
    n_a = len(k_parts)
    a_refs = refs[:n_a]
    w_ref, *r_refs, o_ref = refs[n_a:]
    k = pl.program_id(2)

    def product():
        acc, k0 = None, 0
        for a_ref, kp in zip(a_refs, k_parts):
            d = jnp.dot(a_ref[...], w_ref[k0:k0 + kp, :], preferred_element_type=F32)
            acc = d if acc is None else acc + d
            k0 += kp
        return acc

    def residual():
        if len(r_refs) == 1:
            return r_refs[0][...]
        z_ref, mu_ref, rs_ref, g_ref, b_ref = r_refs
        nrep = z_ref.shape[1] // mu_ref.shape[1]
        mu, rs = jnp.tile(mu_ref[...], (1, nrep)), jnp.tile(rs_ref[...], (1, nrep))
        return (z_ref[...] - mu) * rs * g_ref[...] + b_ref[...]

    if nk == 1:
        o_ref[...] = product() + DEEPNORM_ALPHA * residual()
    else:
        @pl.when(k == 0)
        def _():
            o_ref[...] = product() + DEEPNORM_ALPHA * residual()

        @pl.when(k > 0)
        def _():
            o_ref[...] += product()


class _NormedRows(NamedTuple):
    z: jax.Array
    mean: jax.Array
    rstd: jax.Array
    gain: jax.Array
    bias: jax.Array


def _matmul_resid(a_parts, w, resid, *, tm, tn, tk, name):
    tile = pl.BlockSpec((tm, tn), lambda i, j, k: (i, j))
    if isinstance(resid, _NormedRows):
        m, n = resid.z.shape
        stat = pl.BlockSpec((tm, V7X_LANES), lambda i, j, k: (i, 0))
        vec = pl.BlockSpec((1, tn), lambda i, j, k: (0, j))
        r_ops = [resid.z, resid.mean, resid.rstd, resid.gain.reshape(1, n), resid.bias.reshape(1, n)]
        r_specs = [tile, stat, stat, vec, vec]
    else:
        m, n = resid.shape
        r_ops, r_specs = [resid], [tile]
    kdim = w.shape[0]
    if len(a_parts) > 1:
        assert tk == kdim
        k_parts = tuple(a.shape[1] for a in a_parts)
        a_specs = [pl.BlockSpec((tm, kp), lambda i, j, k: (i, 0)) for kp in k_parts]
    else:
        k_parts = (tk,)
        a_specs = [pl.BlockSpec((tm, tk), lambda i, j, k: (i, k))]
    nk = kdim // tk
    return pl.pallas_call(
        functools.partial(_mm_resid_kernel, k_parts=k_parts, nk=nk),
        grid=(m // tm, n // tn, nk),
        in_specs=a_specs + [pl.BlockSpec((tk, tn), lambda i, j, k: (k, j))] + r_specs,
        out_specs=tile,
        out_shape=jax.ShapeDtypeStruct((m, n), F32),
        compiler_params=_params("parallel", "parallel", "arbitrary"),
        name=name,
    )(*a_parts, w, *r_ops)


def _ln_kernel(z_ref, g_ref, b_ref, y_ref, mu_ref, rs_ref):
    inv_n = 1.0 / z_ref.shape[1]
    nblk = z_ref.shape[0] // LN_ROW_BLOCK
    lanes = mu_ref.shape[1]
    nrep = z_ref.shape[1] // lanes

    def block(r):
        start = r * LN_ROW_BLOCK
        return pl.ds(start if isinstance(r, int) else pl.multiple_of(start, LN_ROW_BLOCK), LN_ROW_BLOCK)

    def mean_rows(r):
        rows = block(r)
        mu = jnp.sum(z_ref[rows, :], axis=-1, keepdims=True) * inv_n
        mu_ref[rows, :] = jnp.broadcast_to(mu, (LN_ROW_BLOCK, lanes))

    def rstd_rows(r):
        rows = block(r)
        zc = z_ref[rows, :] - jnp.tile(mu_ref[rows, :], (1, nrep))
        var = jnp.sum(zc * zc, axis=-1, keepdims=True) * inv_n
        rs_ref[rows, :] = jnp.broadcast_to(lax.rsqrt(var + NORM_EPS), (LN_ROW_BLOCK, lanes))

    def norm_rows(r):
        rows = block(r)
        mu = jnp.tile(mu_ref[rows, :], (1, LN_COL_BLOCK // lanes))
        rs = jnp.tile(rs_ref[rows, :], (1, LN_COL_BLOCK // lanes))
        for c0 in range(0, z_ref.shape[1], LN_COL_BLOCK):
            cols = slice(c0, c0 + LN_COL_BLOCK)
            y = (z_ref[rows, cols] - mu) * rs * g_ref[:, cols] + b_ref[:, cols]
            y_ref[rows, cols] = y.astype(y_ref.dtype)

    lag = LN_STAGE_LAG

    def trip(r, with_mean=True, with_rstd=True, with_norm=True):
        if with_norm:
            norm_rows(r - 2 * lag)
        if with_rstd:
            rstd_rows(r - lag)
        if with_mean:
            mean_rows(r)

    for r in range(2 * lag):
        trip(r, with_rstd=r >= lag, with_norm=False)
    lax.fori_loop(2 * lag, nblk, lambda r, c: (trip(r), c)[1], 0, unroll=2)
    for r in range(nblk, nblk + 2 * lag):
        trip(r, with_mean=False, with_rstd=r - lag < nblk)


def _layernorm(z, gain, bias, *, tm, out_dtype):
    m, n = z.shape
    row_spec = pl.BlockSpec((tm, n), lambda i: (i, 0))
    vec_spec = pl.BlockSpec((1, n), lambda i: (0, 0))
    stat_spec = pl.BlockSpec((tm, V7X_LANES), lambda i: (i, 0))
    stat_shape = jax.ShapeDtypeStruct((m, V7X_LANES), F32)
    return pl.pallas_call(
        _ln_kernel,
        grid=(m // tm,),
        in_specs=[row_spec, vec_spec, vec_spec],
        out_specs=[row_spec, stat_spec, stat_spec],
        out_shape=[jax.ShapeDtypeStruct((m, n), out_dtype), stat_shape, stat_shape],
        compiler_params=_params("parallel"),
        name="layernorm",
    )(z, gain.reshape(1, n), bias.reshape(1, n))


def _norm_stage(z, gain, bias):
    y_bf, mean, rstd = _layernorm(z, gain, bias, tm=LN_TM, out_dtype=BF16)
    return _NormedRows(z, mean, rstd, gain, bias), y_bf


def _chunk_steps(nheads, lag, load, emit, st_ref, tril_bf, causal):
    nt = (((1,), (1,)), ((), ()))
    tn = (((0,), (0,)), ((), ()))
    live = {}

    def stage1(hd):
        q, k, v, la = load(hd)
        la_hi = la.astype(BF16)
        la_lo = (la - la_hi.astype(F32)).astype(BF16)
        b = (jnp.dot(tril_bf, la_hi, preferred_element_type=F32)
             + jnp.dot(tril_bf, la_lo, preferred_element_type=F32))
        live[hd] = (q, k, v, b)

    def stage2(hd):
        q, k, v, b = live[hd]
        c = q.shape[0]
        decay = jnp.exp(b[c - 1:c, :])
        q_dec = (q * jnp.exp(b)).astype(BF16)
        k_inv = k * jnp.exp(-b)
        k_end = (k_inv * decay).astype(BF16)
        v_bf = v.astype(BF16)
        scores = lax.dot_general(q_dec, k_inv.astype(BF16), nt, preferred_element_type=F32)
        kv = lax.dot_general(v_bf, k_end, tn, preferred_element_type=F32)
        inter = lax.dot_general(q_dec, st_ref[hd].astype(BF16), nt, preferred_element_type=F32)
        live[hd] = (scores, kv, inter, decay, v_bf)

    def stage3(hd):
        scores, kv, inter, decay, v_bf = live.pop(hd)
        sc = jnp.where(causal, scores, 0.0).astype(BF16)
        o = jnp.dot(sc, v_bf, preferred_element_type=F32) + inter
        st_ref[hd] = decay * st_ref[hd] + kv
        emit(hd, o)

    for t in range(nheads + 2 * lag):
        if t < nheads:
            stage1(t)
        if 0 <= t - lag < nheads:
            stage2(t - lag)
        if 0 <= t - 2 * lag < nheads:
            stage3(t - 2 * lag)


def _gated_rmsnorm(o, gain, gate):
    ms = jnp.mean(o * o, axis=-1, keepdims=True)
    return o * lax.rsqrt(ms + NORM_EPS) * gain * (gate * jax.nn.sigmoid(gate))


def _chunk_masks():
    row = lax.broadcasted_iota(jnp.int32, (CHUNK, CHUNK), 0)
    col = lax.broadcasted_iota(jnp.int32, (CHUNK, CHUNK), 1)
    causal = row >= col
    return jnp.where(causal, 1.0, 0.0).astype(BF16), causal


def _log_sigmoid(z):
    return jnp.minimum(z, 0.0) - jnp.log1p(jnp.exp(-jnp.abs(z)))


def _gla_kernel(q_ref, k_ref, v_ref, g_ref, alr_ref, wa_ref, ba_ref, gain_ref, o_ref, st_ref,
                *, hb, nchunks, unroll):
    @pl.when(pl.program_id(2) == 0)
    def _():
        st_ref[...] = jnp.zeros_like(st_ref)

    tril_bf, causal = _chunk_masks()
    gain = gain_ref[...]
    ba = ba_ref[...]
    wa = wa_ref[...]

    def body(c, carry):
        rows = pl.ds(pl.multiple_of(c * CHUNK, CHUNK), CHUNK)
        z = jnp.dot(alr_ref[rows, :].astype(BF16), wa, preferred_element_type=F32) + ba
        la_all = _log_sigmoid(z) / GLA_TAU
        kcs = [slice(hd * GLA_DK, (hd + 1) * GLA_DK) for hd in range(hb)]
        vcs = [slice(hd * GLA_DV, (hd + 1) * GLA_DV) for hd in range(hb)]

        def load(hd):
            kc = kcs[hd]
            return q_ref[rows, kc] * (GLA_DK ** -0.5), k_ref[rows, kc], v_ref[rows, vcs[hd]], la_all[:, kc]

        def emit(hd, o):
            vc = vcs[hd]
            o_ref[rows, vc] = _gated_rmsnorm(o, gain, g_ref[rows, vc]).astype(o_ref.dtype)

        _chunk_steps(hb, GLA_STAGE_LAG, load, emit, st_ref, tril_bf, causal)
        return carry

    lax.fori_loop(0, nchunks, body, 0, unroll=unroll)


def _gla(h, alr, w_alpha, b_alpha, norm_g, *, bsz, seq, col_q, col_k, col_v, col_g, tt, hb, unroll):
    n = bsz * seq
    nt = seq // tt
    kw, vw = hb * GLA_DK, hb * GLA_DV
    hg = GLA_HEADS // hb

    def rows(b, g, t):
        return b * nt + t

    return pl.pallas_call(
        functools.partial(_gla_kernel, hb=hb, nchunks=tt // CHUNK, unroll=unroll),
        grid=(bsz, hg, nt),
        in_specs=[pl.BlockSpec((tt, kw), lambda b, g, t: (rows(b, g, t), col_q // kw + g)),
                  pl.BlockSpec((tt, kw), lambda b, g, t: (rows(b, g, t), col_k // kw + g)),
                  pl.BlockSpec((tt, vw), lambda b, g, t: (rows(b, g, t), col_v // vw + g)),
                  pl.BlockSpec((tt, vw), lambda b, g, t: (rows(b, g, t), col_g // vw + g)),
                  pl.BlockSpec((tt, V7X_LANES), lambda b, g, t: (rows(b, g, t), 0)),
                  pl.BlockSpec((V7X_LANES, kw), lambda b, g, t: (0, g)),
                  pl.BlockSpec((1, kw), lambda b, g, t: (0, g)),
                  pl.BlockSpec((1, GLA_DV), lambda b, g, t: (0, 0))],
        out_specs=pl.BlockSpec((tt, vw), lambda b, g, t: (rows(b, g, t), g)),
        out_shape=jax.ShapeDtypeStruct((n, GLA_HEADS * GLA_DV), BF16),
        scratch_shapes=[pltpu.VMEM((hb, GLA_DV, GLA_DK), F32)],
        compiler_params=_params("parallel", "parallel", "arbitrary"),
        name="gla",
    )(h, h, h, h, alr, w_alpha, b_alpha, norm_g)


def _hgrn_kernel(q_ref, f_ref, i_ref, g_ref, tab_ref, gain_ref, o_ref, st_ref, *, layer, hb, nchunks,
                 unroll):
    @pl.when(pl.program_id(2) == 0)
    def _():
        st_ref[...] = jnp.zeros_like(st_ref)

    tab = tab_ref[...]
    e = jnp.exp(tab - jnp.max(tab, axis=0, keepdims=True))
    soft = e / jnp.sum(e, axis=0, keepdims=True)
    lb_all = jnp.sum(soft[0:layer + 1], axis=0, keepdims=True) - soft[0:1]
    tril_bf, causal = _chunk_masks()
    gain = gain_ref[...]

    def body(c, carry):
        rows = pl.ds(pl.multiple_of(c * CHUNK, CHUNK), CHUNK)
        kcs = [slice(hd * HG_DK, (hd + 1) * HG_DK) for hd in range(hb)]
        vcs = [slice(hd * HG_DV, (hd + 1) * HG_DV) for hd in range(hb)]

        def load(hd):
            kc = kcs[hd]
            lb = lb_all[:, kc]
            f = lb + (1.0 - lb) * jax.nn.sigmoid(f_ref[rows, kc])
            iv = i_ref[rows, vcs[hd]]
            return q_ref[rows, kc], 1.0 - f, iv * jax.nn.sigmoid(iv), jnp.log(f)

        def emit(hd, o):
            vc = vcs[hd]
            o_ref[rows, vc] = _gated_rmsnorm(o, gain, g_ref[rows, vc]).astype(o_ref.dtype)

        _chunk_steps(hb, HGRN_STAGE_LAG, load, emit, st_ref, tril_bf, causal)
        return carry

    lax.fori_loop(0, nchunks, body, 0, unroll=unroll)


def _hgrn(h, lb_table, norm_g, *, layer, bsz, seq, heads, tt, hb, unroll):
    n = bsz * seq
    nt = seq // tt
    w = hb * HG_DK
    hg = heads // hb

    def spec(seg):
        return pl.BlockSpec((tt, w), lambda b, g, t: (b * nt + t, seg * hg + g))

    return pl.pallas_call(
        functools.partial(_hgrn_kernel, layer=layer, hb=hb, nchunks=tt // CHUNK, unroll=unroll),
        grid=(bsz, hg, nt),
        in_specs=[spec(0), spec(1), spec(2), spec(3),
                  pl.BlockSpec((DEPTH, w), lambda b, g, t: (0, g)),
                  pl.BlockSpec((1, HG_DV), lambda b, g, t: (0, 0))],
        out_specs=pl.BlockSpec((tt, w), lambda b, g, t: (b * nt + t, g)),
        out_shape=jax.ShapeDtypeStruct((n, heads * HG_DV), BF16),
        scratch_shapes=[pltpu.VMEM((hb, HG_DV, HG_DK), F32)],
        compiler_params=_params("parallel", "parallel", "arbitrary"),
        name="hgrn2",
    )(h, h, h, h, lb_table, norm_g)


def _s5_prep_cast_kernel(lr_row_ref, li_row_ref, lr_col_ref, li_col_ref, ldt_ref,
                         btr_ref, bti_ref, ctr_ref, cti_ref, src_ref, wgate_ref,
                         w1_ref, f_ref, coef_ref, dst_ref, gate_ref):
    _s5_prep_kernel(lr_row_ref, li_row_ref, lr_col_ref, li_col_ref, ldt_ref,
                    btr_ref, bti_ref, ctr_ref, cti_ref, w1_ref, f_ref, coef_ref)
    x = src_ref[...].astype(dst_ref.dtype)
    dst_ref[...] = x
    gate_ref[...] = jnp.dot(x, wgate_ref[...], preferred_element_type=F32)


def _s5_prep_kernel(lr_row_ref, li_row_ref, lr_col_ref, li_col_ref, ldt_ref,
                    btr_ref, bti_ref, ctr_ref, cti_ref, w1_ref, f_ref, coef_ref):
    L, J, P = S5_CHUNK, S5_GROUP, S5_STATE
    hi = lax.Precision.HIGHEST
    dt = jnp.exp(ldt_ref[0])
    lr, li = lr_row_ref[0], li_row_ref[0]
    npow = ((L - 1) - lax.broadcasted_iota(jnp.int32, (L, 2 * P), 0)).astype(F32)
    mag = jnp.exp(npow * (lr * dt))
    th = npow * (li * dt)
    ar, ai = mag * jnp.cos(th), mag * jnp.sin(th)
    half = lax.broadcasted_iota(jnp.int32, (L, 2 * P), 1) < P
    a1r, a1i = ar[L - 2:L - 1], ai[L - 2:L - 1]
    nr, ni = a1r - 1.0, a1i
    den = lr * lr + li * li
    fr, fi = (nr * lr + ni * li) / den, (ni * lr - nr * li) / den
    btr, bti = btr_ref[0], bti_ref[0]
    bbr, bbi = fr * btr - fi * bti, fr * bti + fi * btr
    pa, pb = jnp.where(half, ar, ai), jnp.where(half, -ai, ar)
    for s in range(L):
        rows = slice(s * J, (s + 1) * J)
        w1_ref[0, rows, L * J:] = (bbr * pa[s:s + 1] + bbi * pb[s:s + 1]).astype(BF16)
    alr_, ali_ = ar[0:1] * a1r - ai[0:1] * a1i, ar[0:1] * a1i + ai[0:1] * a1r
    half1 = half[0:1]
    coef_ref[0, 0:1, :] = alr_
    coef_ref[0, 1:2, :] = jnp.where(half1, -ali_, ali_)
    coef_ref[0, 2:, :] = jnp.zeros((V7X_SUBLANES - 2, 2 * P), F32)

    lrc, lic = lr_col_ref[0], li_col_ref[0]
    lane = lax.broadcasted_iota(jnp.int32, (P, L * J), 1)
    lag = jnp.right_shift(lane, int(math.log2(J))).astype(F32)
    magq = jnp.exp(lag * (lrc * dt))
    thq = lag * (lic * dt)
    aqr, aqi = magq * jnp.cos(thq), magq * jnp.sin(thq)
    cr, ci = ctr_ref[0], cti_ref[0]
    qr, qi = cr * aqr - ci * aqi, cr * aqi + ci * aqr
    r = (jnp.dot(bbr[:, :P], qr, precision=hi, preferred_element_type=F32)
         - jnp.dot(bbi[:, :P], qi, precision=hi, preferred_element_type=F32))
    lane_r = lax.broadcasted_iota(jnp.int32, (J, L * J), 1)
    for s in range(L):
        blk = r if s == 0 else jnp.where(lane_r >= s * J, pltpu.roll(r, s * J, 1), 0.0)
        w1_ref[0, s * J:(s + 1) * J, 0:L * J] = blk.astype(BF16)
    c1r, c1i = aqr[:, J:J + 1], aqi[:, J:J + 1]
    f_ref[0, 0:P, :] = (qr * c1r - qi * c1i).astype(BF16)
    f_ref[0, P:, :] = (-(qr * c1i + qi * c1r)).astype(BF16)


def _s5_prep(lam_re, lam_im, log_dt, b_re, b_im, c_re, c_im, cast_src=None, gate_w=None):
    g = lam_re.shape[0]
    P, J, L = S5_STATE, S5_GROUP, S5_CHUNK
    dbl = lambda a: jnp.concatenate([a, a], axis=-1)
    lr_row = dbl(lam_re).reshape(g, 1, 2 * P)
    li_row = dbl(lam_im).reshape(g, 1, 2 * P)
    lr_col = lam_re.reshape(g, P, 1)
    li_col = lam_im.reshape(g, P, 1)
    ldt = log_dt.reshape(g, 1, 1)
    btr = dbl(jnp.swapaxes(b_re, 1, 2))
    bti = dbl(jnp.swapaxes(b_im, 1, 2))
    ctr = jnp.tile(jnp.swapaxes(c_re, 1, 2), (1, 1, L))
    cti = jnp.tile(jnp.swapaxes(c_im, 1, 2), (1, 1, L))
    blk = lambda *s: pl.BlockSpec((1,) + s, lambda i: (i, 0, 0))
    operands = [lr_row, li_row, lr_col, li_col, ldt, btr, bti, ctr, cti]
    in_specs = [blk(1, 2 * P), blk(1, 2 * P), blk(P, 1), blk(P, 1), blk(1, 1),
                blk(J, 2 * P), blk(J, 2 * P), blk(P, L * J), blk(P, L * J)]
    out_specs = [blk(L * J, L * J + 2 * P), blk(2 * P, L * J), blk(V7X_SUBLANES, 2 * P)]
    out_shape = [jax.ShapeDtypeStruct((g, L * J, L * J + 2 * P), BF16),
                 jax.ShapeDtypeStruct((g, 2 * P, L * J), BF16),
                 jax.ShapeDtypeStruct((g, V7X_SUBLANES, 2 * P), F32)]
    body = _s5_prep_kernel
    if cast_src is not None:
        m, d = cast_src.shape
        slab = pl.BlockSpec((m // g, d), lambda i: (i, 0))
        gate_slab = pl.BlockSpec((m // g, gate_w.shape[1]), lambda i: (i, 0))
        operands += [cast_src, gate_w]
        in_specs += [slab, pl.BlockSpec(gate_w.shape, lambda i: (0, 0))]
        out_specs += [slab, gate_slab]
        out_shape += [jax.ShapeDtypeStruct((m, d), BF16), jax.ShapeDtypeStruct((m, gate_w.shape[1]), F32)]
        body = _s5_prep_cast_kernel
    return pl.pallas_call(body, grid=(g,), in_specs=in_specs, out_specs=out_specs, out_shape=out_shape,
                          compiler_params=_params("parallel"), name="s5_prep")(*operands)


def _s5_assemble_kernel(w1_ref, f_ref, coef_ref, wbig_ref, fbig_ref, cbig_ref):
    L, J, P, Q = S5_CHUNK, S5_GROUP, S5_STATE, S5_LANE_GROUPS
    lanes = Q * J
    width = L * lanes
    nstate = 2 * Q * P
    jbits, pbits = int(math.log2(J)), int(math.log2(P))

    def iota(shape, axis):
        return lax.broadcasted_iota(jnp.int32, shape, axis)

    a, b = iota((L * J, width), 0), iota((L * J, width), 1)
    sel_sc = jnp.where(((b >> int(math.log2(lanes))) == (a >> jbits)) & ((b & (J - 1)) == (a & (J - 1))),
                       1.0, 0.0).astype(BF16)
    a, b = iota((2 * P, nstate), 0), iota((2 * P, nstate), 1)
    sel_st = jnp.where(((b >> int(math.log2(Q * P))) == (a >> pbits)) & ((b & (P - 1)) == (a & (P - 1))),
                       1.0, 0.0).astype(BF16)
    col_group_sc = (iota((lanes, width), 1) >> jbits) & (Q - 1)
    col_group_st = (iota((lanes, nstate), 1) >> pbits) & (Q - 1)
    row_group_j = iota((lanes, 1), 0) >> jbits

    r_stack = jnp.concatenate([w1_ref[g, 0:J, 0:L * J] for g in range(Q)], axis=0)
    r_big = jnp.dot(r_stack, sel_sc, preferred_element_type=F32)
    r_big = jnp.where(col_group_sc == row_group_j, r_big, 0.0).astype(BF16)
    for s in range(L):
        rows = slice(s * lanes, (s + 1) * lanes)
        if s:
            wbig_ref[0, rows, 0:s * lanes] = jnp.zeros((lanes, s * lanes), BF16)
        wbig_ref[0, rows, s * lanes:width] = r_big[:, 0:width - s * lanes]
        e_stack = jnp.concatenate([w1_ref[g, s * J:(s + 1) * J, L * J:] for g in range(Q)], axis=0)
        e_big = jnp.dot(e_stack, sel_st, preferred_element_type=F32)
        wbig_ref[0, rows, width:] = jnp.where(col_group_st == row_group_j, e_big, 0.0).astype(BF16)

    row_in_pair = iota((lanes, 1), 0) >> pbits
    for m in range(nstate // lanes):
        ri, g0 = divmod(m * (lanes // P), Q)
        f_stack = jnp.concatenate([f_ref[g0 + t, ri * P:(ri + 1) * P, :] for t in range(lanes // P)], axis=0)
        f_big = jnp.dot(f_stack, sel_sc, preferred_element_type=F32)
        fbig_ref[0, m * lanes:(m + 1) * lanes, :] = jnp.where(
            col_group_sc == g0 + row_in_pair, f_big, 0.0).astype(BF16)

    low = iota((V7X_SUBLANES, lanes), 1) < P
    for k in range(Q // 2):
        c0, c1 = coef_ref[2 * k], coef_ref[2 * k + 1]
        row_is_a2 = iota((V7X_SUBLANES, lanes), 0) == 1
        re_half = jnp.where(low, c0, jnp.where(row_is_a2, -c1, c1))
        im_half = jnp.where(low, jnp.where(row_is_a2, -c0, c0), c1)
        cbig_ref[0, :, k * lanes:(k + 1) * lanes] = re_half
        cbig_ref[0, :, Q * P + k * lanes:Q * P + (k + 1) * lanes] = im_half


def _s5_assemble(w1, fmat, coef):
    g = w1.shape[0]
    L, J, P, Q = S5_CHUNK, S5_GROUP, S5_STATE, S5_LANE_GROUPS
    nb = g // Q
    width, nstate = L * Q * J, 2 * Q * P
    grp = lambda *s: pl.BlockSpec((Q,) + s, lambda i: (i, 0, 0))
    out = lambda *s: pl.BlockSpec((1,) + s, lambda i: (i, 0, 0))
    return pl.pallas_call(
        _s5_assemble_kernel,
        grid=(nb,),
        in_specs=[grp(L * J, L * J + 2 * P), grp(2 * P, L * J), grp(V7X_SUBLANES, 2 * P)],
        out_specs=[out(width, width + nstate), out(nstate, width), out(V7X_SUBLANES, nstate)],
        out_shape=[jax.ShapeDtypeStruct((nb, width, width + nstate), BF16),
                   jax.ShapeDtypeStruct((nb, nstate, width), BF16),
                   jax.ShapeDtypeStruct((nb, V7X_SUBLANES, nstate), F32)],
        compiler_params=_params("parallel"),
        name="s5_assemble",
    )(w1, fmat, coef)


def _s5_main_kernel(u_ref, w_ref, f_ref, coef_ref, o_ref, ys_ref, sp_ref, st_ref, *, nchunk, nb):
    width = u_ref.shape[2]
    half = st_ref.shape[1] // 2

    @pl.when(pl.program_id(1) == 0)
    def _():
        st_ref[...] = jnp.zeros_like(st_ref)

    for c0 in range(0, width, S5_TRI_COLS):
        c1 = c0 + S5_TRI_COLS
        ys_ref[:, c0:c1] = jnp.dot(u_ref[0, :, 0:c1], w_ref[0, 0:c1, c0:c1], preferred_element_type=F32)
    ys_ref[:, width:] = jnp.dot(u_ref[0], w_ref[0, :, width:], preferred_element_type=F32)
    a1, a2 = coef_ref[0, 0:1, :], coef_ref[0, 1:2, :]

    def body(c, s):
        rows = pl.ds(pl.multiple_of(c * nb, nb), nb)
        sp_ref[rows, :] = s
        swapped = jnp.concatenate([s[:, half:], s[:, :half]], axis=1)
        return a1 * s + a2 * swapped + ys_ref[rows, width:]

    st_ref[...] = lax.fori_loop(0, nchunk, body, st_ref[...], unroll=4)
    y = ys_ref[:, 0:width] + jnp.dot(sp_ref[...].astype(BF16), f_ref[0], preferred_element_type=F32)
    o_ref[0] = y.astype(o_ref.dtype)


def _s5_main(u_c, wbig, fbig, cbig, *, nb, rows_per_step):
    nblk, rows, width = u_c.shape
    nstate = fbig.shape[1]
    nsteps = rows // rows_per_step
    fixed = lambda *s: pl.BlockSpec((1,) + s, lambda i, r: (i, 0, 0))
    moving = pl.BlockSpec((1, rows_per_step, width), lambda i, r: (i, r, 0))
    return pl.pallas_call(
        functools.partial(_s5_main_kernel, nchunk=rows_per_step // nb, nb=nb),
        grid=(nblk, nsteps),
        in_specs=[moving, fixed(width, width + nstate), fixed(nstate, width),
                  fixed(V7X_SUBLANES, nstate)],
        out_specs=moving,
        out_shape=jax.ShapeDtypeStruct((nblk, rows, width), BF16),
        scratch_shapes=[pltpu.VMEM((rows_per_step, width + nstate), F32),
                        pltpu.VMEM((rows_per_step, nstate), F32),
                        pltpu.VMEM((nb, nstate), F32)],
        compiler_params=_params("parallel", "arbitrary"),
        name="s5_main",
    )(u_c, wbig, fbig, cbig)


def _s5_glu_kernel(ys_ref, u_ref, d_ref, w_ref, o_ref):
    y = jax.nn.gelu(ys_ref[...].astype(F32) + d_ref[...] * u_ref[...], approximate=True)
    z = jnp.dot(y.astype(BF16), w_ref[...], preferred_element_type=F32)
    o_ref[...] = (y * jax.nn.sigmoid(z)).astype(o_ref.dtype)


def _s5_glu(y_ssm, h, d_skip, w_glu, *, tm):
    m, n = y_ssm.shape
    return pl.pallas_call(
        _s5_glu_kernel,
        grid=(m // tm,),
        in_specs=[pl.BlockSpec((tm, n), lambda i: (i, 0)),
                  pl.BlockSpec((tm, n), lambda i: (i, 0)),
                  pl.BlockSpec((1, n), lambda i: (0, 0)),
                  pl.BlockSpec((n, n), lambda i: (0, 0))],
        out_specs=pl.BlockSpec((tm, n), lambda i: (i, 0)),
        out_shape=jax.ShapeDtypeStruct((m, n), BF16),
        compiler_params=_params("parallel"),
        name="s5_glu",
    )(y_ssm, h, d_skip.reshape(1, n), w_glu)


def _pad_cols(w, n):
    return jnp.pad(w, ((0, 0), (0, n - w.shape[1])))


def _ffn_block(x_res, x_bf, w_gate, w_up, w_down, layer, ln_g, ln_b, *, final):
    hidden = w_gate.shape[2]
    hpad = -(-hidden // FFN_DOWN_TK) * FFN_DOWN_TK
    a, wd = _ffn_up(x_bf, w_gate, w_up, w_down, layer, tm=1024, tn=512, n_out=hpad)
    z = _matmul_resid([a], wd, x_res, tm=1024, tn=1024, tk=FFN_DOWN_TK, name="ffn_down")
    if final:
        return _layernorm(z, ln_g, ln_b, tm=LN_TM, out_dtype=F32)[0], None
    return _norm_stage(z, ln_g, ln_b)


def _even_layer(x_res, x_bf, bsz, seq, e, w_in, w_out, lam_re, lam_im, log_dt, b_re, b_im, c_re, c_im,
                d_skip, w_glu, w_alpha, b_alpha, norm_g, ln_g, ln_b):
    n = bsz * seq
    s5w = d_skip.shape[0]
    groups = s5w // S5_GROUP
    kw = GLA_HEADS * GLA_DK
    vw = GLA_HEADS * GLA_DV
    main = s5w + 2 * kw + 2 * vw
    w_gate = _pad_cols(w_in[e, :, main:], V7X_LANES)
    if x_bf is None:
        w1, fmat, coef, x_bf, alr = _s5_prep(lam_re, lam_im, log_dt, b_re, b_im, c_re, c_im,
                                             cast_src=x_res, gate_w=w_gate.astype(BF16))
    else:
        w1, fmat, coef = _s5_prep(lam_re, lam_im, log_dt, b_re, b_im, c_re, c_im)
        alr = _matmul(x_bf, w_gate[None], 0, n=V7X_LANES, tm=1024, tn=V7X_LANES, out_dtype=F32,
                      name="even_in_gate")
    h = _matmul(x_bf, w_in, e, n=main, tm=1024, tn=IN_PROJ_TN, out_dtype=F32, name="even_in")

    wbig, fbig, cbig = _s5_assemble(w1, fmat, coef)
    nchunk = seq // S5_CHUNK
    nblk = s5w // V7X_LANES
    u_c = (h[:, :s5w].astype(BF16).reshape(bsz, nchunk, S5_CHUNK, nblk, V7X_LANES)
           .transpose(3, 1, 0, 2, 4).reshape(nblk, nchunk * bsz, S5_CHUNK * V7X_LANES))
    y_c = _s5_main(u_c, wbig, fbig, cbig, nb=bsz, rows_per_step=256)
    y_ssm = (y_c.reshape(nblk, nchunk, bsz, S5_CHUNK, V7X_LANES)
             .transpose(2, 1, 3, 0, 4).reshape(n, s5w))
    y_s5 = _s5_glu(y_ssm, h, d_skip, w_glu.astype(BF16), tm=512)

    wa = jnp.pad(w_alpha, ((0, V7X_LANES - w_alpha.shape[0]), (0, 0))).astype(BF16)
    y_gla = _gla(h, alr, wa, b_alpha.reshape(1, kw), norm_g.reshape(1, GLA_DV), bsz=bsz, seq=seq,
                 col_q=s5w, col_k=s5w + kw, col_v=s5w + 2 * kw, col_g=s5w + 2 * kw + vw,
                 tt=512, hb=8, unroll=2)

    z = _matmul_resid([y_s5, y_gla], w_out.astype(BF16), x_res, tm=1024, tn=1024, tk=w_out.shape[0],
                      name="even_out")
    return _norm_stage(z, ln_g, ln_b)


def _odd_layer(x_res, x_bf, bsz, seq, layer, o, w_in, w_out, lb_table, norm_g, ln_g, ln_b):
    heads = w_out.shape[0] // HG_DV
    h = _matmul(x_bf, w_in, o, n=w_in.shape[2], tm=1024, tn=IN_PROJ_TN, out_dtype=F32, name="odd_in")
    y = _hgrn(h, lb_table, norm_g.reshape(1, HG_DV), layer=layer, bsz=bsz, seq=seq, heads=heads,
              tt=512, hb=16, unroll=2)
    z = _matmul_resid([y], w_out.astype(BF16), x_res, tm=1024, tn=1024, tk=w_out.shape[0], name="odd_out")
    return _norm_stage(z, ln_g, ln_b)


def kernel(x, ev_w_in, ev_w_out, s5_lam_re, s5_lam_im, s5_log_dt, s5_b_re, s5_b_im, s5_c_re, s5_c_im,
           s5_d, s5_w_glu, gla_w_alpha, gla_b_alpha, gla_norm_g, od_w_in, od_w_out, hg_lb_table,
           hg_norm_g, ln_mix_g, ln_mix_b, ln_ffn_g, ln_ffn_b, ffn_w_gate, ffn_w_up, ffn_w_down):
    bsz, seq, d = x.shape
    depth = ln_mix_g.shape[0]
    assert depth == DEPTH
    xf = x.reshape(bsz * seq, d)
    xb = None
    for layer in range(depth):
        if layer % 2 == 0:
            e = layer // 2
            xf, xb = _even_layer(xf, xb, bsz, seq, e, ev_w_in, ev_w_out[e], s5_lam_re[e], s5_lam_im[e],
                                 s5_log_dt[e], s5_b_re[e], s5_b_im[e], s5_c_re[e], s5_c_im[e], s5_d[e],
                                 s5_w_glu[e], gla_w_alpha[e], gla_b_alpha[e], gla_norm_g[e],
                                 ln_mix_g[layer], ln_mix_b[layer])
        else:
            o = layer // 2
            xf, xb = _odd_layer(xf, xb, bsz, seq, layer, o, od_w_in, od_w_out[o], hg_lb_table,
                                hg_norm_g[o], ln_mix_g[layer], ln_mix_b[layer])
        xf, xb = _ffn_block(xf, xb, ffn_w_gate, ffn_w_up, ffn_w_down, layer,
                            ln_ffn_g[layer], ln_ffn_b[layer], final=layer == depth - 1)
    return xf.reshape(bsz, seq, d).astype(x.dtype)
```
